```python
import jax, jax.numpy as jnp
from jax import lax
import numpy as np

D_MODEL = 1024
BATCH = 1
SEQ = 16384
DEPTH = 4

GRID_W = 64
CTX_LEN = 256
N_MIXERS = 3
D_FF = 2816
N_MOD = 9
NORM_EPS = 1e-6

GLA_HEADS = 4
GLA_DK = 128
GLA_DV = 256
GLA_RANK = 16
GLA_LOGIT_NORM = 16.0
GLA_CHUNK = 64
GLA_HK = GLA_HEADS * GLA_DK
GLA_HV = GLA_HEADS * GLA_DV
GLA_IN = 2 * GLA_HK + 2 * GLA_HV + 2 * GLA_RANK

D_RNN = 1536
RG_BLOCKS = 12
RG_BS = D_RNN // RG_BLOCKS
RG_CONV = 4
RG_C = 8.0

SWA_HEADS = 16
SWA_KV = 4
SWA_G = SWA_HEADS // SWA_KV
HEAD_DIM = 64
WINDOW = 128
Q_BLOCK = 128
ROT_AXIS = HEAD_DIM // 2
ROPE_BASE = 10000.0
SWA_QKV = (SWA_HEADS + 2 * SWA_KV) * HEAD_DIM

kernel_name = 'hybrid_gla_rglru_swa_macaron_dit'


def _count(m):
    return len(range(m, DEPTH, N_MIXERS))


def _rmsnorm(x, g):
    x32 = x.astype(jnp.float32)
    y = x32 * lax.rsqrt(jnp.mean(x32 * x32, axis=-1, keepdims=True) + NORM_EPS)
    return (y * g.astype(jnp.float32)).astype(x.dtype)


def _modulate(x, g, shift, scale):
    return _rmsnorm(x, g) * (1 + scale[:, None, :]) + shift[:, None, :]


def _swiglu(h, w_in, w_out):
    gate, up = jnp.split(h @ w_in, 2, axis=-1)
    return (jax.nn.silu(gate) * up) @ w_out


def _gla_states(k, v, g, s0):
    B, T, H, dk = k.shape
    n = T // GLA_CHUNK
    kc = k.reshape(B, n, GLA_CHUNK, H, dk)
    vc = v.reshape(B, n, GLA_CHUNK, H, -1)
    b = jnp.cumsum(g.reshape(B, n, GLA_CHUNK, H, dk), axis=2)
    b_last = b[:, :, -1]
    upd = jnp.einsum('bnchd,bnche->bnhde', kc * jnp.exp(b_last[:, :, None] - b), vc)
    decay = jnp.exp(b_last)

    def step(s, xs):
        dec, u = xs
        return dec[..., None] * s + u, s

    s_fin, s_before = lax.scan(step, s0, (jnp.moveaxis(decay, 1, 0), jnp.moveaxis(upd, 1, 0)))
    return jnp.moveaxis(s_before, 0, 1), s_fin, b


def _gla_output(q, k, v, b, s_before):
    B, T, H, dk = q.shape
    n = T // GLA_CHUNK
    qc = q.reshape(B, n, GLA_CHUNK, H, dk) * jnp.exp(b)
    kc = k.reshape(B, n, GLA_CHUNK, H, dk) * jnp.exp(-b)
    vc = v.reshape(B, n, GLA_CHUNK, H, -1)
    att = jnp.einsum('bnchd,bnshd->bnhcs', qc, kc)
    lower = jnp.tril(jnp.ones((GLA_CHUNK, GLA_CHUNK), dtype=bool))
    att = jnp.where(lower, att, 0.0)
    o = jnp.einsum('bnhcs,bnshe->bnche', att, vc) + jnp.einsum('bnchd,bnhde->bnche', qc, s_before)
    return o.reshape(B, T, H, -1)


def _gla_dir(q, k, v, g, s0, reverse):
    if reverse:
        q, k, v, g = (jnp.flip(t, 1) for t in (q, k, v, g))
    s_before, s_fin, b = _gla_states(k, v, g, s0)
    o = _gla_output(q, k, v, b, s_before)
    return (jnp.flip(o, 1) if reverse else o), s_fin


def _gla_mixer(h_lat, h_ctx, w_in, w_a2, b_a, norm_g, w_out, ctx_out):
    def project(h):
        B, T, _ = h.shape
        q, k, v, og, lr = jnp.split(h @ w_in, [GLA_HK, 2 * GLA_HK, 2 * GLA_HK + GLA_HV, 2 * GLA_HK + 2 * GLA_HV], axis=-1)
        q = q.reshape(B, T, GLA_HEADS, GLA_DK) * (GLA_DK ** -0.5)
        k = k.reshape(B, T, GLA_HEADS, GLA_DK)
        v = v.reshape(B, T, GLA_HEADS, GLA_DV)
        lr = lr.reshape(B, T, 2, GLA_RANK)
        logit = jnp.einsum('btzr,zrk->btzk', lr, w_a2) + b_a
        g = (jax.nn.log_sigmoid(logit.astype(jnp.float32)) / GLA_LOGIT_NORM).reshape(B, T, 2, GLA_HEADS, GLA_DK)
        return q, k, v, og, g[:, :, 0], g[:, :, 1]

    qc, kc, vc, ogc, gcf, gcb = project(h_ctx)
    q, k, v, og, gf, gb = project(h_lat)
    B = h_lat.shape[0]
    s0 = jnp.zeros((B, GLA_HEADS, GLA_DK, GLA_DV), jnp.float32)
    if ctx_out:
        oc_f, sc_f = _gla_dir(qc, kc, vc, gcf, s0, False)
        oc_b, sc_b = _gla_dir(qc, kc, vc, gcb, s0, True)
    else:
        sc_f = _gla_states(kc, vc, gcf, s0)[1]
        sc_b = _gla_states(jnp.flip(kc, 1), jnp.flip(vc, 1), jnp.flip(gcb, 1), s0)[1]
    o_f, _ = _gla_dir(q, k, v, gf, sc_f, False)
    o_b, _ = _gla_dir(q, k, v, gb, sc_b, True)

    def finish(o, gate, dtype):
        Bo, T = o.shape[:2]
        o = _rmsnorm(o, norm_g).reshape(Bo, T, GLA_HV) * jax.nn.silu(gate)
        return (o @ w_out).astype(dtype)

    y_lat = finish(o_f + o_b, og, h_lat.dtype)
    y_ctx = finish(oc_f + oc_b, ogc, h_ctx.dtype) if ctx_out else None
    return y_lat, y_ctx


def _dwconv(x, w, b):
    y = lax.conv_general_dilated(x, w[:, None, :].astype(x.dtype), window_strides=(1,), padding=[(2, 1)],
                                 dimension_numbers=('NWC', 'WIO', 'NWC'), feature_group_count=x.shape[-1])
    return y + b


def _linear_scan(a, u, h0):
    u = u.at[:, 0].add(a[:, 0] * h0)

    def comb(l, r):
        return l[0] * r[0], r[0] * l[1] + r[1]

    _, h = lax.associative_scan(comb, (a, u), axis=1)
    return h


def _rglru_mixer(h_lat, h_ctx, w_in, conv_w, conv_b, w_a, b_a, w_x, b_x, lam, w_out, ctx_out):
    def branch(h):
        B, T, _ = h.shape
        gate, xr = jnp.split(h @ w_in, 2, axis=-1)
        xr = _dwconv(xr, conv_w, conv_b)
        xb = xr.reshape(B, T, RG_BLOCKS, RG_BS)
        r = jax.nn.sigmoid(jnp.einsum('btnj,znjk->zbtnk', xb, w_a).reshape(2, B, T, D_RNN) + b_a[:, None, None])
        i = jax.nn.sigmoid(jnp.einsum('btnj,znjk->zbtnk', xb, w_x).reshape(2, B, T, D_RNN) + b_x[:, None, None])
        log_a = (-RG_C * jax.nn.softplus(-lam.astype(jnp.float32)))[:, None, None] * r.astype(jnp.float32)
        a = jnp.exp(log_a)
        u = jnp.sqrt(-jnp.expm1(2.0 * log_a)) * (i * xr[None]).astype(jnp.float32)
        return jax.nn.gelu(gate), a, u

    gc, ac, uc = branch(h_ctx)
    gl, al, ul = branch(h_lat)
    z = jnp.zeros((h_lat.shape[0], D_RNN), jnp.float32)
    hc_f = _linear_scan(ac[0], uc[0], z)
    hc_b = jnp.flip(_linear_scan(jnp.flip(ac[1], 1), jnp.flip(uc[1], 1), z), 1)
    hl_f = _linear_scan(al[0], ul[0], hc_f[:, -1])
    hl_b = jnp.flip(_linear_scan(jnp.flip(al[1], 1), jnp.flip(ul[1], 1), hc_b[:, 0]), 1)
    y_lat = (((hl_f + hl_b) * gl) @ w_out).astype(h_lat.dtype)
    y_ctx = (((hc_f + hc_b) * gc) @ w_out).astype(h_ctx.dtype) if ctx_out else None
    return y_lat, y_ctx


def _rope_2d(x, cos_r, sin_r, cos_c, sin_c):
    extra = x.ndim - 3

    def bc(t):
        return t.reshape((1, t.shape[0]) + (1,) * extra + (t.shape[1],))

    def rot(part, cos, sin):
        x1, x2 = jnp.split(part, 2, axis=-1)
        cos, sin = bc(cos), bc(sin)
        return jnp.concatenate([x1 * cos - x2 * sin, x2 * cos + x1 * sin], axis=-1)

    xr, xc = jnp.split(x.astype(jnp.float32), 2, axis=-1)
    return jnp.concatenate([rot(xr, cos_r, sin_r), rot(xc, cos_c, sin_c)], axis=-1).astype(x.dtype)


def _swa_mixer(h_lat, h_ctx, w_qkv, sink, w_out, rope, ctx_out):
    def project(h):
        B, T, _ = h.shape
        q, k, v = jnp.split(h @ w_qkv, [SWA_HEADS * HEAD_DIM, (SWA_HEADS + SWA_KV) * HEAD_DIM], axis=-1)
        return (q.reshape(B, T, SWA_KV, SWA_G, HEAD_DIM) * (HEAD_DIM ** -0.5),
                k.reshape(B, T, SWA_KV, HEAD_DIM), v.reshape(B, T, SWA_KV, HEAD_DIM))

    qc, kc, vc = project(h_ctx)
    q, k, v = project(h_lat)
    q = _rope_2d(q, *rope)
    k = _rope_2d(k, *rope)
    B, S = h_lat.shape[:2]
    n_ctx = kc.shape[1]
    nb = S // Q_BLOCK
    sink_l = sink.astype(jnp.float32).reshape(SWA_KV, SWA_G, 1, 1)

    pad = ((0, 0), (Q_BLOCK, Q_BLOCK), (0, 0), (0, 0))
    kp = jnp.pad(k, pad).reshape(B, nb + 2, Q_BLOCK, SWA_KV, HEAD_DIM)
    vp = jnp.pad(v, pad).reshape(B, nb + 2, Q_BLOCK, SWA_KV, HEAD_DIM)

    def band(t):
        return jnp.concatenate([t[:, :-2], t[:, 1:-1], t[:, 2:]], axis=2)

    qb = q.reshape(B, nb, Q_BLOCK, SWA_KV, SWA_G, HEAD_DIM)
    offs_q = jnp.arange(Q_BLOCK)
    offs_k = jnp.arange(3 * Q_BLOCK) - Q_BLOCK

    def block(args):
        n, qn, kn, vn = args
        qpos = n * Q_BLOCK + offs_q
        kpos = n * Q_BLOCK + offs_k
        valid = (kpos[None, :] >= 0) & (kpos[None, :] < S) & (jnp.abs(kpos[None, :] - qpos[:, None]) <= WINDOW)
        s_loc = jnp.where(valid, jnp.einsum('bqkgd,bskd->bkgqs', qn, kn).astype(jnp.float32), -jnp.inf)
        s_ctx = jnp.einsum('bqkgd,bckd->bkgqc', qn, kc).astype(jnp.float32)
        s_sink = jnp.broadcast_to(sink_l, (B, SWA_KV, SWA_G, Q_BLOCK, 1))
        p = jax.nn.softmax(jnp.concatenate([s_sink, s_ctx, s_loc], axis=-1), axis=-1).astype(vn.dtype)
        return (jnp.einsum('bkgqc,bckd->bqkgd', p[..., 1:1 + n_ctx], vc)
                + jnp.einsum('bkgqs,bskd->bqkgd', p[..., 1 + n_ctx:], vn))

    o = lax.map(block, (jnp.arange(nb), jnp.moveaxis(qb, 1, 0), jnp.moveaxis(band(kp), 1, 0), jnp.moveaxis(band(vp), 1, 0)))
    o = jnp.moveaxis(o, 0, 1).reshape(B, S, SWA_HEADS * HEAD_DIM)
    y_lat = (o @ w_out).astype(h_lat.dtype)
    y_ctx = None
    if ctx_out:
        s = jnp.einsum('bqkgd,bckd->bkgqc', qc, kc).astype(jnp.float32)
        s_sink = jnp.broadcast_to(sink_l, (B, SWA_KV, SWA_G, n_ctx, 1))
        p = jax.nn.softmax(jnp.concatenate([s_sink, s], axis=-1), axis=-1).astype(vc.dtype)
        oc = jnp.einsum('bkgqc,bckd->bqkgd', p[..., 1:], vc).reshape(B, n_ctx, SWA_HEADS * HEAD_DIM)
        y_ctx = (oc @ w_out).astype(h_ctx.dtype)
    return y_lat, y_ctx


def setup_inputs(seed: int = 0) -> dict:
    key = jax.random.key(seed)
    ks = iter(jax.random.split(key, 40))
    D = D_MODEL
    na, nr, ns = _count(0), _count(1), _count(2)

    def nrm(shape, s):
        return jax.random.normal(next(ks), shape, jnp.float32) * s

    a0 = jax.random.uniform(next(ks), (nr, 2, D_RNN), jnp.float32, 0.9, 0.999)
    p = a0 ** (1.0 / RG_C)
    lam = jnp.log(p) - jnp.log1p(-p)
    return {
        'x': nrm((BATCH, SEQ, D), 1.0),
        'c': nrm((BATCH, D), 1.0),
        'ctx': nrm((BATCH, CTX_LEN, D), 1.0),
        'c_ctx': nrm((D,), 1.0),
        'ada_w': nrm((DEPTH, D, N_MOD * D), 0.5 * D ** -0.5),
        'ada_b': nrm((DEPTH, N_MOD * D), 0.01),
        'norm_g': 1.0 + nrm((DEPTH, 3, D), 0.01),
        'final_g': 1.0 + nrm((D,), 0.01),
        'ffn_w_in': nrm((DEPTH, 2, D, 2 * D_FF), D ** -0.5),
        'ffn_w_out': nrm((DEPTH, 2, D_FF, D), D_FF ** -0.5),
        'gla_w_in': nrm((na, D, GLA_IN), D ** -0.5),
        'gla_w_a2': nrm((na, 2, GLA_RANK, GLA_HK), GLA_RANK ** -0.5),
        'gla_b_a': nrm((na, 2, GLA_HK), 0.01),
        'gla_norm_g': 1.0 + nrm((na, GLA_DV), 0.01),
        'gla_w_out': nrm((na, GLA_HV, D), GLA_HV ** -0.5),
        'rg_w_in': nrm((nr, D, 2 * D_RNN), D ** -0.5),
        'rg_conv_w': nrm((nr, RG_CONV, D_RNN), RG_CONV ** -0.5),
        'rg_conv_b': nrm((nr, D_RNN), 0.01),
        'rg_w_a': nrm((nr, 2, RG_BLOCKS, RG_BS, RG_BS), RG_BS ** -0.5),
        'rg_b_a': nrm((nr, 2, D_RNN), 0.01),
        'rg_w_x': nrm((nr, 2, RG_BLOCKS, RG_BS, RG_BS), RG_BS ** -0.5),
        'rg_b_x': nrm((nr, 2, D_RNN), 0.01),
        'rg_lam': lam,
        'rg_w_out': nrm((nr, D_RNN, D), D_RNN ** -0.5),
        'swa_w_qkv': nrm((ns, D, SWA_QKV), D ** -0.5),
        'swa_sink': nrm((ns, SWA_HEADS), 1.0),
        'swa_w_out': nrm((ns, SWA_HEADS * HEAD_DIM, D), (SWA_HEADS * HEAD_DIM) ** -0.5),
    }


def reference(x, c, ctx, c_ctx, ada_w, ada_b, norm_g, final_g, ffn_w_in, ffn_w_out,
              gla_w_in, gla_w_a2, gla_b_a, gla_norm_g, gla_w_out,
              rg_w_in, rg_conv_w, rg_conv_b, rg_w_a, rg_b_a, rg_w_x, rg_b_x, rg_lam, rg_w_out,
              swa_w_qkv, swa_sink, swa_w_out):
    B, S, D = x.shape
    rows = S // GRID_W
    row_ids = jnp.repeat(jnp.arange(rows), GRID_W).astype(jnp.float32)
    col_ids = jnp.tile(jnp.arange(GRID_W), rows).astype(jnp.float32)
    freqs = ROPE_BASE ** (-jnp.arange(0, ROT_AXIS, 2, dtype=jnp.float32) / ROT_AXIS)
    ang_r = row_ids[:, None] * freqs
    ang_c = col_ids[:, None] * freqs
    rope = (jnp.cos(ang_r), jnp.sin(ang_r), jnp.cos(ang_c), jnp.sin(ang_c))

    xc = ctx
    act_lat = jax.nn.silu(c)
    act_ctx = jax.nn.silu(c_ctx)[None]
    for i in range(DEPTH):
        m, j = i % N_MIXERS, i // N_MIXERS
        last = i == DEPTH - 1
        ml = jnp.split(act_lat @ ada_w[i] + ada_b[i], N_MOD, axis=-1)
        mc = jnp.split(act_ctx @ ada_w[i] + ada_b[i], N_MOD, axis=-1)
        x = x + 0.5 * ml[2][:, None] * _swiglu(_modulate(x, norm_g[i, 0], ml[0], ml[1]), ffn_w_in[i, 0], ffn_w_out[i, 0])
        xc = xc + 0.5 * mc[2][:, None] * _swiglu(_modulate(xc, norm_g[i, 0], mc[0], mc[1]), ffn_w_in[i, 0], ffn_w_out[i, 0])
        h_l = _modulate(x, norm_g[i, 1], ml[3], ml[4])
        h_c = _modulate(xc, norm_g[i, 1], mc[3], mc[4])
        if m == 0:
            y_l, y_c = _gla_mixer(h_l, h_c, gla_w_in[j], gla_w_a2[j], gla_b_a[j], gla_norm_g[j], gla_w_out[j], not last)
        elif m == 1:
            y_l, y_c = _rglru_mixer(h_l, h_c, rg_w_in[j], rg_conv_w[j], rg_conv_b[j], rg_w_a[j], rg_b_a[j],
                                    rg_w_x[j], rg_b_x[j], rg_lam[j], rg_w_out[j], not last)
        else:
            y_l, y_c = _swa_mixer(h_l, h_c, swa_w_qkv[j], swa_sink[j], swa_w_out[j], rope, not last)
        x = x + ml[5][:, None] * y_l
        x = x + 0.5 * ml[8][:, None] * _swiglu(_modulate(x, norm_g[i, 2], ml[6], ml[7]), ffn_w_in[i, 1], ffn_w_out[i, 1])
        if not last:
            xc = xc + mc[5][:, None] * y_c
            xc = xc + 0.5 * mc[8][:, None] * _swiglu(_modulate(xc, norm_g[i, 2], mc[6], mc[7]), ffn_w_in[i, 1], ffn_w_out[i, 1])
    return _rmsnorm(x, final_g)
```

```python
import functools
import math

import jax
import jax.numpy as jnp
from jax import lax
from jax.experimental import pallas as pl
from jax.experimental.pallas import tpu as pltpu

F32 = jnp.float32
BF16 = jnp.bfloat16

N_MOD = 9
NORM_EPS = 1e-6
GLA_HEADS = 4
GLA_DK = 128
GLA_DV = 256
GLA_RANK = 16
GLA_LOGIT_NORM = 16.0
GLA_CHUNK = 64
GLA_HK = GLA_HEADS * GLA_DK
GLA_HV = GLA_HEADS * GLA_DV
GLA_LR_PAD = 128
D_RNN = 1536
RG_BLOCKS = 12
RG_BS = D_RNN // RG_BLOCKS
RG_C = 8.0
SWA_HEADS = 16
SWA_KV = 4
SWA_G = SWA_HEADS // SWA_KV
HEAD_DIM = 64
WINDOW = 128
Q_BLOCK = 128
ROT_AXIS = HEAD_DIM // 2
ROPE_BASE = 10000.0
GRID_W = 64
N_MIXERS = 3

TM = 256
SUBLANES = 8
LANES = 128
VMEM_LIMIT = 56 * 1024 * 1024
NEG = -1e30


def _params(n_axes=1):
    return pltpu.CompilerParams(dimension_semantics=("arbitrary",) * n_axes, vmem_limit_bytes=VMEM_LIMIT)


def _dot(a, b):
    return jnp.dot(a, b, preferred_element_type=F32)


def _dot_nt(a, b):
    return lax.dot_general(a, b, (((1,), (1,)), ((), ())), preferred_element_type=F32)


def _dot_tn(a, b):
    return lax.dot_general(a, b, (((0,), (0,)), ((), ())), preferred_element_type=F32)


def _rms(x, g):
    return (x * lax.rsqrt(jnp.mean(x * x, axis=-1, keepdims=True) + NORM_EPS)) * g


def _rms_mod(x, g, shift, scale):
    return _rms(x, g) * (1.0 + scale) + shift


def _sigmoid(x):
    return 1.0 / (1.0 + jnp.exp(-x))


def _silu(x):
    return x * _sigmoid(x)


def _softplus(x):
    return jnp.maximum(x, 0.0) + jnp.log1p(jnp.exp(-jnp.abs(x)))


def _gelu_tanh(x):
    return 0.5 * x * (1.0 + jnp.tanh(math.sqrt(2.0 / math.pi) * (x + 0.044715 * (x * x * x))))


def _row(ref, r):
    return ref[r:r + 1, :]


def _tile_spec(width, tm=TM):
    return pl.BlockSpec((tm, width), lambda i: (i, 0))


def _full_spec(shape):
    nd = len(shape)
    return pl.BlockSpec(shape, lambda i: (0,) * nd)


def _mod_spec(n_lat, d):
    return pl.BlockSpec((None, N_MOD, d), lambda i: (jnp.minimum(i // n_lat, 1), 0, 0))


def _ada_kernel(cc_ref, w_ref, b_ref, o_ref):
    act = _silu(cc_ref[...]).astype(BF16)
    o_ref[...] = _dot(act, w_ref[...].astype(BF16)) + b_ref[...]


def _ada_vectors(c, c_ctx, ada_w, ada_b):
    depth, d, nd = ada_w.shape
    cc = jnp.zeros((SUBLANES, d), F32).at[0].set(c[0]).at[1].set(c_ctx)
    out = pl.pallas_call(
        _ada_kernel,
        grid=(depth, nd // d),
        in_specs=[pl.BlockSpec((SUBLANES, d), lambda l, j: (0, 0)),
                  pl.BlockSpec((None, d, d), lambda l, j: (l, 0, j)),
                  pl.BlockSpec((None, 1, d), lambda l, j: (l, 0, j))],
        out_specs=pl.BlockSpec((None, SUBLANES, d), lambda l, j: (l, 0, j)),
        out_shape=jax.ShapeDtypeStruct((depth, SUBLANES, nd), F32),
        compiler_params=_params(2),
        name="ada_vectors",
    )(cc, ada_w, ada_b.reshape(depth, 1, nd))
    return out.reshape(depth, SUBLANES, N_MOD, d)[:, :2]


def _ffn_kernel(x_ref, mod_ref, g_ref, w1_ref, w2_ref, o_ref, *, r0):
    x = x_ref[...]
    h = _rms_mod(x, g_ref[...], _row(mod_ref, r0), _row(mod_ref, r0 + 1)).astype(BF16)
    gu = _dot(h, w1_ref[...])
    f = w2_ref.shape[0]
    a = (_silu(gu[:, :f]) * gu[:, f:]).astype(BF16)
    y = _dot(a, w2_ref[...])
    o_ref[...] = x + (0.5 * _row(mod_ref, r0 + 2)) * y


def _ffn(xs, mod, g, w1, w2, r0, n_lat):
    t, d = xs.shape
    return pl.pallas_call(
        functools.partial(_ffn_kernel, r0=r0),
        grid=(t // TM,),
        in_specs=[_tile_spec(d), _mod_spec(n_lat, d), _full_spec((1, d)),
                  _full_spec(w1.shape), _full_spec(w2.shape)],
        out_specs=_tile_spec(d),
        out_shape=jax.ShapeDtypeStruct((t, d), F32),
        compiler_params=_params(),
        name="ffn",
    )(xs, mod, g.reshape(1, d), w1, w2)


def _out_kernel(x_ref, mod_ref, y_ref, w_ref, o_ref):
    o_ref[...] = x_ref[...] + _row(mod_ref, 5) * _dot(y_ref[...], w_ref[...])


def _out_proj(xs, mod, y, w, n_lat):
    t, d = xs.shape
    return pl.pallas_call(
        _out_kernel,
        grid=(t // TM,),
        in_specs=[_tile_spec(d), _mod_spec(n_lat, d), _tile_spec(y.shape[1]), _full_spec(w.shape)],
        out_specs=_tile_spec(d),
        out_shape=jax.ShapeDtypeStruct((t, d), F32),
        compiler_params=_params(),
        name="out_proj",
    )(xs, mod, y, w)


def _gla_proj_kernel(x_ref, mod_ref, g_ref, w_ref, w2_ref, ba_ref, q_ref, k_ref, v_ref, og_ref, gg_ref):
    h = _rms_mod(x_ref[...], g_ref[...], _row(mod_ref, 3), _row(mod_ref, 4)).astype(BF16)
    y = _dot(h, w_ref[...])
    q_ref[...] = y[:, :GLA_HK] * (GLA_DK ** -0.5)
    k_ref[...] = y[:, GLA_HK:2 * GLA_HK]
    v_ref[...] = y[:, 2 * GLA_HK:2 * GLA_HK + GLA_HV].astype(BF16)
    og_ref[...] = y[:, 2 * GLA_HK + GLA_HV:2 * GLA_HK + 2 * GLA_HV].astype(BF16)
    lr = y[:, 2 * GLA_HK + 2 * GLA_HV:].astype(BF16)
    logit = _dot(lr, w2_ref[...]) + ba_ref[...]
    gg_ref[...] = (-_softplus(-logit)) / GLA_LOGIT_NORM


def _gla_proj(xs, mod, g, w, w2, ba, n_lat):
    t, d = xs.shape
    return pl.pallas_call(
        _gla_proj_kernel,
        grid=(t // TM,),
        in_specs=[_tile_spec(d), _mod_spec(n_lat, d), _full_spec((1, d)),
                  _full_spec(w.shape), _full_spec(w2.shape), _full_spec(ba.shape)],
        out_specs=[_tile_spec(GLA_HK), _tile_spec(GLA_HK), _tile_spec(GLA_HV), _tile_spec(GLA_HV),
                   _tile_spec(2 * GLA_HK)],
        out_shape=[jax.ShapeDtypeStruct((t, GLA_HK), F32), jax.ShapeDtypeStruct((t, GLA_HK), F32),
                   jax.ShapeDtypeStruct((t, GLA_HV), BF16), jax.ShapeDtypeStruct((t, GLA_HV), BF16),
                   jax.ShapeDtypeStruct((t, 2 * GLA_HK), F32)],
        compiler_params=_params(),
        name="gla_proj",
    )(xs, mod, g.reshape(1, d), w, w2, ba)


def _gla_kernel(q_ref, k_ref, v_ref, g_ref, o_ref, s_ref, *, rev):
    @pl.when(pl.program_id(0) == 0)
    def _():
        s_ref[...] = jnp.zeros_like(s_ref)

    c = GLA_CHUNK
    nc = q_ref.shape[0] // c
    r_i = lax.broadcasted_iota(jnp.int32, (c, c), 0)
    c_i = lax.broadcasted_iota(jnp.int32, (c, c), 1)
    tri = (c_i >= r_i) if rev else (c_i <= r_i)
    tri_bf = tri.astype(BF16)
    for ci in (reversed(range(nc)) if rev else range(nc)):
        sl = slice(ci * c, (ci + 1) * c)
        g = g_ref[sl, :]
        g_hi = g.astype(BF16)
        g_lo = (g - g_hi.astype(F32)).astype(BF16)
        b = _dot(tri_bf, g_hi) + _dot(tri_bf, g_lo)
        btot = b[0:1, :] if rev else b[c - 1:c, :]
        k = k_ref[sl, :]
        qe = (q_ref[sl, :] * jnp.exp(b)).astype(BF16)
        ke = (k * jnp.exp(-b)).astype(BF16)
        kd = (k * jnp.exp(btot - b)).astype(BF16)
        dec = jnp.exp(btot)
        v = v_ref[sl, :]
        for h in range(GLA_HEADS):
            ks = slice(h * GLA_DK, (h + 1) * GLA_DK)
            vs = slice(h * GLA_DV, (h + 1) * GLA_DV)
            att = jnp.where(tri, _dot_nt(qe[:, ks], ke[:, ks]), 0.0).astype(BF16)
            st = s_ref[h]
            o_ref[sl, vs] = _dot(att, v[:, vs]) + _dot_nt(qe[:, ks], st.astype(BF16))
            s_ref[h] = dec[:, ks] * st + _dot_tn(v[:, vs], kd[:, ks])


def _gla_dir(q, k, v, gg, n_lat, rev):
    t = q.shape[0]
    n_tot = t // TM
    if rev:
        blk = lambda s: n_tot - 1 - s
    else:
        blk = lambda s: (s + n_lat) % n_tot
    col = 1 if rev else 0
    return pl.pallas_call(
        functools.partial(_gla_kernel, rev=rev),
        grid=(n_tot,),
        in_specs=[pl.BlockSpec((TM, GLA_HK), lambda s: (blk(s), 0)),
                  pl.BlockSpec((TM, GLA_HK), lambda s: (blk(s), 0)),
                  pl.BlockSpec((TM, GLA_HV), lambda s: (blk(s), 0)),
                  pl.BlockSpec((TM, GLA_HK), lambda s: (blk(s), col))],
        out_specs=pl.BlockSpec((TM, GLA_HV), lambda s: (blk(s), 0)),
        out_shape=jax.ShapeDtypeStruct((t, GLA_HV), F32),
        scratch_shapes=[pltpu.VMEM((GLA_HEADS, GLA_DV, GLA_DK), F32)],
        compiler_params=_params(),
        name="gla_bwd" if rev else "gla_fwd",
    )(q, k, v, gg)


def _gla_out_kernel(x_ref, mod_ref, of_ref, ob_ref, og_ref, ng_ref, w_ref, o_ref):
    o = of_ref[...] + ob_ref[...]
    ng = ng_ref[...]
    parts = [_rms(o[:, h * GLA_DV:(h + 1) * GLA_DV], ng) for h in range(GLA_HEADS)]
    y = (jnp.concatenate(parts, axis=1) * _silu(og_ref[...].astype(F32))).astype(BF16)
    o_ref[...] = x_ref[...] + _row(mod_ref, 5) * _dot(y, w_ref[...])


def _gla_out(xs, mod, o_f, o_b, og, ng, w, n_lat):
    t, d = xs.shape
    return pl.pallas_call(
        _gla_out_kernel,
        grid=(t // TM,),
        in_specs=[_tile_spec(d), _mod_spec(n_lat, d), _tile_spec(GLA_HV), _tile_spec(GLA_HV),
                  _tile_spec(GLA_HV), _full_spec((1, GLA_DV)), _full_spec(w.shape)],
        out_specs=_tile_spec(d),
        out_shape=jax.ShapeDtypeStruct((t, d), F32),
        compiler_params=_params(),
        name="gla_out",
    )(xs, mod, o_f, o_b, og, ng.reshape(1, GLA_DV), w)


def _gla_layer(xs, mod, norm_g, w_in, w_a2, b_a, gla_norm_g, w_out, n_lat):
    d = xs.shape[1]
    n_main = 2 * GLA_HK + 2 * GLA_HV
    w = jnp.zeros((d, n_main + GLA_LR_PAD), BF16).at[:, :w_in.shape[1]].set(w_in.astype(BF16))
    w2 = jnp.zeros((GLA_LR_PAD, 2 * GLA_HK), F32)
    w2 = w2.at[:GLA_RANK, :GLA_HK].set(w_a2[0]).at[GLA_RANK:2 * GLA_RANK, GLA_HK:].set(w_a2[1]).astype(BF16)
    q, k, v, og, gg = _gla_proj(xs, mod, norm_g, w, w2, b_a.reshape(1, 2 * GLA_HK), n_lat)
    o_f = _gla_dir(q, k, v, gg, n_lat, rev=False)
    o_b = _gla_dir(q, k, v, gg, n_lat, rev=True)
    return _gla_out(xs, mod, o_f, o_b, og, gla_norm_g, w_out.astype(BF16), n_lat)


def _rg_proj_kernel(x_ref, mod_ref, g_ref, w_ref, gate_ref, xr_ref):
    h = _rms_mod(x_ref[...], g_ref[...], _row(mod_ref, 3), _row(mod_ref, 4)).astype(BF16)
    y = _dot(h, w_ref[...])
    gate_ref[...] = _gelu_tanh(y[:, :D_RNN]).astype(BF16)
    xr_ref[...] = y[:, D_RNN:]


def _rg_proj(xs, mod, g, w, n_lat):
    t, d = xs.shape
    return pl.pallas_call(
        _rg_proj_kernel,
        grid=(t // TM,),
        in_specs=[_tile_spec(d), _mod_spec(n_lat, d), _full_spec((1, d)), _full_spec(w.shape)],
        out_specs=[_tile_spec(D_RNN), _tile_spec(D_RNN)],
        out_shape=[jax.ShapeDtypeStruct((t, D_RNN), BF16), jax.ShapeDtypeStruct((t, D_RNN), F32)],
        compiler_params=_params(),
        name="rg_proj",
    )(xs, mod, g.reshape(1, d), w)


def _rg_scan_kernel(xm_ref, xp_ref, xn_ref, cw_ref, cb_ref, w_ref, ba_ref, bx_ref, lam_ref, h_ref,
                    xs_ref, a_ref, u_ref, p_ref, hs_ref, carry_ref, *, rev, n_lat, n_tot):
    s = pl.program_id(0)
    blk = (n_tot - 1 - s) if rev else (s + n_lat) % n_tot

    @pl.when(s == 0)
    def _():
        carry_ref[...] = jnp.zeros_like(carry_ref)

    tm = xm_ref.shape[0]
    hal = SUBLANES
    has_prev = jnp.logical_and(blk != 0, blk != n_lat)
    has_next = jnp.logical_and(blk != n_lat - 1, blk != n_tot - 1)
    xs_ref[0:hal, :] = jnp.where(has_prev, xp_ref[...], 0.0)
    xs_ref[hal:hal + tm, :] = xm_ref[...]
    xs_ref[hal + tm:, :] = jnp.where(has_next, xn_ref[...], 0.0)
    xc = cb_ref[...]
    for j in range(4):
        xc = xc + _row(cw_ref, j) * xs_ref[hal - 2 + j:hal - 2 + j + tm, :]
    cz = -RG_C * _softplus(-lam_ref[...])
    for n in range(RG_BLOCKS):
        ns = slice(n * RG_BS, (n + 1) * RG_BS)
        xb = xc[:, ns]
        z = _dot(xb.astype(BF16), w_ref[n])
        r = _sigmoid(z[:, :RG_BS] + ba_ref[:, ns])
        i = _sigmoid(z[:, RG_BS:] + bx_ref[:, ns])
        a = jnp.exp(cz[:, ns] * r)
        a_ref[n] = a
        u_ref[n] = jnp.sqrt(1.0 - a * a) * (i * xb)

    seg = tm // SUBLANES
    nt = a_ref.shape[0]

    def rows(i):
        return pl.ds(i, SUBLANES, stride=seg)

    def local(it, carry):
        i = (seg - 1 - it) if rev else it
        out = []
        for n in range(nt):
            a = a_ref[n, rows(i), :]
            hl = a * carry[2 * n] + u_ref[n, rows(i), :]
            p = a * carry[2 * n + 1]
            hs_ref[n, rows(i), :] = hl
            p_ref[n, rows(i), :] = p
            out += [hl, p]
        return tuple(out)

    init = (jnp.zeros((SUBLANES, LANES), F32), jnp.ones((SUBLANES, LANES), F32)) * nt
    ends = lax.fori_loop(0, seg, local, init)
    cmats = []
    for n in range(nt):
        ns = slice(n * LANES, (n + 1) * LANES)
        hl_end, p_end = ends[2 * n], ends[2 * n + 1]
        c = carry_ref[:, ns]
        cin = [None] * SUBLANES
        for sg in (reversed(range(SUBLANES)) if rev else range(SUBLANES)):
            cin[sg] = c
            c = hl_end[sg:sg + 1, :] + p_end[sg:sg + 1, :] * c
        carry_ref[:, ns] = c
        cmats.append(jnp.concatenate(cin, axis=0))

    def fix(i, _):
        for n in range(nt):
            hs_ref[n, rows(i), :] = hs_ref[n, rows(i), :] + p_ref[n, rows(i), :] * cmats[n]
        return 0

    lax.fori_loop(0, seg, fix, 0)
    for n in range(nt):
        h_ref[:, n * LANES:(n + 1) * LANES] = hs_ref[n]


def _rg_scan(xr, cw, cb, w, ba, bx, lam, n_lat, rev):
    t, dr = xr.shape
    n_tot = t // TM
    per = TM // SUBLANES
    n8 = t // SUBLANES
    if rev:
        blk = lambda s: n_tot - 1 - s
    else:
        blk = lambda s: (s + n_lat) % n_tot
    return pl.pallas_call(
        functools.partial(_rg_scan_kernel, rev=rev, n_lat=n_lat, n_tot=n_tot),
        grid=(n_tot,),
        in_specs=[pl.BlockSpec((TM, dr), lambda s: (blk(s), 0)),
                  pl.BlockSpec((SUBLANES, dr), lambda s: (jnp.maximum(blk(s) * per - 1, 0), 0)),
                  pl.BlockSpec((SUBLANES, dr), lambda s: (jnp.minimum((blk(s) + 1) * per, n8 - 1), 0)),
                  _full_spec(cw.shape), _full_spec(cb.shape), _full_spec(w.shape),
                  _full_spec(ba.shape), _full_spec(bx.shape), _full_spec(lam.shape)],
        out_specs=pl.BlockSpec((TM, dr), lambda s: (blk(s), 0)),
        out_shape=jax.ShapeDtypeStruct((t, dr), F32),
        scratch_shapes=[pltpu.VMEM((TM + 2 * SUBLANES, dr), F32)]
        + [pltpu.VMEM((dr // LANES, TM, LANES), F32)] * 4 + [pltpu.VMEM((1, dr), F32)],
        compiler_params=_params(),
        name="rg_scan_bwd" if rev else "rg_scan_fwd",
    )(xr, xr, xr, cw, cb, w, ba, bx, lam)


def _rg_out_kernel(x_ref, mod_ref, hf_ref, hb_ref, gate_ref, w_ref, o_ref):
    y = ((hf_ref[...] + hb_ref[...]) * gate_ref[...].astype(F32)).astype(BF16)
    o_ref[...] = x_ref[...] + _row(mod_ref, 5) * _dot(y, w_ref[...])


def _rg_out(xs, mod, h_f, h_b, gate, w, n_lat):
    t, d = xs.shape
    return pl.pallas_call(
        _rg_out_kernel,
        grid=(t // TM,),
        in_specs=[_tile_spec(d), _mod_spec(n_lat, d), _tile_spec(D_RNN), _tile_spec(D_RNN),
                  _tile_spec(D_RNN), _full_spec(w.shape)],
        out_specs=_tile_spec(d),
        out_shape=jax.ShapeDtypeStruct((t, d), F32),
        compiler_params=_params(),
        name="rg_out",
    )(xs, mod, h_f, h_b, gate, w)


def _rg_layer(xs, mod, norm_g, w_in, conv_w, conv_b, w_a, b_a, w_x, b_x, lam, w_out, n_lat):
    gate, xr = _rg_proj(xs, mod, norm_g, w_in.astype(BF16), n_lat)
    hs = []
    for z in range(2):
        w = jnp.concatenate([w_a[z], w_x[z]], axis=-1).astype(BF16)
        hs.append(_rg_scan(xr, conv_w, conv_b.reshape(1, D_RNN), w, b_a[z].reshape(1, D_RNN),
                           b_x[z].reshape(1, D_RNN), lam[z].reshape(1, D_RNN), n_lat, rev=(z == 1)))
    return _rg_out(xs, mod, hs[0], hs[1], gate, w_out.astype(BF16), n_lat)


def _swa_proj_kernel(x_ref, mod_ref, g_ref, w_ref, cos_ref, sin_ref, q_ref, k_ref, v_ref):
    h = _rms_mod(x_ref[...], g_ref[...], _row(mod_ref, 3), _row(mod_ref, 4)).astype(BF16)
    y = _dot(h, w_ref[...])
    nq = SWA_HEADS * HEAD_DIM
    nk = SWA_KV * HEAD_DIM
    cos = cos_ref[...]
    sin = sin_ref[...]
    half = ROT_AXIS // 2
    lane = lax.broadcasted_iota(jnp.int32, (1, LANES), 1)
    first = (lane % ROT_AXIS) < half

    def rope(xg):
        partner = jnp.where(first, pltpu.roll(xg, LANES - half, axis=1), pltpu.roll(xg, half, axis=1))
        return xg * cos + partner * sin

    q = y[:, :nq] * (HEAD_DIM ** -0.5)
    for gi in range(nq // LANES):
        q_ref[:, gi * LANES:(gi + 1) * LANES] = rope(q[:, gi * LANES:(gi + 1) * LANES]).astype(BF16)
    k = y[:, nq:nq + nk]
    for gi in range(nk // LANES):
        k_ref[:, gi * LANES:(gi + 1) * LANES] = rope(k[:, gi * LANES:(gi + 1) * LANES]).astype(BF16)
    v_ref[...] = y[:, nq + nk:].astype(BF16)


def _swa_proj(xs, mod, g, w, cos, sin, n_lat):
    t, d = xs.shape
    nq = SWA_HEADS * HEAD_DIM
    nk = SWA_KV * HEAD_DIM
    return pl.pallas_call(
        _swa_proj_kernel,
        grid=(t // TM,),
        in_specs=[_tile_spec(d), _mod_spec(n_lat, d), _full_spec((1, d)), _full_spec(w.shape),
                  _tile_spec(LANES), _tile_spec(LANES)],
        out_specs=[_tile_spec(nq), _tile_spec(nk), _tile_spec(nk)],
        out_shape=[jax.ShapeDtypeStruct((t, nq), BF16), jax.ShapeDtypeStruct((t, nk), BF16),
                   jax.ShapeDtypeStruct((t, nk), BF16)],
        compiler_params=_params(),
        name="swa_proj",
    )(xs, mod, g.reshape(1, d), w, cos, sin)


def _swa_kernel(sink_ref, q_ref, *refs, local, nb):
    if local:
        kp_ref, k0_ref, kn_ref, vp_ref, v0_ref, vn_ref, kc_ref, vc_ref, o_ref = refs
    else:
        kc_ref, vc_ref, o_ref = refs
    qb = q_ref.shape[0]
    if local:
        i = pl.program_id(0)
        r_i = lax.broadcasted_iota(jnp.int32, (qb, qb), 0)
        c_i = lax.broadcasted_iota(jnp.int32, (qb, qb), 1)
        m_prev = jnp.logical_and(c_i >= r_i, i > 0)
        m_next = jnp.logical_and(c_i <= r_i, i < nb - 1)
    outs = []
    for j in range(SWA_KV):
        js = slice(j * HEAD_DIM, (j + 1) * HEAD_DIM)
        kc = kc_ref[:, js]
        vc = vc_ref[:, js]
        if local:
            kp, k0, kn = kp_ref[:, js], k0_ref[:, js], kn_ref[:, js]
            vp, v0, vn = vp_ref[:, js], v0_ref[:, js], vn_ref[:, js]
        for g in range(SWA_G):
            h = j * SWA_G + g
            qh = q_ref[:, h * HEAD_DIM:(h + 1) * HEAD_DIM]
            sink = sink_ref[h]
            s_c = _dot_nt(qh, kc)
            m = jnp.maximum(jnp.max(s_c, axis=-1, keepdims=True), sink)
            if local:
                s_p = jnp.where(m_prev, _dot_nt(qh, kp), NEG)
                s_0 = _dot_nt(qh, k0)
                s_n = jnp.where(m_next, _dot_nt(qh, kn), NEG)
                m = jnp.maximum(m, jnp.max(s_p, axis=-1, keepdims=True))
                m = jnp.maximum(m, jnp.max(s_0, axis=-1, keepdims=True))
                m = jnp.maximum(m, jnp.max(s_n, axis=-1, keepdims=True))
            p_c = jnp.exp(s_c - m)
            den = jnp.exp(sink - m) + jnp.sum(p_c, axis=-1, keepdims=True)
            acc = _dot(p_c.astype(BF16), vc)
            if local:
                for s_x, v_x in ((s_p, vp), (s_0, v0), (s_n, vn)):
                    p_x = jnp.exp(s_x - m)
                    den = den + jnp.sum(p_x, axis=-1, keepdims=True)
                    acc = acc + _dot(p_x.astype(BF16), v_x)
            outs.append(acc / den)
    o_ref[...] = jnp.concatenate(outs, axis=1).astype(BF16)


def _swa_attn(q, k, v, sink, s_len):
    t, nq = q.shape
    nk = k.shape[1]
    c_len = t - s_len
    nb = s_len // Q_BLOCK
    smem = pl.BlockSpec(memory_space=pltpu.SMEM)
    ctx_spec = pl.BlockSpec((c_len, nk), lambda i: (s_len // c_len, 0))
    band = [pl.BlockSpec((Q_BLOCK, nk), lambda i: (jnp.maximum(i - 1, 0), 0)),
            pl.BlockSpec((Q_BLOCK, nk), lambda i: (i, 0)),
            pl.BlockSpec((Q_BLOCK, nk), lambda i: (jnp.minimum(i + 1, nb - 1), 0))]
    o_lat = pl.pallas_call(
        functools.partial(_swa_kernel, local=True, nb=nb),
        grid=(nb,),
        in_specs=[smem, pl.BlockSpec((Q_BLOCK, nq), lambda i: (i, 0))] + band + band + [ctx_spec, ctx_spec],
        out_specs=pl.BlockSpec((Q_BLOCK, nq), lambda i: (i, 0)),
        out_shape=jax.ShapeDtypeStruct((s_len, nq), BF16),
        compiler_params=_params(),
        name="swa_attn",
    )(sink, q, k, k, k, v, v, v, k, v)
    ncb = c_len // Q_BLOCK
    o_ctx = pl.pallas_call(
        functools.partial(_swa_kernel, local=False, nb=ncb),
        grid=(ncb,),
        in_specs=[smem, pl.BlockSpec((Q_BLOCK, nq), lambda i: (nb + i, 0)), ctx_spec, ctx_spec],
        out_specs=pl.BlockSpec((Q_BLOCK, nq), lambda i: (i, 0)),
        out_shape=jax.ShapeDtypeStruct((c_len, nq), BF16),
        compiler_params=_params(),
        name="swa_attn_ctx",
    )(sink, q, k, v)
    return jnp.concatenate([o_lat, o_ctx], axis=0)


def _rope_tables(s_len, c_len):
    pos = jnp.arange(s_len)
    ids = jnp.stack([(pos // GRID_W).astype(F32), (pos % GRID_W).astype(F32)], axis=1)
    freqs = ROPE_BASE ** (-jnp.arange(0, ROT_AXIS, 2, dtype=F32) / ROT_AXIS)
    lane = jnp.arange(LANES)
    dim = lane % HEAD_DIM
    ang = ids[:, dim // ROT_AXIS] * freqs[dim % (ROT_AXIS // 2)][None, :]
    sign = jnp.where((dim % ROT_AXIS) < ROT_AXIS // 2, -1.0, 1.0)
    cos = jnp.concatenate([jnp.cos(ang), jnp.ones((c_len, LANES), F32)], axis=0)
    sin = jnp.concatenate([jnp.sin(ang) * sign, jnp.zeros((c_len, LANES), F32)], axis=0)
    return cos, sin


def _swa_layer(xs, mod, norm_g, w_qkv, sink, w_out, rope, n_lat, s_len):
    q, k, v = _swa_proj(xs, mod, norm_g, w_qkv.astype(BF16), rope[0], rope[1], n_lat)
    o = _swa_attn(q, k, v, sink, s_len)
    return _out_proj(xs, mod, o, w_out.astype(BF16), n_lat)


def _final_kernel(x_ref, g_ref, o_ref):
    o_ref[...] = _rms(x_ref[...], g_ref[...])


def _final_norm(xs, g, s_len):
    d = xs.shape[1]
    return pl.pallas_call(
        _final_kernel,
        grid=(s_len // TM,),
        in_specs=[_tile_spec(d), _full_spec((1, d))],
        out_specs=_tile_spec(d),
        out_shape=jax.ShapeDtypeStruct((s_len, d), F32),
        compiler_params=_params(),
        name="final_norm",
    )(xs, g.reshape(1, d))


def kernel(x, c, ctx, c_ctx, ada_w, ada_b, norm_g, final_g, ffn_w_in, ffn_w_out, gla_w_in, gla_w_a2, gla_b_a, gla_norm_g, gla_w_out, rg_w_in, rg_conv_w, rg_conv_b, rg_w_a, rg_b_a, rg_w_x, rg_b_x, rg_lam, rg_w_out, swa_w_qkv, swa_sink, swa_w_out):
    b, s_len, d = x.shape
    c_len = ctx.shape[1]
    assert b == 1 and s_len % TM == 0 and c_len % TM == 0 and s_len % c_len == 0
    depth = ada_w.shape[0]
    n_lat = s_len // TM
    xs = jnp.concatenate([x[0], ctx[0]], axis=0)
    mods = _ada_vectors(c, c_ctx, ada_w, ada_b)
    rope = _rope_tables(s_len, c_len)
    for i in range(depth):
        m, j = i % N_MIXERS, i // N_MIXERS
        mod = mods[i]
        xs = _ffn(xs, mod, norm_g[i, 0], ffn_w_in[i, 0].astype(BF16), ffn_w_out[i, 0].astype(BF16), 0, n_lat)
        if m == 0:
            xs = _gla_layer(xs, mod, norm_g[i, 1], gla_w_in[j], gla_w_a2[j], gla_b_a[j], gla_norm_g[j],
                            gla_w_out[j], n_lat)
        elif m == 1:
            xs = _rg_layer(xs, mod, norm_g[i, 1], rg_w_in[j], rg_conv_w[j], rg_conv_b[j], rg_w_a[j], rg_b_a[j],
                           rg_w_x[j], rg_b_x[j], rg_lam[j], rg_w_out[j], n_lat)
        else:
            xs = _swa_layer(xs, mod, norm_g[i, 1], swa_w_qkv[j], swa_sink[j], swa_w_out[j], rope, n_lat, s_len)
        xs = _ffn(xs, mod, norm_g[i, 2], ffn_w_in[i, 1].astype(BF16), ffn_w_out[i, 1].astype(BF16), 6, n_lat)
    return _final_norm(xs, final_g, s_len)[None]
```

```python
import functools
import math

import jax
import jax.numpy as jnp
from jax import lax
from jax.experimental import pallas as pl
from jax.experimental.pallas import tpu as pltpu

F32 = jnp.float32
BF16 = jnp.bfloat16

N_MOD = 9
NORM_EPS = 1e-6
GLA_HEADS = 4
GLA_DK = 128
GLA_DV = 256
GLA_RANK = 16
GLA_LOGIT_NORM = 16.0
GLA_CHUNK = 64
GLA_HK = GLA_HEADS * GLA_DK
GLA_HV = GLA_HEADS * GLA_DV
GLA_LR_PAD = 128
D_RNN = 1536
RG_BLOCKS = 12
RG_BS = D_RNN // RG_BLOCKS
RG_C = 8.0
SWA_HEADS = 16
SWA_KV = 4
SWA_G = SWA_HEADS // SWA_KV
HEAD_DIM = 64
WINDOW = 128
Q_BLOCK = 128
ROT_AXIS = HEAD_DIM // 2
ROPE_BASE = 10000.0
GRID_W = 64
N_MIXERS = 3

TM = 256
SUBLANES = 8
LANES = 128
VMEM_LIMIT = 56 * 1024 * 1024
NEG = -1e30


def _odd_pitch(rows):
    tiles = -(-rows // SUBLANES)
    return (tiles + (tiles % 2 == 0)) * SUBLANES


RG_SEG = TM // SUBLANES
RG_HPITCH = _odd_pitch(RG_SEG)
RG_XPITCH = _odd_pitch(RG_SEG + 2 * SUBLANES)


def _params(n_axes=1):
    return pltpu.CompilerParams(dimension_semantics=("arbitrary",) * n_axes, vmem_limit_bytes=VMEM_LIMIT)


def _dot(a, b):
    return jnp.dot(a, b, preferred_element_type=F32)


def _dot_nt(a, b):
    return lax.dot_general(a, b, (((1,), (1,)), ((), ())), preferred_element_type=F32)


def _dot_tn(a, b):
    return lax.dot_general(a, b, (((0,), (0,)), ((), ())), preferred_element_type=F32)


def _rms(x, g):
    return (x * lax.rsqrt(jnp.mean(x * x, axis=-1, keepdims=True) + NORM_EPS)) * g


def _rms_mod(x, g, shift, scale):
    return _rms(x, g) * (1.0 + scale) + shift


def _sigmoid(x):
    return 1.0 / (1.0 + jnp.exp(-x))


def _silu(x):
    return x * _sigmoid(x)


def _softplus(x):
    return jnp.maximum(x, 0.0) + jnp.log1p(jnp.exp(-jnp.abs(x)))


def _gelu_tanh(x):
    return 0.5 * x * (1.0 + jnp.tanh(math.sqrt(2.0 / math.pi) * (x + 0.044715 * (x * x * x))))


def _row(ref, r):
    return ref[r:r + 1, :]


def _tile_spec(width, tm=TM):
    return pl.BlockSpec((tm, width), lambda i: (i, 0))


def _full_spec(shape):
    nd = len(shape)
    return pl.BlockSpec(shape, lambda i: (0,) * nd)


def _pick_spec(shape, lead):
    tail = shape[len(lead):]
    return pl.BlockSpec((None,) * len(lead) + tuple(tail), lambda i: tuple(lead) + (0,) * len(tail))


def _mod_spec(mods, layer, n_lat):
    _, _, n_mod, d = mods.shape
    return pl.BlockSpec((None, None, n_mod, d), lambda i: (layer, jnp.minimum(i // n_lat, 1), 0, 0))


def _ada_kernel(cc_ref, w_ref, b_ref, o_ref):
    act = _silu(cc_ref[...]).astype(BF16)
    o_ref[...] = _dot(act, w_ref[...].astype(BF16)) + b_ref[...]


def _ada_vectors(c, c_ctx, ada_w, ada_b):
    depth, d, nd = ada_w.shape
    cc = jnp.zeros((SUBLANES, d), F32).at[0].set(c[0]).at[1].set(c_ctx)
    out = pl.pallas_call(
        _ada_kernel,
        grid=(depth, nd // d),
        in_specs=[pl.BlockSpec((SUBLANES, d), lambda l, j: (0, 0)),
                  pl.BlockSpec((None, d, d), lambda l, j: (l, 0, j)),
                  pl.BlockSpec((None, 1, d), lambda l, j: (l, 0, j))],
        out_specs=pl.BlockSpec((None, SUBLANES, d), lambda l, j: (l, 0, j)),
        out_shape=jax.ShapeDtypeStruct((depth, SUBLANES, nd), F32),
        compiler_params=_params(2),
        name="ada_vectors",
    )(cc, ada_w, ada_b.reshape(depth, 1, nd))
    return out.reshape(depth, SUBLANES, N_MOD, d)


def _ffn_kernel(*refs, r0, n_lat, split_in, final):
    if split_in:
        xl_ref, xc_ref, *refs = refs
        x = jnp.where(pl.program_id(0) < n_lat, xl_ref[...], xc_ref[...])
    else:
        x_ref, *refs = refs
        x = x_ref[...]
    if final:
        mod_ref, g_ref, w1_ref, w2_ref, fg_ref, o_ref = refs
    else:
        mod_ref, g_ref, w1_ref, w2_ref, o_ref = refs
    h = _rms_mod(x, g_ref[...], _row(mod_ref, r0), _row(mod_ref, r0 + 1)).astype(BF16)
    gu = _dot(h, w1_ref[...])
    f = w2_ref.shape[0]
    a = (_silu(gu[:, :f]) * gu[:, f:]).astype(BF16)
    y = x + (0.5 * _row(mod_ref, r0 + 2)) * _dot(a, w2_ref[...])
    o_ref[...] = _rms(y, fg_ref[...]) if final else y


def _ffn(xs, mods, norm_g, w1, w2, layer, which, n_lat, n_tiles, final_g=None):
    split_in = isinstance(xs, tuple)
    d = (xs[0] if split_in else xs).shape[1]
    if split_in:
        x_specs = [pl.BlockSpec((TM, d), lambda i: (jnp.minimum(i, n_lat - 1), 0)),
                   pl.BlockSpec((TM, d), lambda i: (jnp.maximum(i - n_lat, 0), 0))]
        x_args = list(xs)
    else:
        x_specs, x_args = [_tile_spec(d)], [xs]
    fin_specs, fin_args = ([_full_spec((1, d))], [final_g.reshape(1, d)]) if final_g is not None else ([], [])
    return pl.pallas_call(
        functools.partial(_ffn_kernel, r0=3 * which, n_lat=n_lat, split_in=split_in, final=final_g is not None),
        grid=(n_tiles,),
        in_specs=x_specs + [_mod_spec(mods, layer, n_lat), _pick_spec(norm_g.shape, (3 * layer + which,)),
                            _pick_spec(w1.shape, (layer, which // 2)), _pick_spec(w2.shape, (layer, which // 2))]
        + fin_specs,
        out_specs=_tile_spec(d),
        out_shape=jax.ShapeDtypeStruct((n_tiles * TM, d), F32),
        compiler_params=_params(),
        name="ffn",
    )(*x_args, mods, norm_g, w1, w2, *fin_args)


def _out_kernel(x_ref, mod_ref, y_ref, w_ref, o_ref):
    o_ref[...] = x_ref[...] + _row(mod_ref, 5) * _dot(y_ref[...], w_ref[...])


def _out_proj(xs, mods, layer, y, w, n_lat):
    t, d = xs.shape
    return pl.pallas_call(
        _out_kernel,
        grid=(t // TM,),
        in_specs=[_tile_spec(d), _mod_spec(mods, layer, n_lat), _tile_spec(y.shape[1]), _full_spec(w.shape)],
        out_specs=_tile_spec(d),
        out_shape=jax.ShapeDtypeStruct((t, d), F32),
        compiler_params=_params(),
        name="out_proj",
    )(xs, mods, y, w)


def _gla_proj_kernel(x_ref, mod_ref, g_ref, w_ref, w2_ref, ba_ref, q_ref, k_ref, v_ref, og_ref, gg_ref):
    h = _rms_mod(x_ref[...], g_ref[...], _row(mod_ref, 3), _row(mod_ref, 4)).astype(BF16)
    y = _dot(h, w_ref[...])
    q_ref[...] = (y[:, :GLA_HK] * (GLA_DK ** -0.5)).astype(BF16)
    k_ref[...] = y[:, GLA_HK:2 * GLA_HK].astype(BF16)
    v_ref[...] = y[:, 2 * GLA_HK:2 * GLA_HK + GLA_HV].astype(BF16)
    og_ref[...] = y[:, 2 * GLA_HK + GLA_HV:2 * GLA_HK + 2 * GLA_HV].astype(BF16)
    lr = y[:, 2 * GLA_HK + 2 * GLA_HV:].astype(BF16)
    logit = _dot(lr, w2_ref[...]) + ba_ref[...]
    gg_ref[...] = (-_softplus(-logit)) / GLA_LOGIT_NORM


def _gla_proj(xs, mods, norm_g, layer, w, w2, ba, j, n_lat):
    t, d = xs.shape
    return pl.pallas_call(
        _gla_proj_kernel,
        grid=(t // TM,),
        in_specs=[_tile_spec(d), _mod_spec(mods, layer, n_lat), _pick_spec(norm_g.shape, (3 * layer + 1,)),
                  _pick_spec(w.shape, (j,)), _pick_spec(w2.shape, (j,)), _pick_spec(ba.shape, (j,))],
        out_specs=[_tile_spec(GLA_HK), _tile_spec(GLA_HK), _tile_spec(GLA_HV), _tile_spec(GLA_HV),
                   _tile_spec(2 * GLA_HK)],
        out_shape=[jax.ShapeDtypeStruct((t, GLA_HK), BF16), jax.ShapeDtypeStruct((t, GLA_HK), BF16),
                   jax.ShapeDtypeStruct((t, GLA_HV), BF16), jax.ShapeDtypeStruct((t, GLA_HV), BF16),
                   jax.ShapeDtypeStruct((t, 2 * GLA_HK), F32)],
        compiler_params=_params(),
        name="gla_proj",
    )(xs, mods, norm_g, w, w2, ba)


def _gla_chunk(q_ref, k_ref, v_ref, g_ref, o_ref, s_ref, ci, rev):
    c = GLA_CHUNK
    r_i = lax.broadcasted_iota(jnp.int32, (c, c), 0)
    c_i = lax.broadcasted_iota(jnp.int32, (c, c), 1)
    tri = (c_i >= r_i) if rev else (c_i <= r_i)
    tri_bf = tri.astype(BF16)
    sl = slice(ci * c, (ci + 1) * c)
    g = g_ref[sl, :]
    g_hi = g.astype(BF16)
    g_lo = (g - g_hi.astype(F32)).astype(BF16)
    b = _dot(tri_bf, g_hi) + _dot(tri_bf, g_lo)
    btot = b[0:1, :] if rev else b[c - 1:c, :]
    k = k_ref[sl, :].astype(F32)
    qe = (q_ref[sl, :].astype(F32) * jnp.exp(b)).astype(BF16)
    ke = (k * jnp.exp(-b)).astype(BF16)
    kd = (k * jnp.exp(btot - b)).astype(BF16)
    dec = jnp.exp(btot)
    v = v_ref[sl, :]
    for h in range(GLA_HEADS):
        ks = slice(h * GLA_DK, (h + 1) * GLA_DK)
        vs = slice(h * GLA_DV, (h + 1) * GLA_DV)
        att = jnp.where(tri, _dot_nt(qe[:, ks], ke[:, ks]), 0.0).astype(BF16)
        st = s_ref[h]
        o_ref[sl, vs] = (_dot(att, v[:, vs]) + _dot_nt(qe[:, ks], st.astype(BF16))).astype(o_ref.dtype)
        s_ref[h] = dec[:, ks] * st + _dot_tn(v[:, vs], kd[:, ks])


def _gla_kernel(qf_ref, kf_ref, vf_ref, gf_ref, qb_ref, kb_ref, vb_ref, gb_ref, of_ref, ob_ref, sf_ref, sb_ref):
    @pl.when(pl.program_id(0) == 0)
    def _():
        sf_ref[...] = jnp.zeros_like(sf_ref)
        sb_ref[...] = jnp.zeros_like(sb_ref)

    nc = qf_ref.shape[0] // GLA_CHUNK
    for n in range(nc):
        _gla_chunk(qf_ref, kf_ref, vf_ref, gf_ref, of_ref, sf_ref, n, False)
        _gla_chunk(qb_ref, kb_ref, vb_ref, gb_ref, ob_ref, sb_ref, nc - 1 - n, True)


def _gla_scan(q, k, v, gg, n_lat):
    t = q.shape[0]
    n_tot = t // TM
    fwd = lambda s: (s + n_lat) % n_tot
    bwd = lambda s: n_tot - 1 - s

    def specs(blk, col):
        return [pl.BlockSpec((TM, GLA_HK), lambda s: (blk(s), 0)), pl.BlockSpec((TM, GLA_HK), lambda s: (blk(s), 0)),
                pl.BlockSpec((TM, GLA_HV), lambda s: (blk(s), 0)), pl.BlockSpec((TM, GLA_HK), lambda s: (blk(s), col))]

    return pl.pallas_call(
        _gla_kernel,
        grid=(n_tot,),
        in_specs=specs(fwd, 0) + specs(bwd, 1),
        out_specs=[pl.BlockSpec((TM, GLA_HV), lambda s: (fwd(s), 0)), pl.BlockSpec((TM, GLA_HV), lambda s: (bwd(s), 0))],
        out_shape=[jax.ShapeDtypeStruct((t, GLA_HV), BF16)] * 2,
        scratch_shapes=[pltpu.VMEM((GLA_HEADS, GLA_DV, GLA_DK), F32)] * 2,
        compiler_params=_params(),
        name="gla_scan",
    )(q, k, v, gg, q, k, v, gg)


def _gla_out_kernel(x_ref, mod_ref, of_ref, ob_ref, og_ref, ng_ref, w_ref, o_ref):
    o = of_ref[...].astype(F32) + ob_ref[...].astype(F32)
    ng = ng_ref[...]
    parts = [_rms(o[:, h * GLA_DV:(h + 1) * GLA_DV], ng) for h in range(GLA_HEADS)]
    y = (jnp.concatenate(parts, axis=1) * _silu(og_ref[...].astype(F32))).astype(BF16)
    o_ref[...] = x_ref[...] + _row(mod_ref, 5) * _dot(y, w_ref[...])


def _gla_out(xs, mods, layer, o_f, o_b, og, ng, w, j, n_lat, n_tiles):
    d = xs.shape[1]
    return pl.pallas_call(
        _gla_out_kernel,
        grid=(n_tiles,),
        in_specs=[_tile_spec(d), _mod_spec(mods, layer, n_lat), _tile_spec(GLA_HV), _tile_spec(GLA_HV),
                  _tile_spec(GLA_HV), _pick_spec(ng.shape, (j,)), _pick_spec(w.shape, (j,))],
        out_specs=_tile_spec(d),
        out_shape=jax.ShapeDtypeStruct((n_tiles * TM, d), F32),
        compiler_params=_params(),
        name="gla_out",
    )(xs, mods, o_f, o_b, og, ng, w)


def _gla_weights(w_in, w_a2, b_a, gla_norm_g, w_out):
    na, d, n_in = w_in.shape
    n_main = 2 * GLA_HK + 2 * GLA_HV
    w = jnp.pad(w_in.astype(BF16), ((0, 0), (0, 0), (0, n_main + GLA_LR_PAD - n_in)))
    w2 = jnp.zeros((na, GLA_LR_PAD, 2 * GLA_HK), F32)
    w2 = w2.at[:, :GLA_RANK, :GLA_HK].set(w_a2[:, 0]).at[:, GLA_RANK:2 * GLA_RANK, GLA_HK:].set(w_a2[:, 1])
    return (w, w2.astype(BF16), b_a.reshape(na, 1, 2 * GLA_HK), gla_norm_g.reshape(na, 1, GLA_DV), w_out.astype(BF16))


def _gla_layer(xs, mods, norm_g, layer, gw, j, n_lat, n_out_tiles):
    w, w2, ba, ng, w_out = gw
    q, k, v, og, gg = _gla_proj(xs, mods, norm_g, layer, w, w2, ba, j, n_lat)
    o_f, o_b = _gla_scan(q, k, v, gg, n_lat)
    return _gla_out(xs, mods, layer, o_f, o_b, og, ng, w_out, j, n_lat, n_out_tiles)


def _rg_proj_kernel(x_ref, mod_ref, g_ref, w_ref, gate_ref, xr_ref):
    h = _rms_mod(x_ref[...], g_ref[...], _row(mod_ref, 3), _row(mod_ref, 4)).astype(BF16)
    y = _dot(h, w_ref[...])
    gate_ref[...] = _gelu_tanh(y[:, :D_RNN]).astype(BF16)
    xr_ref[...] = y[:, D_RNN:]


def _rg_proj(xs, mods, norm_g, layer, w, n_lat):
    t, d = xs.shape
    return pl.pallas_call(
        _rg_proj_kernel,
        grid=(t // TM,),
        in_specs=[_tile_spec(d), _mod_spec(mods, layer, n_lat), _pick_spec(norm_g.shape, (3 * layer + 1,)),
                  _full_spec(w.shape)],
        out_specs=[_tile_spec(D_RNN), _tile_spec(D_RNN)],
        out_shape=[jax.ShapeDtypeStruct((t, D_RNN), BF16), jax.ShapeDtypeStruct((t, D_RNN), F32)],
        compiler_params=_params(),
        name="rg_proj",
    )(xs, mods, norm_g, w)


def _rg_scan_kernel(xm_ref, xp_ref, xn_ref, cw_ref, cb_ref, w_ref, ba_ref, bx_ref, lam_ref, h_ref,
                    xs_ref, a_ref, u_ref, xp_pad_ref, hp_pad_ref, carry_ref, *, rev, n_lat, n_tot):
    s = pl.program_id(0)
    blk = (n_tot - 1 - s) if rev else (s + n_lat) % n_tot

    @pl.when(s == 0)
    def _():
        carry_ref[...] = jnp.zeros_like(carry_ref)

    tm = xm_ref.shape[0]
    hal = SUBLANES
    seg = tm // SUBLANES
    nt = a_ref.shape[0]
    assert seg == RG_SEG and RG_BS == LANES
    has_prev = jnp.logical_and(blk != 0, blk != n_lat)
    has_next = jnp.logical_and(blk != n_lat - 1, blk != n_tot - 1)
    xs_ref[0:hal, :] = jnp.where(has_prev, xp_ref[...], 0.0)
    xs_ref[hal:hal + tm, :] = xm_ref[...]
    xs_ref[hal + tm:, :] = jnp.where(has_next, xn_ref[...], 0.0)
    cz = -RG_C * _softplus(-lam_ref[...])
    ext = seg + 2 * hal
    for n in range(nt):
        ns = slice(n * LANES, (n + 1) * LANES)
        for sg in range(SUBLANES):
            xp_pad_ref[n, sg * RG_XPITCH:sg * RG_XPITCH + ext, :] = xs_ref[sg * seg:sg * seg + ext, ns]
        xe = [xp_pad_ref[n, pl.ds(k, SUBLANES, stride=RG_XPITCH), :] for k in range(hal - 2, hal + seg + 1)]
        cw = [jnp.broadcast_to(cw_ref[j:j + 1, ns], (SUBLANES, LANES)) for j in range(4)]
        cb = jnp.broadcast_to(cb_ref[:, ns], (SUBLANES, LANES))
        xc = jnp.concatenate(
            [cb + cw[0] * xe[i] + cw[1] * xe[i + 1] + cw[2] * xe[i + 2] + cw[3] * xe[i + 3] for i in range(seg)],
            axis=0)
        z = _dot(xc.astype(BF16), w_ref[n])
        r = 0.5 + 0.5 * jnp.tanh(0.5 * (z[:, :RG_BS] + ba_ref[:, ns]))
        i_g = 0.5 + 0.5 * jnp.tanh(0.5 * (z[:, RG_BS:] + bx_ref[:, ns]))
        a = jnp.exp(cz[:, ns] * r)
        a_ref[n] = a
        y = 1.0 - a * a
        u_ref[n] = jnp.where(y > 0.0, y * lax.rsqrt(y), 0.0) * (i_g * xc)

    def vreg(i):
        return pl.ds(pl.multiple_of(i * SUBLANES, SUBLANES), SUBLANES)

    def local(it, carry):
        i = (seg - 1 - it) if rev else it
        out = []
        for n in range(nt):
            a = a_ref[n, vreg(i), :]
            out += [a * carry[2 * n] + u_ref[n, vreg(i), :], a * carry[2 * n + 1]]
        return tuple(out)

    init = (jnp.zeros((SUBLANES, LANES), F32), jnp.ones((SUBLANES, LANES), F32)) * nt
    ends = lax.fori_loop(0, seg, local, init, unroll=4)
    cmats = []
    for n in range(nt):
        ns = slice(n * LANES, (n + 1) * LANES)
        hl_end, p_end = ends[2 * n], ends[2 * n + 1]
        c = carry_ref[:, ns]
        cin = [None] * SUBLANES
        for sg in (reversed(range(SUBLANES)) if rev else range(SUBLANES)):
            cin[sg] = c
            c = hl_end[sg:sg + 1, :] + p_end[sg:sg + 1, :] * c
        carry_ref[:, ns] = c
        cmats.append(jnp.concatenate(cin, axis=0))

    def final(it, carry):
        i = (seg - 1 - it) if rev else it
        out = []
        for n in range(nt):
            h = a_ref[n, vreg(i), :] * carry[n] + u_ref[n, vreg(i), :]
            hp_pad_ref[n, pl.ds(i, SUBLANES, stride=RG_HPITCH), :] = h
            out.append(h)
        return tuple(out)

    lax.fori_loop(0, seg, final, tuple(cmats), unroll=4)
    for n in range(nt):
        for sg in range(SUBLANES):
            h_ref[sg * seg:(sg + 1) * seg, n * LANES:(n + 1) * LANES] = (
                hp_pad_ref[n, sg * RG_HPITCH:sg * RG_HPITCH + seg, :].astype(h_ref.dtype))


def _rg_scan(xr, cw, cb, w, ba, bx, lam, z, n_lat):
    t, dr = xr.shape
    n_tot = t // TM
    per = TM // SUBLANES
    n8 = t // SUBLANES
    rev = z == 1
    if rev:
        blk = lambda s: n_tot - 1 - s
    else:
        blk = lambda s: (s + n_lat) % n_tot
    return pl.pallas_call(
        functools.partial(_rg_scan_kernel, rev=rev, n_lat=n_lat, n_tot=n_tot),
        grid=(n_tot,),
        in_specs=[pl.BlockSpec((TM, dr), lambda s: (blk(s), 0)),
                  pl.BlockSpec((SUBLANES, dr), lambda s: (jnp.maximum(blk(s) * per - 1, 0), 0)),
                  pl.BlockSpec((SUBLANES, dr), lambda s: (jnp.minimum((blk(s) + 1) * per, n8 - 1), 0)),
                  _full_spec(cw.shape), _full_spec(cb.shape), _pick_spec(w.shape, (z,)),
                  _pick_spec(ba.shape, (z,)), _pick_spec(bx.shape, (z,)), _pick_spec(lam.shape, (z,))],
        out_specs=pl.BlockSpec((TM, dr), lambda s: (blk(s), 0)),
        out_shape=jax.ShapeDtypeStruct((t, dr), BF16),
        scratch_shapes=[pltpu.VMEM((TM + 2 * SUBLANES, dr), F32),
                        pltpu.VMEM((dr // LANES, TM, LANES), F32), pltpu.VMEM((dr // LANES, TM, LANES), F32),
                        pltpu.VMEM((dr // LANES, SUBLANES * RG_XPITCH, LANES), F32),
                        pltpu.VMEM((dr // LANES, SUBLANES * RG_HPITCH, LANES), F32),
                        pltpu.VMEM((1, dr), F32)],
        compiler_params=_params(),
        name="rg_scan_bwd" if rev else "rg_scan_fwd",
    )(xr, xr, xr, cw, cb, w, ba, bx, lam)


def _rg_out_kernel(x_ref, mod_ref, hf_ref, hb_ref, gate_ref, w_ref, o_ref):
    y = ((hf_ref[...].astype(F32) + hb_ref[...].astype(F32)) * gate_ref[...].astype(F32)).astype(BF16)
    o_ref[...] = x_ref[...] + _row(mod_ref, 5) * _dot(y, w_ref[...])


def _rg_out(xs, mods, layer, h_f, h_b, gate, w, n_lat):
    t, d = xs.shape
    return pl.pallas_call(
        _rg_out_kernel,
        grid=(t // TM,),
        in_specs=[_tile_spec(d), _mod_spec(mods, layer, n_lat), _tile_spec(D_RNN), _tile_spec(D_RNN),
                  _tile_spec(D_RNN), _full_spec(w.shape)],
        out_specs=_tile_spec(d),
        out_shape=jax.ShapeDtypeStruct((t, d), F32),
        compiler_params=_params(),
        name="rg_out",
    )(xs, mods, h_f, h_b, gate, w)


def _rg_layer(xs, mods, norm_g, layer, w_in, conv_w, conv_b, w_a, b_a, w_x, b_x, lam, w_out, n_lat):
    gate, xr = _rg_proj(xs, mods, norm_g, layer, w_in.astype(BF16), n_lat)
    w = jnp.concatenate([w_a, w_x], axis=-1).astype(BF16)
    hs = [_rg_scan(xr, conv_w, conv_b.reshape(1, D_RNN), w, b_a.reshape(2, 1, D_RNN), b_x.reshape(2, 1, D_RNN),
                   lam.reshape(2, 1, D_RNN), z, n_lat) for z in range(2)]
    return _rg_out(xs, mods, layer, hs[0], hs[1], gate, w_out.astype(BF16), n_lat)


def _swa_proj_kernel(x_ref, mod_ref, g_ref, w_ref, cos_ref, sin_ref, q_ref, kz_ref, vz_ref):
    h = _rms_mod(x_ref[...], g_ref[...], _row(mod_ref, 3), _row(mod_ref, 4)).astype(BF16)
    y = _dot(h, w_ref[...])
    nq = SWA_HEADS * HEAD_DIM
    nk = SWA_KV * HEAD_DIM
    cos = cos_ref[...]
    sin = sin_ref[...]
    rot = ROT_AXIS // 2
    lane = lax.broadcasted_iota(jnp.int32, (1, LANES), 1)
    first = (lane % ROT_AXIS) < rot

    def rope(xg):
        partner = jnp.where(first, pltpu.roll(xg, LANES - rot, axis=1), pltpu.roll(xg, rot, axis=1))
        return xg * cos + partner * sin

    q = y[:, :nq] * (HEAD_DIM ** -0.5)
    for gi in range(nq // LANES):
        q_ref[:, gi * LANES:(gi + 1) * LANES] = rope(q[:, gi * LANES:(gi + 1) * LANES]).astype(BF16)
    low = lane < HEAD_DIM
    for gi in range(nk // LANES):
        kg = rope(y[:, nq + gi * LANES:nq + (gi + 1) * LANES])
        vg = y[:, nq + nk + gi * LANES:nq + nk + (gi + 1) * LANES]
        for par_j in range(2):
            j = 2 * gi + par_j
            for par_q in range(2):
                src_k = kg if par_q == par_j else pltpu.roll(kg, HEAD_DIM, axis=1)
                src_v = vg if par_q == par_j else pltpu.roll(vg, HEAD_DIM, axis=1)
                keep = low if par_q == 0 else jnp.logical_not(low)
                cs = slice((2 * j + par_q) * LANES, (2 * j + par_q + 1) * LANES)
                kz_ref[:, cs] = jnp.where(keep, src_k, 0.0).astype(BF16)
                vz_ref[:, cs] = jnp.where(keep, src_v, 1.0).astype(BF16)


def _swa_proj(xs, mods, norm_g, layer, w, cos, sin, n_lat):
    t, d = xs.shape
    nq = SWA_HEADS * HEAD_DIM
    nz = 2 * SWA_KV * LANES
    return pl.pallas_call(
        _swa_proj_kernel,
        grid=(t // TM,),
        in_specs=[_tile_spec(d), _mod_spec(mods, layer, n_lat), _pick_spec(norm_g.shape, (3 * layer + 1,)),
                  _full_spec(w.shape), _tile_spec(LANES), _tile_spec(LANES)],
        out_specs=[_tile_spec(nq), _tile_spec(nz), _tile_spec(nz)],
        out_shape=[jax.ShapeDtypeStruct((t, nq), BF16), jax.ShapeDtypeStruct((t, nz), BF16),
                   jax.ShapeDtypeStruct((t, nz), BF16)],
        compiler_params=_params(),
        name="swa_proj",
    )(xs, mods, norm_g, w, cos, sin)


def _swa_kernel(sink_ref, q_ref, kp_ref, k0_ref, kn_ref, vp_ref, v0_ref, vn_ref, kc_ref, vc_ref, o_ref,
                kz_ref, vz_ref, *, nb):
    qb = q_ref.shape[0]
    c_len = kc_ref.shape[0]
    i = pl.program_id(0)

    @pl.when(i == 0)
    def _():
        kz_ref[0:c_len, :] = kc_ref[...]
        vz_ref[0:c_len, :] = vc_ref[...]

    for b, (k_ref, v_ref) in enumerate(((kp_ref, vp_ref), (k0_ref, v0_ref), (kn_ref, vn_ref))):
        kz_ref[c_len + b * qb:c_len + (b + 1) * qb, :] = k_ref[...]
        vz_ref[c_len + b * qb:c_len + (b + 1) * qb, :] = v_ref[...]
    r_i = lax.broadcasted_iota(jnp.int32, (qb, qb), 0)
    c_i = lax.broadcasted_iota(jnp.int32, (qb, qb), 1)
    b_prev = jnp.where(jnp.logical_and(c_i >= r_i, jnp.logical_and(i > 0, i < nb)), 0.0, NEG)
    b_own = jnp.where(i < nb, jnp.zeros((qb, qb), F32), NEG)
    b_next = jnp.where(jnp.logical_and(c_i <= r_i, i < nb - 1), 0.0, NEG)
    bias = jnp.concatenate([jnp.zeros((qb, c_len), F32), b_prev, b_own, b_next], axis=1)
    bias = jnp.concatenate([bias, bias], axis=0)
    row = lax.broadcasted_iota(jnp.int32, (2 * qb, 1), 0)
    low = lax.broadcasted_iota(jnp.int32, (1, LANES), 1) < HEAD_DIM
    for j in range(SWA_KV):
        qs = jnp.concatenate([q_ref[:, (2 * j) * LANES:(2 * j + 1) * LANES],
                              q_ref[:, (2 * j + 1) * LANES:(2 * j + 2) * LANES]], axis=0)
        halves = []
        for par in range(2):
            cs = slice((2 * j + par) * LANES, (2 * j + par + 1) * LANES)
            s = _dot_nt(qs, kz_ref[:, cs]) + bias
            sink = jnp.where(row < qb, sink_ref[j * SWA_G + par], sink_ref[j * SWA_G + 2 + par])
            m = jnp.maximum(jnp.max(s, axis=-1, keepdims=True), sink)
            p = jnp.exp(s - m).astype(BF16)
            r = _dot(p, vz_ref[:, cs])
            halves.append(r / (pltpu.roll(r, HEAD_DIM, axis=1) + jnp.exp(sink - m)))
        o = jnp.where(low, halves[0], halves[1]).astype(BF16)
        o_ref[:, (2 * j) * LANES:(2 * j + 1) * LANES] = o[0:qb]
        o_ref[:, (2 * j + 1) * LANES:(2 * j + 2) * LANES] = o[qb:]


def _swa_attn(q, kz, vz, sink, s_len):
    t, nq = q.shape
    nz = kz.shape[1]
    c_len = t - s_len
    nb = s_len // Q_BLOCK
    n_blk = t // Q_BLOCK
    smem = pl.BlockSpec(memory_space=pltpu.SMEM)
    ctx_spec = pl.BlockSpec((c_len, nz), lambda i: (s_len // c_len, 0))
    band = [pl.BlockSpec((Q_BLOCK, nz), lambda i: (jnp.maximum(i - 1, 0), 0)),
            pl.BlockSpec((Q_BLOCK, nz), lambda i: (i, 0)),
            pl.BlockSpec((Q_BLOCK, nz), lambda i: (jnp.minimum(i + 1, n_blk - 1), 0))]
    n_keys = c_len + 3 * Q_BLOCK
    return pl.pallas_call(
        functools.partial(_swa_kernel, nb=nb),
        grid=(n_blk,),
        in_specs=[smem, pl.BlockSpec((Q_BLOCK, nq), lambda i: (i, 0))] + band + band + [ctx_spec, ctx_spec],
        out_specs=pl.BlockSpec((Q_BLOCK, nq), lambda i: (i, 0)),
        out_shape=jax.ShapeDtypeStruct((t, nq), BF16),
        scratch_shapes=[pltpu.VMEM((n_keys, nz), BF16), pltpu.VMEM((n_keys, nz), BF16)],
        compiler_params=_params(),
        name="swa_attn",
    )(sink, q, kz, kz, kz, vz, vz, vz, kz, vz)


def _rope_tables(s_len, c_len):
    pos = jnp.arange(s_len)
    ids = jnp.stack([(pos // GRID_W).astype(F32), (pos % GRID_W).astype(F32)], axis=1)
    freqs = ROPE_BASE ** (-jnp.arange(0, ROT_AXIS, 2, dtype=F32) / ROT_AXIS)
    lane = jnp.arange(LANES)
    dim = lane % HEAD_DIM
    ang = ids[:, dim // ROT_AXIS] * freqs[dim % (ROT_AXIS // 2)][None, :]
    sign = jnp.where((dim % ROT_AXIS) < ROT_AXIS // 2, -1.0, 1.0)
    cos = jnp.concatenate([jnp.cos(ang), jnp.ones((c_len, LANES), F32)], axis=0)
    sin = jnp.concatenate([jnp.sin(ang) * sign, jnp.zeros((c_len, LANES), F32)], axis=0)
    return cos, sin


def _swa_layer(xs, mods, norm_g, layer, w_qkv, sink, w_out, rope, n_lat, s_len):
    q, kz, vz = _swa_proj(xs, mods, norm_g, layer, w_qkv.astype(BF16), rope[0], rope[1], n_lat)
    o = _swa_attn(q, kz, vz, sink, s_len)
    return _out_proj(xs, mods, layer, o, w_out.astype(BF16), n_lat)


def kernel(x, c, ctx, c_ctx, ada_w, ada_b, norm_g, final_g, ffn_w_in, ffn_w_out, gla_w_in, gla_w_a2, gla_b_a, gla_norm_g, gla_w_out, rg_w_in, rg_conv_w, rg_conv_b, rg_w_a, rg_b_a, rg_w_x, rg_b_x, rg_lam, rg_w_out, swa_w_qkv, swa_sink, swa_w_out):
    b, s_len, d = x.shape
    c_len = ctx.shape[1]
    assert b == 1 and s_len % TM == 0 and c_len % TM == 0 and s_len % c_len == 0
    depth = ada_w.shape[0]
    n_lat = s_len // TM
    n_tot = n_lat + c_len // TM
    mods = _ada_vectors(c, c_ctx, ada_w, ada_b)
    ng = norm_g.reshape(depth * 3, 1, d)
    w1 = ffn_w_in.astype(BF16)
    w2 = ffn_w_out.astype(BF16)
    gw = _gla_weights(gla_w_in, gla_w_a2, gla_b_a, gla_norm_g, gla_w_out)
    rope = _rope_tables(s_len, c_len)
    xs = (x[0], ctx[0])
    for i in range(depth):
        m, j = i % N_MIXERS, i // N_MIXERS
        last = i == depth - 1
        n_after = n_lat if last else n_tot
        xs = _ffn(xs, mods, ng, w1, w2, i, 0, n_lat, n_tot)
        if m == 0:
            xs = _gla_layer(xs, mods, ng, i, gw, j, n_lat, n_after)
        elif m == 1:
            xs = _rg_layer(xs, mods, ng, i, rg_w_in[j], rg_conv_w[j], rg_conv_b[j], rg_w_a[j], rg_b_a[j],
                           rg_w_x[j], rg_b_x[j], rg_lam[j], rg_w_out[j], n_lat)
        else:
            xs = _swa_layer(xs, mods, ng, i, swa_w_qkv[j], swa_sink[j], swa_w_out[j], rope, n_lat, s_len)
        xs = _ffn(xs, mods, ng, w1, w2, i, 2, n_lat, n_after, final_g=final_g if last else None)
    return xs[None]
```

```python
import functools
import math

import jax
import jax.numpy as jnp
from jax import lax
from jax.experimental import pallas as pl
from jax.experimental.pallas import tpu as pltpu

F32 = jnp.float32
BF16 = jnp.bfloat16

N_MOD = 9
NORM_EPS = 1e-6
GLA_HEADS = 4
GLA_DK = 128
GLA_DV = 256
GLA_RANK = 16
GLA_LOGIT_NORM = 16.0
GLA_CHUNK = 64
GLA_HK = GLA_HEADS * GLA_DK
GLA_HV = GLA_HEADS * GLA_DV
GLA_LR_PAD = 128
D_RNN = 1536
RG_BLOCKS = 12
RG_BS = D_RNN // RG_BLOCKS
RG_C = 8.0
SWA_HEADS = 16
SWA_KV = 4
SWA_G = SWA_HEADS // SWA_KV
HEAD_DIM = 64
WINDOW = 128
Q_BLOCK = 128
ROT_AXIS = HEAD_DIM // 2
ROPE_BASE = 10000.0
GRID_W = 64
N_MIXERS = 3

TM = 256
SUBLANES = 8
LANES = 128
VMEM_LIMIT = 56 * 1024 * 1024
NEG = -1e30
LOG2E = 1.0 / math.log(2.0)


def _odd_pitch(rows):
    tiles = -(-rows // SUBLANES)
    return (tiles + (tiles % 2 == 0)) * SUBLANES


RG_SEG = TM // SUBLANES
RG_HPITCH = _odd_pitch(RG_SEG)
RG_XPITCH = _odd_pitch(RG_SEG + 2 * SUBLANES)


def _params(n_axes=1):
    return pltpu.CompilerParams(dimension_semantics=("arbitrary",) * n_axes, vmem_limit_bytes=VMEM_LIMIT)


def _dot(a, b):
    return jnp.dot(a, b, preferred_element_type=F32)


def _dot_nt(a, b):
    return lax.dot_general(a, b, (((1,), (1,)), ((), ())), preferred_element_type=F32)


def _dot_tn(a, b):
    return lax.dot_general(a, b, (((0,), (0,)), ((), ())), preferred_element_type=F32)


def _rms(x, g):
    return (x * lax.rsqrt(jnp.mean(x * x, axis=-1, keepdims=True) + NORM_EPS)) * g


def _rms_mod(x, g, shift, scale):
    return _rms(x, g) * (1.0 + scale) + shift


def _sigmoid(x):
    return 1.0 / (1.0 + jnp.exp(-x))


def _silu(x):
    return x * _sigmoid(x)


def _softplus(x):
    return jnp.maximum(x, 0.0) + jnp.log1p(jnp.exp(-jnp.abs(x)))


def _gelu_tanh(x):
    return 0.5 * x * (1.0 + jnp.tanh(math.sqrt(2.0 / math.pi) * (x + 0.044715 * (x * x * x))))


def _row(ref, r):
    return ref[r:r + 1, :]


def _tile_spec(width, tm=TM):
    return pl.BlockSpec((tm, width), lambda i: (i, 0))


def _full_spec(shape):
    nd = len(shape)
    return pl.BlockSpec(shape, lambda i: (0,) * nd, pipeline_mode=pl.Buffered(1))


def _pick_spec(shape, lead):
    tail = shape[len(lead):]
    return pl.BlockSpec((None,) * len(lead) + tuple(tail), lambda i: tuple(lead) + (0,) * len(tail),
                        pipeline_mode=pl.Buffered(1))


def _mod_spec(mods, layer, n_lat):
    _, _, n_mod, d = mods.shape
    return pl.BlockSpec((None, None, n_mod, d), lambda i: (layer, jnp.minimum(i // n_lat, 1), 0, 0))


def _ada_kernel(cc_ref, w_ref, b_ref, o_ref):
    act = _silu(cc_ref[...]).astype(BF16)
    o_ref[...] = _dot(act, w_ref[...].astype(BF16)) + b_ref[...]


def _ada_vectors(c, c_ctx, ada_w, ada_b):
    depth, d, nd = ada_w.shape
    cc = jnp.zeros((SUBLANES, d), F32).at[0].set(c[0]).at[1].set(c_ctx)
    out = pl.pallas_call(
        _ada_kernel,
        grid=(depth, nd // d),
        in_specs=[pl.BlockSpec((SUBLANES, d), lambda l, j: (0, 0)),
                  pl.BlockSpec((None, d, d), lambda l, j: (l, 0, j)),
                  pl.BlockSpec((None, 1, d), lambda l, j: (l, 0, j))],
        out_specs=pl.BlockSpec((None, SUBLANES, d), lambda l, j: (l, 0, j)),
        out_shape=jax.ShapeDtypeStruct((depth, SUBLANES, nd), F32),
        compiler_params=_params(2),
        name="ada_vectors",
    )(cc, ada_w, ada_b.reshape(depth, 1, nd))
    return out.reshape(depth, SUBLANES, N_MOD, d)


def _swiglu_step(x, mod_ref, r0, g_ref, w1_ref, w2_ref):
    h = _rms_mod(x, g_ref[...], _row(mod_ref, r0), _row(mod_ref, r0 + 1)).astype(BF16)
    gu = _dot(h, w1_ref[...])
    f = w2_ref.shape[0]
    a = (_silu(gu[:, :f]) * gu[:, f:]).astype(BF16)
    return x + (0.5 * _row(mod_ref, r0 + 2)) * _dot(a, w2_ref[...])


def _pre_kernel(*refs, proj, n_par, n_lat, split_in):
    refs = list(refs)
    if split_in:
        xl_ref, xc_ref = refs[:2]
        refs = refs[2:]
        x = jnp.where(pl.program_id(0) < n_lat, xl_ref[...], xc_ref[...])
    else:
        x = refs.pop(0)[...]
    mod_ref, g1_ref, w1_ref, w2_ref, gm_ref = refs[:5]
    par, x1_ref, outs = refs[5:5 + n_par], refs[5 + n_par], refs[6 + n_par:]
    x1 = _swiglu_step(x, mod_ref, 0, g1_ref, w1_ref, w2_ref)
    x1_ref[...] = x1
    h = _rms_mod(x1, gm_ref[...], _row(mod_ref, 3), _row(mod_ref, 4)).astype(BF16)
    proj(h, *par, *outs)


def _pre_call(name, proj, xs, mods, norm_g, w1, w2, layer, n_lat, n_tiles, par, par_specs, out_widths, out_dtypes):
    split_in = isinstance(xs, tuple)
    d = (xs[0] if split_in else xs).shape[1]
    if split_in:
        x_specs = [pl.BlockSpec((TM, d), lambda i: (jnp.minimum(i, n_lat - 1), 0)),
                   pl.BlockSpec((TM, d), lambda i: (jnp.maximum(i - n_lat, 0), 0))]
        x_args = list(xs)
    else:
        x_specs, x_args = [_tile_spec(d)], [xs]
    t = n_tiles * TM
    return pl.pallas_call(
        functools.partial(_pre_kernel, proj=proj, n_par=len(par), n_lat=n_lat, split_in=split_in),
        grid=(n_tiles,),
        in_specs=x_specs + [_mod_spec(mods, layer, n_lat), _pick_spec(norm_g.shape, (3 * layer,)),
                            _pick_spec(w1.shape, (layer, 0)), _pick_spec(w2.shape, (layer, 0)),
                            _pick_spec(norm_g.shape, (3 * layer + 1,))] + par_specs,
        out_specs=[_tile_spec(d)] + [_tile_spec(w) for w in out_widths],
        out_shape=[jax.ShapeDtypeStruct((t, d), F32)]
        + [jax.ShapeDtypeStruct((t, w), dt) for w, dt in zip(out_widths, out_dtypes)],
        compiler_params=_params(),
        name=name,
    )(*x_args, mods, norm_g, w1, w2, norm_g, *par)


def _post_kernel(*refs, head, n_head, final):
    x_ref, mod_ref = refs[:2]
    head_refs = refs[2:2 + n_head]
    g2_ref, w1_ref, w2_ref = refs[2 + n_head:5 + n_head]
    o_ref = refs[-1]
    y = head(*head_refs[:-1])
    x2 = x_ref[...] + _row(mod_ref, 5) * _dot(y, head_refs[-1][...])
    x3 = _swiglu_step(x2, mod_ref, 6, g2_ref, w1_ref, w2_ref)
    o_ref[...] = _rms(x3, refs[5 + n_head][...]) if final else x3


def _post_call(name, head, xs, mods, norm_g, w1, w2, layer, n_lat, n_tiles, head_args, head_specs, final_g):
    d = xs.shape[1]
    fin_specs, fin_args = ([_full_spec((1, d))], [final_g.reshape(1, d)]) if final_g is not None else ([], [])
    return pl.pallas_call(
        functools.partial(_post_kernel, head=head, n_head=len(head_args), final=final_g is not None),
        grid=(n_tiles,),
        in_specs=[_tile_spec(d), _mod_spec(mods, layer, n_lat)] + head_specs
        + [_pick_spec(norm_g.shape, (3 * layer + 2,)), _pick_spec(w1.shape, (layer, 1)),
           _pick_spec(w2.shape, (layer, 1))] + fin_specs,
        out_specs=_tile_spec(d),
        out_shape=jax.ShapeDtypeStruct((n_tiles * TM, d), F32),
        compiler_params=_params(),
        name=name,
    )(xs, mods, *head_args, norm_g, w1, w2, *fin_args)


def _gla_proj(h, w_ref, w2_ref, ba_ref, q_ref, k_ref, v_ref, og_ref, gg_ref):
    y = _dot(h, w_ref[...])
    q_ref[...] = (y[:, :GLA_HK] * (GLA_DK ** -0.5)).astype(BF16)
    k_ref[...] = y[:, GLA_HK:2 * GLA_HK].astype(BF16)
    v_ref[...] = y[:, 2 * GLA_HK:2 * GLA_HK + GLA_HV].astype(BF16)
    og_ref[...] = y[:, 2 * GLA_HK + GLA_HV:2 * GLA_HK + 2 * GLA_HV].astype(BF16)
    lr = y[:, 2 * GLA_HK + 2 * GLA_HV:].astype(BF16)
    logit = _dot(lr, w2_ref[...]) + ba_ref[...]
    gg_ref[...] = (-_softplus(-logit)) / GLA_LOGIT_NORM


def _gla_tile(q_ref, k_ref, v_ref, g_ref, o_ref, s_ref, rev):
    c = GLA_CHUNK
    tm = q_ref.shape[0]
    nc = tm // c
    r_i = lax.broadcasted_iota(jnp.int32, (tm, tm), 0)
    c_i = lax.broadcasted_iota(jnp.int32, (tm, tm), 1)
    same = (r_i // c) == (c_i // c)
    tri = jnp.logical_and(same, (c_i >= r_i) if rev else (c_i <= r_i))
    tri_bf = tri.astype(BF16)
    g = g_ref[...]
    g_hi = g.astype(BF16)
    g_lo = (g - g_hi.astype(F32)).astype(BF16)
    b = _dot(tri_bf, g_hi) + _dot(tri_bf, g_lo)
    ends = [b[n * c:n * c + 1, :] if rev else b[(n + 1) * c - 1:(n + 1) * c, :] for n in range(nc)]
    btot = jnp.concatenate([jnp.broadcast_to(e, (c, e.shape[1])) for e in ends], axis=0)
    k = k_ref[...].astype(F32)
    qe = (q_ref[...].astype(F32) * jnp.exp(b)).astype(BF16)
    ke = (k * jnp.exp(-b)).astype(BF16)
    kd = (k * jnp.exp(btot - b)).astype(BF16)
    decs = [jnp.exp(e) for e in ends]
    own = (lax.broadcasted_iota(jnp.int32, (tm, nc * GLA_DK), 0) // c
           == lax.broadcasted_iota(jnp.int32, (tm, nc * GLA_DK), 1) // GLA_DK)
    for h in range(GLA_HEADS):
        ks = slice(h * GLA_DK, (h + 1) * GLA_DK)
        vs = slice(h * GLA_DV, (h + 1) * GLA_DV)
        vh = v_ref[:, vs]
        att = jnp.where(tri, _dot_nt(qe[:, ks], ke[:, ks]), 0.0).astype(BF16)
        o_in = _dot(att, vh)
        kd_bd = jnp.where(own, jnp.concatenate([kd[:, ks]] * nc, axis=1), jnp.zeros((), BF16))
        upd = _dot_tn(vh, kd_bd)
        st = s_ref[h]
        for n in (reversed(range(nc)) if rev else range(nc)):
            rows = slice(n * c, (n + 1) * c)
            o_ref[rows, vs] = (o_in[rows, :] + _dot_nt(qe[rows, ks], st.astype(BF16))).astype(o_ref.dtype)
            st = decs[n][:, ks] * st + upd[:, n * GLA_DK:(n + 1) * GLA_DK]
        s_ref[h] = st


def _gla_kernel(qf_ref, kf_ref, vf_ref, gf_ref, qb_ref, kb_ref, vb_ref, gb_ref, of_ref, ob_ref, sf_ref, sb_ref):
    @pl.when(pl.program_id(0) == 0)
    def _():
        sf_ref[...] = jnp.zeros_like(sf_ref)
        sb_ref[...] = jnp.zeros_like(sb_ref)

    _gla_tile(qf_ref, kf_ref, vf_ref, gf_ref, of_ref, sf_ref, False)
    _gla_tile(qb_ref, kb_ref, vb_ref, gb_ref, ob_ref, sb_ref, True)


def _gla_scan(q, k, v, gg, n_lat):
    t = q.shape[0]
    n_tot = t // TM
    fwd = lambda s: (s + n_lat) % n_tot
    bwd = lambda s: n_tot - 1 - s

    def specs(blk, col):
        return [pl.BlockSpec((TM, GLA_HK), lambda s: (blk(s), 0)), pl.BlockSpec((TM, GLA_HK), lambda s: (blk(s), 0)),
                pl.BlockSpec((TM, GLA_HV), lambda s: (blk(s), 0)), pl.BlockSpec((TM, GLA_HK), lambda s: (blk(s), col))]

    return pl.pallas_call(
        _gla_kernel,
        grid=(n_tot,),
        in_specs=specs(fwd, 0) + specs(bwd, 1),
        out_specs=[pl.BlockSpec((TM, GLA_HV), lambda s: (fwd(s), 0)), pl.BlockSpec((TM, GLA_HV), lambda s: (bwd(s), 0))],
        out_shape=[jax.ShapeDtypeStruct((t, GLA_HV), BF16)] * 2,
        scratch_shapes=[pltpu.VMEM((GLA_HEADS, GLA_DV, GLA_DK), F32)] * 2,
        compiler_params=_params(),
        name="gla_scan",
    )(q, k, v, gg, q, k, v, gg)


def _gla_out(of_ref, ob_ref, og_ref, ng_ref):
    o = of_ref[...].astype(F32) + ob_ref[...].astype(F32)
    ng = ng_ref[...]
    parts = [_rms(o[:, h * GLA_DV:(h + 1) * GLA_DV], ng) for h in range(GLA_HEADS)]
    return (jnp.concatenate(parts, axis=1) * _silu(og_ref[...].astype(F32))).astype(BF16)


def _gla_weights(w_in, w_a2, b_a, gla_norm_g, w_out):
    na, d, n_in = w_in.shape
    n_main = 2 * GLA_HK + 2 * GLA_HV
    w = jnp.pad(w_in.astype(BF16), ((0, 0), (0, 0), (0, n_main + GLA_LR_PAD - n_in)))
    w2 = jnp.zeros((na, GLA_LR_PAD, 2 * GLA_HK), F32)
    w2 = w2.at[:, :GLA_RANK, :GLA_HK].set(w_a2[:, 0]).at[:, GLA_RANK:2 * GLA_RANK, GLA_HK:].set(w_a2[:, 1])
    return (w, w2.astype(BF16), b_a.reshape(na, 1, 2 * GLA_HK), gla_norm_g.reshape(na, 1, GLA_DV), w_out.astype(BF16))


def _gla_layer(xs, mods, norm_g, w1, w2, layer, gw, j, n_lat, n_tot, n_out_tiles, final_g):
    w, wlr, ba, ng, w_out = gw
    x1, q, k, v, og, gg = _pre_call(
        "pre_gla", _gla_proj, xs, mods, norm_g, w1, w2, layer, n_lat, n_tot, [w, wlr, ba],
        [_pick_spec(w.shape, (j,)), _pick_spec(wlr.shape, (j,)), _pick_spec(ba.shape, (j,))],
        [GLA_HK, GLA_HK, GLA_HV, GLA_HV, 2 * GLA_HK], [BF16, BF16, BF16, BF16, F32])
    o_f, o_b = _gla_scan(q, k, v, gg, n_lat)
    return _post_call(
        "post_gla", _gla_out, x1, mods, norm_g, w1, w2, layer, n_lat, n_out_tiles, [o_f, o_b, og, ng, w_out],
        [_tile_spec(GLA_HV)] * 3 + [_pick_spec(ng.shape, (j,)), _pick_spec(w_out.shape, (j,))], final_g)


def _rg_proj(h, w_ref, gate_ref, xr_ref):
    y = _dot(h, w_ref[...])
    gate_ref[...] = _gelu_tanh(y[:, :D_RNN]).astype(BF16)
    xr_ref[...] = y[:, D_RNN:]


def _rg_scan_kernel(xm_ref, xp_ref, xn_ref, cw_ref, cb_ref, w_ref, ba_ref, bx_ref, lam_ref, h_ref,
                    xs_ref, a_ref, u_ref, xp_pad_ref, hp_pad_ref, carry_ref, *, rev, n_lat, n_tot):
    s = pl.program_id(0)
    blk = (n_tot - 1 - s) if rev else (s + n_lat) % n_tot

    @pl.when(s == 0)
    def _():
        carry_ref[...] = jnp.zeros_like(carry_ref)

    tm = xm_ref.shape[0]
    hal = SUBLANES
    seg = tm // SUBLANES
    nt = a_ref.shape[0]
    assert seg == RG_SEG and RG_BS == LANES
    has_prev = jnp.logical_and(blk != 0, blk != n_lat)
    has_next = jnp.logical_and(blk != n_lat - 1, blk != n_tot - 1)
    xs_ref[0:hal, :] = jnp.where(has_prev, xp_ref[...], 0.0)
    xs_ref[hal:hal + tm, :] = xm_ref[...]
    xs_ref[hal + tm:, :] = jnp.where(has_next, xn_ref[...], 0.0)
    cz = -RG_C * _softplus(-lam_ref[...])
    ext = seg + 2 * hal
    for n in range(nt):
        ns = slice(n * LANES, (n + 1) * LANES)
        for sg in range(SUBLANES):
            xp_pad_ref[n, sg * RG_XPITCH:sg * RG_XPITCH + ext, :] = xs_ref[sg * seg:sg * seg + ext, ns]
        xe = [xp_pad_ref[n, pl.ds(k, SUBLANES, stride=RG_XPITCH), :] for k in range(hal - 2, hal + seg + 1)]
        cw = [jnp.broadcast_to(cw_ref[j:j + 1, ns], (SUBLANES, LANES)) for j in range(4)]
        cb = jnp.broadcast_to(cb_ref[:, ns], (SUBLANES, LANES))
        xc = jnp.concatenate(
            [cb + cw[0] * xe[i] + cw[1] * xe[i + 1] + cw[2] * xe[i + 2] + cw[3] * xe[i + 3] for i in range(seg)],
            axis=0)
        z = _dot(xc.astype(BF16), w_ref[n])
        r = 0.5 + 0.5 * jnp.tanh(0.5 * (z[:, :RG_BS] + ba_ref[:, ns]))
        i_g = 0.5 + 0.5 * jnp.tanh(0.5 * (z[:, RG_BS:] + bx_ref[:, ns]))
        a = jnp.exp(cz[:, ns] * r)
        a_ref[n] = a
        y = 1.0 - a * a
        u_ref[n] = jnp.where(y > 0.0, y * lax.rsqrt(y), 0.0) * (i_g * xc)

    def vreg(i):
        return pl.ds(pl.multiple_of(i * SUBLANES, SUBLANES), SUBLANES)

    def local(it, carry):
        i = (seg - 1 - it) if rev else it
        out = []
        for n in range(nt):
            a = a_ref[n, vreg(i), :]
            out += [a * carry[2 * n] + u_ref[n, vreg(i), :], a * carry[2 * n + 1]]
        return tuple(out)

    init = (jnp.zeros((SUBLANES, LANES), F32), jnp.ones((SUBLANES, LANES), F32)) * nt
    ends = lax.fori_loop(0, seg, local, init, unroll=4)
    cmats = []
    for n in range(nt):
        ns = slice(n * LANES, (n + 1) * LANES)
        hl_end, p_end = ends[2 * n], ends[2 * n + 1]
        c = carry_ref[:, ns]
        cin = [None] * SUBLANES
        for sg in (reversed(range(SUBLANES)) if rev else range(SUBLANES)):
            cin[sg] = c
            c = hl_end[sg:sg + 1, :] + p_end[sg:sg + 1, :] * c
        carry_ref[:, ns] = c
        cmats.append(jnp.concatenate(cin, axis=0))

    def final(it, carry):
        i = (seg - 1 - it) if rev else it
        out = []
        for n in range(nt):
            h = a_ref[n, vreg(i), :] * carry[n] + u_ref[n, vreg(i), :]
            hp_pad_ref[n, pl.ds(i, SUBLANES, stride=RG_HPITCH), :] = h
            out.append(h)
        return tuple(out)

    lax.fori_loop(0, seg, final, tuple(cmats), unroll=4)
    for n in range(nt):
        for sg in range(SUBLANES):
            h_ref[sg * seg:(sg + 1) * seg, n * LANES:(n + 1) * LANES] = (
                hp_pad_ref[n, sg * RG_HPITCH:sg * RG_HPITCH + seg, :].astype(h_ref.dtype))


def _rg_scan(xr, cw, cb, w, ba, bx, lam, z, n_lat):
    t, dr = xr.shape
    n_tot = t // TM
    per = TM // SUBLANES
    n8 = t // SUBLANES
    rev = z == 1
    if rev:
        blk = lambda s: n_tot - 1 - s
    else:
        blk = lambda s: (s + n_lat) % n_tot
    return pl.pallas_call(
        functools.partial(_rg_scan_kernel, rev=rev, n_lat=n_lat, n_tot=n_tot),
        grid=(n_tot,),
        in_specs=[pl.BlockSpec((TM, dr), lambda s: (blk(s), 0)),
                  pl.BlockSpec((SUBLANES, dr), lambda s: (jnp.maximum(blk(s) * per - 1, 0), 0)),
                  pl.BlockSpec((SUBLANES, dr), lambda s: (jnp.minimum((blk(s) + 1) * per, n8 - 1), 0)),
                  _full_spec(cw.shape), _full_spec(cb.shape), _pick_spec(w.shape, (z,)),
                  _pick_spec(ba.shape, (z,)), _pick_spec(bx.shape, (z,)), _pick_spec(lam.shape, (z,))],
        out_specs=pl.BlockSpec((TM, dr), lambda s: (blk(s), 0)),
        out_shape=jax.ShapeDtypeStruct((t, dr), BF16),
        scratch_shapes=[pltpu.VMEM((TM + 2 * SUBLANES, dr), F32),
                        pltpu.VMEM((dr // LANES, TM, LANES), F32), pltpu.VMEM((dr // LANES, TM, LANES), F32),
                        pltpu.VMEM((dr // LANES, SUBLANES * RG_XPITCH, LANES), F32),
                        pltpu.VMEM((dr // LANES, SUBLANES * RG_HPITCH, LANES), F32),
                        pltpu.VMEM((1, dr), F32)],
        compiler_params=_params(),
        name="rg_scan_bwd" if rev else "rg_scan_fwd",
    )(xr, xr, xr, cw, cb, w, ba, bx, lam)


def _rg_out(hf_ref, hb_ref, gate_ref):
    return ((hf_ref[...].astype(F32) + hb_ref[...].astype(F32)) * gate_ref[...].astype(F32)).astype(BF16)


def _rg_layer(xs, mods, norm_g, w1, w2, layer, w_in, conv_w, conv_b, w_a, b_a, w_x, b_x, lam, w_out, n_lat, n_tot,
              n_out_tiles, final_g):
    w_in = w_in.astype(BF16)
    x1, gate, xr = _pre_call("pre_rg", _rg_proj, xs, mods, norm_g, w1, w2, layer, n_lat, n_tot, [w_in],
                             [_full_spec(w_in.shape)], [D_RNN, D_RNN], [BF16, F32])
    w = jnp.concatenate([w_a, w_x], axis=-1).astype(BF16)
    hs = [_rg_scan(xr, conv_w, conv_b.reshape(1, D_RNN), w, b_a.reshape(2, 1, D_RNN), b_x.reshape(2, 1, D_RNN),
                   lam.reshape(2, 1, D_RNN), z, n_lat) for z in range(2)]
    w_out = w_out.astype(BF16)
    return _post_call("post_rg", _rg_out, x1, mods, norm_g, w1, w2, layer, n_lat, n_out_tiles,
                      [hs[0], hs[1], gate, w_out], [_tile_spec(D_RNN)] * 3 + [_full_spec(w_out.shape)], final_g)


def _swa_proj(h, w_ref, cos_ref, sin_ref, q_ref, kz_ref, vz_ref):
    y = _dot(h, w_ref[...])
    nq = SWA_HEADS * HEAD_DIM
    nk = SWA_KV * HEAD_DIM
    cos = cos_ref[...]
    sin = sin_ref[...]
    rot = ROT_AXIS // 2
    lane = lax.broadcasted_iota(jnp.int32, (1, LANES), 1)
    first = (lane % ROT_AXIS) < rot

    def rope(xg):
        partner = jnp.where(first, pltpu.roll(xg, LANES - rot, axis=1), pltpu.roll(xg, rot, axis=1))
        return xg * cos + partner * sin

    q = y[:, :nq] * (HEAD_DIM ** -0.5 * LOG2E)
    for gi in range(nq // LANES):
        q_ref[:, gi * LANES:(gi + 1) * LANES] = rope(q[:, gi * LANES:(gi + 1) * LANES]).astype(BF16)
    low = lane < HEAD_DIM
    for gi in range(nk // LANES):
        kg = rope(y[:, nq + gi * LANES:nq + (gi + 1) * LANES])
        vg = y[:, nq + nk + gi * LANES:nq + nk + (gi + 1) * LANES]
        for par_j in range(2):
            j = 2 * gi + par_j
            for par_q in range(2):
                src_k = kg if par_q == par_j else pltpu.roll(kg, HEAD_DIM, axis=1)
                src_v = vg if par_q == par_j else pltpu.roll(vg, HEAD_DIM, axis=1)
                keep = low if par_q == 0 else jnp.logical_not(low)
                cs = slice((2 * j + par_q) * LANES, (2 * j + par_q + 1) * LANES)
                kz_ref[:, cs] = jnp.where(keep, src_k, 0.0).astype(BF16)
                vz_ref[:, cs] = jnp.where(keep, src_v, 1.0).astype(BF16)


def _swa_kernel(sink_ref, q_ref, kp_ref, k0_ref, kn_ref, vp_ref, v0_ref, vn_ref, kc_ref, vc_ref, o_ref,
                kz_ref, vz_ref, *, nb):
    qb = q_ref.shape[0]
    c_len = kc_ref.shape[0]
    i = pl.program_id(0)

    @pl.when(i == 0)
    def _():
        kz_ref[0:c_len, :] = kc_ref[...]
        vz_ref[0:c_len, :] = vc_ref[...]

    row = lax.broadcasted_iota(jnp.int32, (2 * qb, 1), 0)
    low = lax.broadcasted_iota(jnp.int32, (1, LANES), 1) < HEAD_DIM

    def attend(n_keys, mask):
        for j in range(SWA_KV):
            qs = jnp.concatenate([q_ref[:, (2 * j) * LANES:(2 * j + 1) * LANES],
                                  q_ref[:, (2 * j + 1) * LANES:(2 * j + 2) * LANES]], axis=0)
            halves = []
            for par in range(2):
                cs = slice((2 * j + par) * LANES, (2 * j + par + 1) * LANES)
                s = mask(_dot_nt(qs, kz_ref[0:n_keys, cs]))
                sink = LOG2E * jnp.where(row < qb, sink_ref[j * SWA_G + par], sink_ref[j * SWA_G + 2 + par])
                m = jnp.maximum(jnp.max(s, axis=-1, keepdims=True), sink)
                p = jnp.exp2(s - m).astype(BF16)
                r = _dot(p, vz_ref[0:n_keys, cs])
                halves.append(r / (pltpu.roll(r, HEAD_DIM, axis=1) + jnp.exp2(sink - m)))
            o = jnp.where(low, halves[0], halves[1]).astype(BF16)
            o_ref[:, (2 * j) * LANES:(2 * j + 1) * LANES] = o[0:qb]
            o_ref[:, (2 * j + 1) * LANES:(2 * j + 2) * LANES] = o[qb:]

    @pl.when(i < nb)
    def _():
        for b, (k_ref, v_ref) in enumerate(((kp_ref, vp_ref), (k0_ref, v0_ref), (kn_ref, vn_ref))):
            kz_ref[c_len + b * qb:c_len + (b + 1) * qb, :] = k_ref[...]
            vz_ref[c_len + b * qb:c_len + (b + 1) * qb, :] = v_ref[...]
        r_i = lax.broadcasted_iota(jnp.int32, (2 * qb, qb), 0) % qb
        c_i = lax.broadcasted_iota(jnp.int32, (2 * qb, qb), 1)
        b_prev = jnp.where(jnp.logical_and(c_i >= r_i, i > 0), 0.0, NEG)
        b_next = jnp.where(jnp.logical_and(c_i <= r_i, i < nb - 1), 0.0, NEG)

        def mask(s):
            return jnp.concatenate([s[:, :c_len], s[:, c_len:c_len + qb] + b_prev,
                                    s[:, c_len + qb:c_len + 2 * qb], s[:, c_len + 2 * qb:] + b_next], axis=1)

        attend(c_len + 3 * qb, mask)

    @pl.when(i >= nb)
    def _():
        attend(c_len, lambda s: s)


def _swa_attn(q, kz, vz, sink, s_len):
    t, nq = q.shape
    nz = kz.shape[1]
    c_len = t - s_len
    nb = s_len // Q_BLOCK
    n_blk = t // Q_BLOCK
    smem = pl.BlockSpec(memory_space=pltpu.SMEM)
    ctx_spec = pl.BlockSpec((c_len, nz), lambda i: (s_len // c_len, 0))
    band = [pl.BlockSpec((Q_BLOCK, nz), lambda i: (jnp.maximum(i - 1, 0), 0)),
            pl.BlockSpec((Q_BLOCK, nz), lambda i: (i, 0)),
            pl.BlockSpec((Q_BLOCK, nz), lambda i: (jnp.minimum(i + 1, n_blk - 1), 0))]
    n_keys = c_len + 3 * Q_BLOCK
    return pl.pallas_call(
        functools.partial(_swa_kernel, nb=nb),
        grid=(n_blk,),
        in_specs=[smem, pl.BlockSpec((Q_BLOCK, nq), lambda i: (i, 0))] + band + band + [ctx_spec, ctx_spec],
        out_specs=pl.BlockSpec((Q_BLOCK, nq), lambda i: (i, 0)),
        out_shape=jax.ShapeDtypeStruct((t, nq), BF16),
        scratch_shapes=[pltpu.VMEM((n_keys, nz), BF16), pltpu.VMEM((n_keys, nz), BF16)],
        compiler_params=_params(),
        name="swa_attn",
    )(sink, q, kz, kz, kz, vz, vz, vz, kz, vz)


def _rope_tables(s_len, c_len):
    pos = jnp.arange(s_len)
    ids = jnp.stack([(pos // GRID_W).astype(F32), (pos % GRID_W).astype(F32)], axis=1)
    freqs = ROPE_BASE ** (-jnp.arange(0, ROT_AXIS, 2, dtype=F32) / ROT_AXIS)
    lane = jnp.arange(LANES)
    dim = lane % HEAD_DIM
    ang = ids[:, dim // ROT_AXIS] * freqs[dim % (ROT_AXIS // 2)][None, :]
    sign = jnp.where((dim % ROT_AXIS) < ROT_AXIS // 2, -1.0, 1.0)
    cos = jnp.concatenate([jnp.cos(ang), jnp.ones((c_len, LANES), F32)], axis=0)
    sin = jnp.concatenate([jnp.sin(ang) * sign, jnp.zeros((c_len, LANES), F32)], axis=0)
    return cos, sin


def _swa_out(o_ref):
    return o_ref[...]


def _swa_layer(xs, mods, norm_g, w1, w2, layer, w_qkv, sink, w_out, rope, n_lat, n_tot, n_out_tiles, s_len, final_g):
    w_qkv = w_qkv.astype(BF16)
    nq = SWA_HEADS * HEAD_DIM
    nz = 2 * SWA_KV * LANES
    x1, q, kz, vz = _pre_call("pre_swa", _swa_proj, xs, mods, norm_g, w1, w2, layer, n_lat, n_tot,
                              [w_qkv, rope[0], rope[1]], [_full_spec(w_qkv.shape), _tile_spec(LANES), _tile_spec(LANES)],
                              [nq, nz, nz], [BF16, BF16, BF16])
    o = _swa_attn(q, kz, vz, sink, s_len)
    w_out = w_out.astype(BF16)
    return _post_call("post_swa", _swa_out, x1, mods, norm_g, w1, w2, layer, n_lat, n_out_tiles, [o, w_out],
                      [_tile_spec(nq), _full_spec(w_out.shape)], final_g)


def kernel(x, c, ctx, c_ctx, ada_w, ada_b, norm_g, final_g, ffn_w_in, ffn_w_out, gla_w_in, gla_w_a2, gla_b_a, gla_norm_g, gla_w_out, rg_w_in, rg_conv_w, rg_conv_b, rg_w_a, rg_b_a, rg_w_x, rg_b_x, rg_lam, rg_w_out, swa_w_qkv, swa_sink, swa_w_out):
    b, s_len, d = x.shape
    c_len = ctx.shape[1]
    assert b == 1 and s_len % TM == 0 and c_len % TM == 0 and s_len % c_len == 0
    depth = ada_w.shape[0]
    n_lat = s_len // TM
    n_tot = n_lat + c_len // TM
    mods = _ada_vectors(c, c_ctx, ada_w, ada_b)
    ng = norm_g.reshape(depth * 3, 1, d)
    w1 = ffn_w_in.astype(BF16)
    w2 = ffn_w_out.astype(BF16)
    gw = _gla_weights(gla_w_in, gla_w_a2, gla_b_a, gla_norm_g, gla_w_out)
    rope = _rope_tables(s_len, c_len)
    xs = (x[0], ctx[0])
    for i in range(depth):
        m, j = i % N_MIXERS, i // N_MIXERS
        last = i == depth - 1
        n_after = n_lat if last else n_tot
        fin = final_g if last else None
        if m == 0:
            xs = _gla_layer(xs, mods, ng, w1, w2, i, gw, j, n_lat, n_tot, n_after, fin)
        elif m == 1:
            xs = _rg_layer(xs, mods, ng, w1, w2, i, rg_w_in[j], rg_conv_w[j], rg_conv_b[j], rg_w_a[j], rg_b_a[j],
                           rg_w_x[j], rg_b_x[j], rg_lam[j], rg_w_out[j], n_lat, n_tot, n_after, fin)
        else:
            xs = _swa_layer(xs, mods, ng, w1, w2, i, swa_w_qkv[j], swa_sink[j], swa_w_out[j], rope, n_lat, n_tot,
                            n_after, s_len, fin)
    return xs[None]
```

```python
import functools
import math

import jax
import jax.numpy as jnp
from jax import lax
from jax.experimental import pallas as pl
from jax.experimental.pallas import tpu as pltpu

F32 = jnp.float32
BF16 = jnp.bfloat16

N_MOD = 9
NORM_EPS = 1e-6
GLA_HEADS = 4
GLA_DK = 128
GLA_DV = 256
GLA_RANK = 16
GLA_LOGIT_NORM = 16.0
GLA_CHUNK = 64
GLA_HK = GLA_HEADS * GLA_DK
GLA_HV = GLA_HEADS * GLA_DV
GLA_LR_PAD = 128
D_RNN = 1536
RG_BLOCKS = 12
RG_BS = D_RNN // RG_BLOCKS
RG_C = 8.0
SWA_HEADS = 16
SWA_KV = 4
SWA_G = SWA_HEADS // SWA_KV
HEAD_DIM = 64
WINDOW = 128
Q_BLOCK = 128
ROT_AXIS = HEAD_DIM // 2
ROPE_BASE = 10000.0
GRID_W = 64
N_MIXERS = 3

TM = 256
SUBLANES = 8
LANES = 128
VMEM_LIMIT = 56 * 1024 * 1024
NEG = -1e30
LOG2E = 1.0 / math.log(2.0)


def _odd_pitch(rows):
    tiles = -(-rows // SUBLANES)
    return (tiles + (tiles % 2 == 0)) * SUBLANES


RG_SEG = TM // SUBLANES
RG_HPITCH = _odd_pitch(RG_SEG)
RG_XPITCH = _odd_pitch(RG_SEG + 2 * SUBLANES)


def _params(n_axes=1):
    return pltpu.CompilerParams(dimension_semantics=("arbitrary",) * n_axes, vmem_limit_bytes=VMEM_LIMIT)


def _dot(a, b):
    return jnp.dot(a, b, preferred_element_type=F32)


def _dot_nt(a, b):
    return lax.dot_general(a, b, (((1,), (1,)), ((), ())), preferred_element_type=F32)


def _dot_tn(a, b):
    return lax.dot_general(a, b, (((0,), (0,)), ((), ())), preferred_element_type=F32)


def _rms(x, g):
    return (x * lax.rsqrt(jnp.mean(x * x, axis=-1, keepdims=True) + NORM_EPS)) * g


def _rms_mod(x, g, shift, scale):
    return _rms(x, g) * (1.0 + scale) + shift


def _sigmoid(x):
    return 1.0 / (1.0 + jnp.exp(-x))


def _silu(x):
    return x * _sigmoid(x)


def _softplus(x):
    return jnp.maximum(x, 0.0) + jnp.log1p(jnp.exp(-jnp.abs(x)))


def _gelu_tanh(x):
    return 0.5 * x * (1.0 + jnp.tanh(math.sqrt(2.0 / math.pi) * (x + 0.044715 * (x * x * x))))


def _row(ref, r):
    return ref[r:r + 1, :]


def _tile_spec(width, tm=TM):
    return pl.BlockSpec((tm, width), lambda i: (i, 0))


def _full_spec(shape):
    nd = len(shape)
    return pl.BlockSpec(shape, lambda i: (0,) * nd, pipeline_mode=pl.Buffered(1))


def _pick_spec(shape, lead):
    tail = shape[len(lead):]
    return pl.BlockSpec((None,) * len(lead) + tuple(tail), lambda i: tuple(lead) + (0,) * len(tail),
                        pipeline_mode=pl.Buffered(1))


def _mod_spec(mods, layer, n_lat):
    _, _, n_mod, d = mods.shape
    return pl.BlockSpec((None, None, n_mod, d), lambda i: (layer, jnp.minimum(i // n_lat, 1), 0, 0))


def _ada_kernel(cc_ref, w_ref, b_ref, o_ref):
    act = _silu(cc_ref[...]).astype(BF16)
    o_ref[...] = _dot(act, w_ref[...].astype(BF16)) + b_ref[...]


def _ada_vectors(c, c_ctx, ada_w, ada_b):
    depth, d, nd = ada_w.shape
    cc = jnp.zeros((SUBLANES, d), F32).at[0].set(c[0]).at[1].set(c_ctx)
    out = pl.pallas_call(
        _ada_kernel,
        grid=(depth, nd // d),
        in_specs=[pl.BlockSpec((SUBLANES, d), lambda l, j: (0, 0)),
                  pl.BlockSpec((None, d, d), lambda l, j: (l, 0, j)),
                  pl.BlockSpec((None, 1, d), lambda l, j: (l, 0, j))],
        out_specs=pl.BlockSpec((None, SUBLANES, d), lambda l, j: (l, 0, j)),
        out_shape=jax.ShapeDtypeStruct((depth, SUBLANES, nd), F32),
        compiler_params=_params(2),
        name="ada_vectors",
    )(cc, ada_w, ada_b.reshape(depth, 1, nd))
    return out.reshape(depth, SUBLANES, N_MOD, d)


def _swiglu_step(x, mod_ref, r0, g_ref, w1_ref, w2_ref):
    h = _rms_mod(x, g_ref[...], _row(mod_ref, r0), _row(mod_ref, r0 + 1)).astype(BF16)
    gu = _dot(h, w1_ref[...])
    f = w2_ref.shape[0]
    a = (_silu(gu[:, :f]) * gu[:, f:]).astype(BF16)
    return x + (0.5 * _row(mod_ref, r0 + 2)) * _dot(a, w2_ref[...])


def _pre_kernel(*refs, proj, n_par, n_lat, split_in):
    refs = list(refs)
    if split_in:
        xl_ref, xc_ref = refs[:2]
        refs = refs[2:]
        x = jnp.where(pl.program_id(0) < n_lat, xl_ref[...], xc_ref[...])
    else:
        x = refs.pop(0)[...]
    mod_ref, g1_ref, w1_ref, w2_ref, gm_ref = refs[:5]
    par, x1_ref, outs = refs[5:5 + n_par], refs[5 + n_par], refs[6 + n_par:]
    x1 = _swiglu_step(x, mod_ref, 0, g1_ref, w1_ref, w2_ref)
    x1_ref[...] = x1
    h = _rms_mod(x1, gm_ref[...], _row(mod_ref, 3), _row(mod_ref, 4)).astype(BF16)
    proj(h, *par, *outs)


def _pre_call(name, proj, xs, mods, norm_g, w1, w2, layer, n_lat, n_tiles, par, par_specs, out_widths, out_dtypes):
    split_in = isinstance(xs, tuple)
    d = (xs[0] if split_in else xs).shape[1]
    if split_in:
        x_specs = [pl.BlockSpec((TM, d), lambda i: (jnp.minimum(i, n_lat - 1), 0)),
                   pl.BlockSpec((TM, d), lambda i: (jnp.maximum(i - n_lat, 0), 0))]
        x_args = list(xs)
    else:
        x_specs, x_args = [_tile_spec(d)], [xs]
    t = n_tiles * TM
    return pl.pallas_call(
        functools.partial(_pre_kernel, proj=proj, n_par=len(par), n_lat=n_lat, split_in=split_in),
        grid=(n_tiles,),
        in_specs=x_specs + [_mod_spec(mods, layer, n_lat), _pick_spec(norm_g.shape, (3 * layer,)),
                            _pick_spec(w1.shape, (layer, 0)), _pick_spec(w2.shape, (layer, 0)),
                            _pick_spec(norm_g.shape, (3 * layer + 1,))] + par_specs,
        out_specs=[_tile_spec(d)] + [_tile_spec(w) for w in out_widths],
        out_shape=[jax.ShapeDtypeStruct((t, d), F32)]
        + [jax.ShapeDtypeStruct((t, w), dt) for w, dt in zip(out_widths, out_dtypes)],
        compiler_params=_params(),
        name=name,
    )(*x_args, mods, norm_g, w1, w2, norm_g, *par)


def _post_kernel(*refs, head, n_head, final):
    x_ref, mod_ref = refs[:2]
    head_refs = refs[2:2 + n_head]
    g2_ref, w1_ref, w2_ref = refs[2 + n_head:5 + n_head]
    o_ref = refs[-1]
    y = head(*head_refs[:-1])
    x2 = x_ref[...] + _row(mod_ref, 5) * _dot(y, head_refs[-1][...])
    x3 = _swiglu_step(x2, mod_ref, 6, g2_ref, w1_ref, w2_ref)
    o_ref[...] = _rms(x3, refs[5 + n_head][...]) if final else x3


def _post_call(name, head, xs, mods, norm_g, w1, w2, layer, n_lat, n_tiles, head_args, head_specs, final_g):
    d = xs.shape[1]
    fin_specs, fin_args = ([_full_spec((1, d))], [final_g.reshape(1, d)]) if final_g is not None else ([], [])
    return pl.pallas_call(
        functools.partial(_post_kernel, head=head, n_head=len(head_args), final=final_g is not None),
        grid=(n_tiles,),
        in_specs=[_tile_spec(d), _mod_spec(mods, layer, n_lat)] + head_specs
        + [_pick_spec(norm_g.shape, (3 * layer + 2,)), _pick_spec(w1.shape, (layer, 1)),
           _pick_spec(w2.shape, (layer, 1))] + fin_specs,
        out_specs=_tile_spec(d),
        out_shape=jax.ShapeDtypeStruct((n_tiles * TM, d), F32),
        compiler_params=_params(),
        name=name,
    )(xs, mods, *head_args, norm_g, w1, w2, *fin_args)


def _gla_proj(h, w_ref, w2_ref, ba_ref, q_ref, k_ref, v_ref, og_ref, gg_ref):
    y = _dot(h, w_ref[...])
    q_ref[...] = (y[:, :GLA_HK] * (GLA_DK ** -0.5)).astype(BF16)
    k_ref[...] = y[:, GLA_HK:2 * GLA_HK].astype(BF16)
    v_ref[...] = y[:, 2 * GLA_HK:2 * GLA_HK + GLA_HV].astype(BF16)
    og_ref[...] = y[:, 2 * GLA_HK + GLA_HV:2 * GLA_HK + 2 * GLA_HV].astype(BF16)
    lr = y[:, 2 * GLA_HK + 2 * GLA_HV:].astype(BF16)
    logit = _dot(lr, w2_ref[...]) + ba_ref[...]
    gg_ref[...] = (-_softplus(-logit)) / GLA_LOGIT_NORM


def _gla_tile(q_ref, k_ref, v_ref, g_ref, o_ref, s_ref, rev):
    c = GLA_CHUNK
    tm = q_ref.shape[0]
    nc = tm // c
    r_i = lax.broadcasted_iota(jnp.int32, (tm, tm), 0)
    c_i = lax.broadcasted_iota(jnp.int32, (tm, tm), 1)
    same = (r_i // c) == (c_i // c)
    tri = jnp.logical_and(same, (c_i >= r_i) if rev else (c_i <= r_i))
    tri_bf = tri.astype(BF16)
    g = g_ref[...]
    g_hi = g.astype(BF16)
    g_lo = (g - g_hi.astype(F32)).astype(BF16)
    b = _dot(tri_bf, g_hi) + _dot(tri_bf, g_lo)
    ends = [b[n * c:n * c + 1, :] if rev else b[(n + 1) * c - 1:(n + 1) * c, :] for n in range(nc)]
    btot = jnp.concatenate([jnp.broadcast_to(e, (c, e.shape[1])) for e in ends], axis=0)
    k = k_ref[...].astype(F32)
    qe = (q_ref[...].astype(F32) * jnp.exp(b)).astype(BF16)
    ke = (k * jnp.exp(-b)).astype(BF16)
    kd = (k * jnp.exp(btot - b)).astype(BF16)
    decs = [jnp.exp(e) for e in ends]
    own = (lax.broadcasted_iota(jnp.int32, (tm, nc * GLA_DK), 0) // c
           == lax.broadcasted_iota(jnp.int32, (tm, nc * GLA_DK), 1) // GLA_DK)
    for h in range(GLA_HEADS):
        ks = slice(h * GLA_DK, (h + 1) * GLA_DK)
        vs = slice(h * GLA_DV, (h + 1) * GLA_DV)
        vh = v_ref[:, vs]
        att = jnp.where(tri, _dot_nt(qe[:, ks], ke[:, ks]), 0.0).astype(BF16)
        o_in = _dot(att, vh)
        kd_bd = jnp.where(own, jnp.concatenate([kd[:, ks]] * nc, axis=1), jnp.zeros((), BF16))
        upd = _dot_tn(vh, kd_bd)
        st = s_ref[h]
        for n in (reversed(range(nc)) if rev else range(nc)):
            rows = slice(n * c, (n + 1) * c)
            o_ref[rows, vs] = (o_in[rows, :] + _dot_nt(qe[rows, ks], st.astype(BF16))).astype(o_ref.dtype)
            st = decs[n][:, ks] * st + upd[:, n * GLA_DK:(n + 1) * GLA_DK]
        s_ref[h] = st


def _gla_kernel(qf_ref, kf_ref, vf_ref, gf_ref, qb_ref, kb_ref, vb_ref, gb_ref, of_ref, ob_ref, sf_ref, sb_ref):
    @pl.when(pl.program_id(0) == 0)
    def _():
        sf_ref[...] = jnp.zeros_like(sf_ref)
        sb_ref[...] = jnp.zeros_like(sb_ref)

    _gla_tile(qf_ref, kf_ref, vf_ref, gf_ref, of_ref, sf_ref, False)
    _gla_tile(qb_ref, kb_ref, vb_ref, gb_ref, ob_ref, sb_ref, True)


def _gla_scan(q, k, v, gg, n_lat):
    t = q.shape[0]
    n_tot = t // TM
    fwd = lambda s: (s + n_lat) % n_tot
    bwd = lambda s: n_tot - 1 - s

    def specs(blk, col):
        return [pl.BlockSpec((TM, GLA_HK), lambda s: (blk(s), 0)), pl.BlockSpec((TM, GLA_HK), lambda s: (blk(s), 0)),
                pl.BlockSpec((TM, GLA_HV), lambda s: (blk(s), 0)), pl.BlockSpec((TM, GLA_HK), lambda s: (blk(s), col))]

    return pl.pallas_call(
        _gla_kernel,
        grid=(n_tot,),
        in_specs=specs(fwd, 0) + specs(bwd, 1),
        out_specs=[pl.BlockSpec((TM, GLA_HV), lambda s: (fwd(s), 0)), pl.BlockSpec((TM, GLA_HV), lambda s: (bwd(s), 0))],
        out_shape=[jax.ShapeDtypeStruct((t, GLA_HV), BF16)] * 2,
        scratch_shapes=[pltpu.VMEM((GLA_HEADS, GLA_DV, GLA_DK), F32)] * 2,
        compiler_params=_params(),
        name="gla_scan",
    )(q, k, v, gg, q, k, v, gg)


def _gla_out(of_ref, ob_ref, og_ref, ng_ref):
    o = of_ref[...].astype(F32) + ob_ref[...].astype(F32)
    ng = ng_ref[...]
    parts = [_rms(o[:, h * GLA_DV:(h + 1) * GLA_DV], ng) for h in range(GLA_HEADS)]
    return (jnp.concatenate(parts, axis=1) * _silu(og_ref[...].astype(F32))).astype(BF16)


def _gla_weights(w_in, w_a2, b_a, gla_norm_g, w_out):
    na, d, n_in = w_in.shape
    n_main = 2 * GLA_HK + 2 * GLA_HV
    w = jnp.pad(w_in.astype(BF16), ((0, 0), (0, 0), (0, n_main + GLA_LR_PAD - n_in)))
    w2 = jnp.zeros((na, GLA_LR_PAD, 2 * GLA_HK), F32)
    w2 = w2.at[:, :GLA_RANK, :GLA_HK].set(w_a2[:, 0]).at[:, GLA_RANK:2 * GLA_RANK, GLA_HK:].set(w_a2[:, 1])
    return (w, w2.astype(BF16), b_a.reshape(na, 1, 2 * GLA_HK), gla_norm_g.reshape(na, 1, GLA_DV), w_out.astype(BF16))


def _gla_layer(xs, mods, norm_g, w1, w2, layer, gw, j, n_lat, n_tot, n_out_tiles, final_g):
    w, wlr, ba, ng, w_out = gw
    x1, q, k, v, og, gg = _pre_call(
        "pre_gla", _gla_proj, xs, mods, norm_g, w1, w2, layer, n_lat, n_tot, [w, wlr, ba],
        [_pick_spec(w.shape, (j,)), _pick_spec(wlr.shape, (j,)), _pick_spec(ba.shape, (j,))],
        [GLA_HK, GLA_HK, GLA_HV, GLA_HV, 2 * GLA_HK], [BF16, BF16, BF16, BF16, F32])
    o_f, o_b = _gla_scan(q, k, v, gg, n_lat)
    return _post_call(
        "post_gla", _gla_out, x1, mods, norm_g, w1, w2, layer, n_lat, n_out_tiles, [o_f, o_b, og, ng, w_out],
        [_tile_spec(GLA_HV)] * 3 + [_pick_spec(ng.shape, (j,)), _pick_spec(w_out.shape, (j,))], final_g)


def _rg_proj(h, w_ref, gate_ref, xr_ref):
    y = _dot(h, w_ref[...])
    gate_ref[...] = _gelu_tanh(y[:, :D_RNN]).astype(BF16)
    xr_ref[...] = y[:, D_RNN:]


def _rg_scan_kernel(xm_ref, xp_ref, xn_ref, cw_ref, cb_ref, w_ref, ba_ref, bx_ref, lam_ref, h_ref,
                    xs_ref, a_ref, u_ref, xp_pad_ref, hp_pad_ref, carry_ref, *, rev, n_lat, n_tot):
    s = pl.program_id(0)
    blk = (n_tot - 1 - s) if rev else (s + n_lat) % n_tot

    @pl.when(s == 0)
    def _():
        carry_ref[...] = jnp.zeros_like(carry_ref)

    tm = xm_ref.shape[0]
    hal = SUBLANES
    seg = tm // SUBLANES
    nt = a_ref.shape[0]
    assert seg == RG_SEG and RG_BS == LANES
    has_prev = jnp.logical_and(blk != 0, blk != n_lat)
    has_next = jnp.logical_and(blk != n_lat - 1, blk != n_tot - 1)
    xs_ref[0:hal, :] = jnp.where(has_prev, xp_ref[...], 0.0)
    xs_ref[hal:hal + tm, :] = xm_ref[...]
    xs_ref[hal + tm:, :] = jnp.where(has_next, xn_ref[...], 0.0)
    czh = (-0.5 * RG_C) * _softplus(-lam_ref[...])
    ext = seg + 2 * hal
    for n in range(nt):
        ns = slice(n * LANES, (n + 1) * LANES)
        for sg in range(SUBLANES):
            xp_pad_ref[n, sg * RG_XPITCH:sg * RG_XPITCH + ext, :] = xs_ref[sg * seg:sg * seg + ext, ns]
        xe = [xp_pad_ref[n, pl.ds(k, SUBLANES, stride=RG_XPITCH), :] for k in range(hal - 2, hal + seg + 1)]
        cw = [jnp.broadcast_to(cw_ref[j:j + 1, ns], (SUBLANES, LANES)) for j in range(4)]
        cb = jnp.broadcast_to(cb_ref[:, ns], (SUBLANES, LANES))
        xc = jnp.concatenate(
            [cb + cw[0] * xe[i] + cw[1] * xe[i + 1] + cw[2] * xe[i + 2] + cw[3] * xe[i + 3] for i in range(seg)],
            axis=0)
        z = _dot(xc.astype(BF16), w_ref[n])
        t_r = jnp.tanh(z[:, :RG_BS] + ba_ref[:, ns])
        t_i = jnp.tanh(z[:, RG_BS:] + bx_ref[:, ns])
        a = jnp.exp(czh[:, ns] + czh[:, ns] * t_r)
        a_ref[n] = a
        y = 1.0 - a * a
        xch = 0.5 * xc
        u_ref[n] = jnp.where(y > 0.0, y * lax.rsqrt(y), 0.0) * (xch + xch * t_i)

    def vreg(i):
        return pl.ds(pl.multiple_of(i * SUBLANES, SUBLANES), SUBLANES)

    def local(it, carry):
        i = (seg - 1 - it) if rev else it
        out = []
        for n in range(nt):
            a = a_ref[n, vreg(i), :]
            out += [a * carry[2 * n] + u_ref[n, vreg(i), :], a * carry[2 * n + 1]]
        return tuple(out)

    init = (jnp.zeros((SUBLANES, LANES), F32), jnp.ones((SUBLANES, LANES), F32)) * nt
    ends = lax.fori_loop(0, seg, local, init, unroll=4)
    cmats = []
    for n in range(nt):
        ns = slice(n * LANES, (n + 1) * LANES)
        hl_end, p_end = ends[2 * n], ends[2 * n + 1]
        c = carry_ref[:, ns]
        cin = [None] * SUBLANES
        for sg in (reversed(range(SUBLANES)) if rev else range(SUBLANES)):
            cin[sg] = c
            c = hl_end[sg:sg + 1, :] + p_end[sg:sg + 1, :] * c
        carry_ref[:, ns] = c
        cmats.append(jnp.concatenate(cin, axis=0))

    def final(it, carry):
        i = (seg - 1 - it) if rev else it
        out = []
        for n in range(nt):
            h = a_ref[n, vreg(i), :] * carry[n] + u_ref[n, vreg(i), :]
            hp_pad_ref[n, pl.ds(i, SUBLANES, stride=RG_HPITCH), :] = h
            out.append(h)
        return tuple(out)

    lax.fori_loop(0, seg, final, tuple(cmats), unroll=4)
    for n in range(nt):
        for sg in range(SUBLANES):
            h_ref[sg * seg:(sg + 1) * seg, n * LANES:(n + 1) * LANES] = (
                hp_pad_ref[n, sg * RG_HPITCH:sg * RG_HPITCH + seg, :].astype(h_ref.dtype))


def _rg_scan(xr, cw, cb, w, ba, bx, lam, z, n_lat):
    t, dr = xr.shape
    n_tot = t // TM
    per = TM // SUBLANES
    n8 = t // SUBLANES
    rev = z == 1
    if rev:
        blk = lambda s: n_tot - 1 - s
    else:
        blk = lambda s: (s + n_lat) % n_tot
    return pl.pallas_call(
        functools.partial(_rg_scan_kernel, rev=rev, n_lat=n_lat, n_tot=n_tot),
        grid=(n_tot,),
        in_specs=[pl.BlockSpec((TM, dr), lambda s: (blk(s), 0)),
                  pl.BlockSpec((SUBLANES, dr), lambda s: (jnp.maximum(blk(s) * per - 1, 0), 0)),
                  pl.BlockSpec((SUBLANES, dr), lambda s: (jnp.minimum((blk(s) + 1) * per, n8 - 1), 0)),
                  _full_spec(cw.shape), _full_spec(cb.shape), _pick_spec(w.shape, (z,)),
                  _pick_spec(ba.shape, (z,)), _pick_spec(bx.shape, (z,)), _pick_spec(lam.shape, (z,))],
        out_specs=pl.BlockSpec((TM, dr), lambda s: (blk(s), 0)),
        out_shape=jax.ShapeDtypeStruct((t, dr), BF16),
        scratch_shapes=[pltpu.VMEM((TM + 2 * SUBLANES, dr), F32),
                        pltpu.VMEM((dr // LANES, TM, LANES), F32), pltpu.VMEM((dr // LANES, TM, LANES), F32),
                        pltpu.VMEM((dr // LANES, SUBLANES * RG_XPITCH, LANES), F32),
                        pltpu.VMEM((dr // LANES, SUBLANES * RG_HPITCH, LANES), F32),
                        pltpu.VMEM((1, dr), F32)],
        compiler_params=_params(),
        name="rg_scan_bwd" if rev else "rg_scan_fwd",
    )(xr, xr, xr, cw, cb, w, ba, bx, lam)


def _rg_out(hf_ref, hb_ref, gate_ref):
    return ((hf_ref[...].astype(F32) + hb_ref[...].astype(F32)) * gate_ref[...].astype(F32)).astype(BF16)


def _rg_layer(xs, mods, norm_g, w1, w2, layer, w_in, conv_w, conv_b, w_a, b_a, w_x, b_x, lam, w_out, n_lat, n_tot,
              n_out_tiles, final_g):
    w_in = w_in.astype(BF16)
    x1, gate, xr = _pre_call("pre_rg", _rg_proj, xs, mods, norm_g, w1, w2, layer, n_lat, n_tot, [w_in],
                             [_full_spec(w_in.shape)], [D_RNN, D_RNN], [BF16, F32])
    w = (0.5 * jnp.concatenate([w_a, w_x], axis=-1)).astype(BF16)
    hs = [_rg_scan(xr, conv_w, conv_b.reshape(1, D_RNN), w, (0.5 * b_a).reshape(2, 1, D_RNN),
                   (0.5 * b_x).reshape(2, 1, D_RNN), lam.reshape(2, 1, D_RNN), z, n_lat) for z in range(2)]
    w_out = w_out.astype(BF16)
    return _post_call("post_rg", _rg_out, x1, mods, norm_g, w1, w2, layer, n_lat, n_out_tiles,
                      [hs[0], hs[1], gate, w_out], [_tile_spec(D_RNN)] * 3 + [_full_spec(w_out.shape)], final_g)


def _swa_proj(h, w_ref, cos_ref, sin_ref, q_ref, kz_ref, vz_ref):
    y = _dot(h, w_ref[...])
    nq = SWA_HEADS * HEAD_DIM
    nk = SWA_KV * HEAD_DIM
    cos = cos_ref[...]
    sin = sin_ref[...]
    rot = ROT_AXIS // 2
    lane = lax.broadcasted_iota(jnp.int32, (1, LANES), 1)
    first = (lane % ROT_AXIS) < rot

    def rope(xg):
        partner = jnp.where(first, pltpu.roll(xg, LANES - rot, axis=1), pltpu.roll(xg, rot, axis=1))
        return xg * cos + partner * sin

    q = y[:, :nq] * (HEAD_DIM ** -0.5 * LOG2E)
    for gi in range(nq // LANES):
        q_ref[:, gi * LANES:(gi + 1) * LANES] = rope(q[:, gi * LANES:(gi + 1) * LANES]).astype(BF16)
    low = lane < HEAD_DIM
    for gi in range(nk // LANES):
        kg = rope(y[:, nq + gi * LANES:nq + (gi + 1) * LANES])
        vg = y[:, nq + nk + gi * LANES:nq + nk + (gi + 1) * LANES]
        for par_j in range(2):
            j = 2 * gi + par_j
            for par_q in range(2):
                src_k = kg if par_q == par_j else pltpu.roll(kg, HEAD_DIM, axis=1)
                src_v = vg if par_q == par_j else pltpu.roll(vg, HEAD_DIM, axis=1)
                keep = low if par_q == 0 else jnp.logical_not(low)
                cs = slice((2 * j + par_q) * LANES, (2 * j + par_q + 1) * LANES)
                kz_ref[:, cs] = jnp.where(keep, src_k, 0.0).astype(BF16)
                vz_ref[:, cs] = jnp.where(keep, src_v, 1.0).astype(BF16)


def _swa_kernel(sink_ref, q_ref, kp_ref, k0_ref, kn_ref, vp_ref, v0_ref, vn_ref, kc_ref, vc_ref, o_ref,
                kz_ref, vz_ref, *, nb):
    qb = q_ref.shape[0]
    c_len = kc_ref.shape[0]
    i = pl.program_id(0)

    @pl.when(i == 0)
    def _():
        kz_ref[0:c_len, :] = kc_ref[...]
        vz_ref[0:c_len, :] = vc_ref[...]

    row = lax.broadcasted_iota(jnp.int32, (2 * qb, 1), 0)
    low = lax.broadcasted_iota(jnp.int32, (1, LANES), 1) < HEAD_DIM

    def attend(n_keys, mask):
        for j in range(SWA_KV):
            qs = jnp.concatenate([q_ref[:, (2 * j) * LANES:(2 * j + 1) * LANES],
                                  q_ref[:, (2 * j + 1) * LANES:(2 * j + 2) * LANES]], axis=0)
            lo = slice((2 * j) * LANES, (2 * j + 1) * LANES)
            hi = slice((2 * j + 1) * LANES, (2 * j + 2) * LANES)
            s_all = _dot_nt(qs, jnp.concatenate([kz_ref[0:n_keys, lo], kz_ref[0:n_keys, hi]], axis=0))
            ps, sinks, ms = [], [], []
            for par in range(2):
                s = mask(s_all[:, par * n_keys:(par + 1) * n_keys])
                sink = LOG2E * jnp.where(row < qb, sink_ref[j * SWA_G + par], sink_ref[j * SWA_G + 2 + par])
                m = jnp.maximum(jnp.max(s, axis=-1, keepdims=True), sink)
                ps.append(jnp.exp2(s - m).astype(BF16))
                sinks.append(sink)
                ms.append(m)
            r_all = _dot(jnp.concatenate(ps, axis=0), vz_ref[0:n_keys, (2 * j) * LANES:(2 * j + 2) * LANES])
            halves = []
            for par in range(2):
                r = r_all[par * 2 * qb:(par + 1) * 2 * qb, par * LANES:(par + 1) * LANES]
                halves.append(r / (pltpu.roll(r, HEAD_DIM, axis=1) + jnp.exp2(sinks[par] - ms[par])))
            o = jnp.where(low, halves[0], halves[1]).astype(BF16)
            o_ref[:, (2 * j) * LANES:(2 * j + 1) * LANES] = o[0:qb]
            o_ref[:, (2 * j + 1) * LANES:(2 * j + 2) * LANES] = o[qb:]

    @pl.when(i < nb)
    def _():
        for b, (k_ref, v_ref) in enumerate(((kp_ref, vp_ref), (k0_ref, v0_ref), (kn_ref, vn_ref))):
            kz_ref[c_len + b * qb:c_len + (b + 1) * qb, :] = k_ref[...]
            vz_ref[c_len + b * qb:c_len + (b + 1) * qb, :] = v_ref[...]
        r_i = lax.broadcasted_iota(jnp.int32, (2 * qb, qb), 0) % qb
        c_i = lax.broadcasted_iota(jnp.int32, (2 * qb, qb), 1)
        b_prev = jnp.where(jnp.logical_and(c_i >= r_i, i > 0), 0.0, NEG)
        b_next = jnp.where(jnp.logical_and(c_i <= r_i, i < nb - 1), 0.0, NEG)

        def mask(s):
            return jnp.concatenate([s[:, :c_len], s[:, c_len:c_len + qb] + b_prev,
                                    s[:, c_len + qb:c_len + 2 * qb], s[:, c_len + 2 * qb:] + b_next], axis=1)

        attend(c_len + 3 * qb, mask)

    @pl.when(i >= nb)
    def _():
        attend(c_len, lambda s: s)


def _swa_attn(q, kz, vz, sink, s_len):
    t, nq = q.shape
    nz = kz.shape[1]
    c_len = t - s_len
    nb = s_len // Q_BLOCK
    n_blk = t // Q_BLOCK
    smem = pl.BlockSpec(memory_space=pltpu.SMEM)
    ctx_spec = pl.BlockSpec((c_len, nz), lambda i: (s_len // c_len, 0))
    band = [pl.BlockSpec((Q_BLOCK, nz), lambda i: (jnp.maximum(i - 1, 0), 0)),
            pl.BlockSpec((Q_BLOCK, nz), lambda i: (i, 0)),
            pl.BlockSpec((Q_BLOCK, nz), lambda i: (jnp.minimum(i + 1, n_blk - 1), 0))]
    n_keys = c_len + 3 * Q_BLOCK
    return pl.pallas_call(
        functools.partial(_swa_kernel, nb=nb),
        grid=(n_blk,),
        in_specs=[smem, pl.BlockSpec((Q_BLOCK, nq), lambda i: (i, 0))] + band + band + [ctx_spec, ctx_spec],
        out_specs=pl.BlockSpec((Q_BLOCK, nq), lambda i: (i, 0)),
        out_shape=jax.ShapeDtypeStruct((t, nq), BF16),
        scratch_shapes=[pltpu.VMEM((n_keys, nz), BF16), pltpu.VMEM((n_keys, nz), BF16)],
        compiler_params=_params(),
        name="swa_attn",
    )(sink, q, kz, kz, kz, vz, vz, vz, kz, vz)


def _rope_tables(s_len, c_len):
    rows = s_len // GRID_W
    freqs = ROPE_BASE ** (-jnp.arange(0, ROT_AXIS, 2, dtype=F32) / ROT_AXIS)
    dim = jnp.arange(LANES) % HEAD_DIM
    by_row = (dim // ROT_AXIS) == 0
    lane_freq = freqs[dim % (ROT_AXIS // 2)]
    sign = jnp.where((dim % ROT_AXIS) < ROT_AXIS // 2, -1.0, 1.0)
    ang_r = jnp.arange(rows, dtype=F32)[:, None] * lane_freq
    ang_c = jnp.arange(GRID_W, dtype=F32)[:, None] * lane_freq

    def table(fn, scale, ctx_fill):
        lat = jnp.where(by_row, (fn(ang_r) * scale)[:, None, :], (fn(ang_c) * scale)[None, :, :])
        return jnp.concatenate([lat.reshape(s_len, LANES), jnp.full((c_len, LANES), ctx_fill, F32)], axis=0)

    return table(jnp.cos, 1.0, 1.0), table(jnp.sin, sign, 0.0)


def _swa_out(o_ref):
    return o_ref[...]


def _swa_layer(xs, mods, norm_g, w1, w2, layer, w_qkv, sink, w_out, rope, n_lat, n_tot, n_out_tiles, s_len, final_g):
    w_qkv = w_qkv.astype(BF16)
    nq = SWA_HEADS * HEAD_DIM
    nz = 2 * SWA_KV * LANES
    x1, q, kz, vz = _pre_call("pre_swa", _swa_proj, xs, mods, norm_g, w1, w2, layer, n_lat, n_tot,
                              [w_qkv, rope[0], rope[1]], [_full_spec(w_qkv.shape), _tile_spec(LANES), _tile_spec(LANES)],
                              [nq, nz, nz], [BF16, BF16, BF16])
    o = _swa_attn(q, kz, vz, sink, s_len)
    w_out = w_out.astype(BF16)
    return _post_call("post_swa", _swa_out, x1, mods, norm_g, w1, w2, layer, n_lat, n_out_tiles, [o, w_out],
                      [_tile_spec(nq), _full_spec(w_out.shape)], final_g)


def kernel(x, c, ctx, c_ctx, ada_w, ada_b, norm_g, final_g, ffn_w_in, ffn_w_out, gla_w_in, gla_w_a2, gla_b_a, gla_norm_g, gla_w_out, rg_w_in, rg_conv_w, rg_conv_b, rg_w_a, rg_b_a, rg_w_x, rg_b_x, rg_lam, rg_w_out, swa_w_qkv, swa_sink, swa_w_out):
    b, s_len, d = x.shape
    c_len = ctx.shape[1]
    assert b == 1 and s_len % TM == 0 and c_len % TM == 0 and s_len % c_len == 0
    depth = ada_w.shape[0]
    n_lat = s_len // TM
    n_tot = n_lat + c_len // TM
    mods = _ada_vectors(c, c_ctx, ada_w, ada_b)
    ng = norm_g.reshape(depth * 3, 1, d)
    w1 = ffn_w_in.astype(BF16)
    w2 = ffn_w_out.astype(BF16)
    gw = _gla_weights(gla_w_in, gla_w_a2, gla_b_a, gla_norm_g, gla_w_out)
    rope = _rope_tables(s_len, c_len)
    xs = (x[0], ctx[0])
    for i in range(depth):
        m, j = i % N_MIXERS, i // N_MIXERS
        last = i == depth - 1
        n_after = n_lat if last else n_tot
        fin = final_g if last else None
        if m == 0:
            xs = _gla_layer(xs, mods, ng, w1, w2, i, gw, j, n_lat, n_tot, n_after, fin)
        elif m == 1:
            xs = _rg_layer(xs, mods, ng, w1, w2, i, rg_w_in[j], rg_conv_w[j], rg_conv_b[j], rg_w_a[j], rg_b_a[j],
                           rg_w_x[j], rg_b_x[j], rg_lam[j], rg_w_out[j], n_lat, n_tot, n_after, fin)
        else:
            xs = _swa_layer(xs, mods, ng, w1, w2, i, swa_w_qkv[j], swa_sink[j], swa_w_out[j], rope, n_lat, n_tot,
                            n_after, s_len, fin)
    return xs[None]
```

```python
import functools
import math

import jax
import jax.numpy as jnp
from jax import lax
from jax.experimental import pallas as pl
from jax.experimental.pallas import tpu as pltpu

F32 = jnp.float32
BF16 = jnp.bfloat16

N_MOD = 9
NORM_EPS = 1e-6
GLA_HEADS = 4
GLA_DK = 128
GLA_DV = 256
GLA_RANK = 16
GLA_LOGIT_NORM = 16.0
GLA_CHUNK = 64
GLA_HK = GLA_HEADS * GLA_DK
GLA_HV = GLA_HEADS * GLA_DV
GLA_LR_PAD = 128
D_RNN = 1536
RG_BLOCKS = 12
RG_BS = D_RNN // RG_BLOCKS
RG_C = 8.0
SWA_HEADS = 16
SWA_KV = 4
SWA_G = SWA_HEADS // SWA_KV
HEAD_DIM = 64
WINDOW = 128
Q_BLOCK = 128
ROT_AXIS = HEAD_DIM // 2
ROPE_BASE = 10000.0
GRID_W = 64
N_MIXERS = 3

TM = 256
FFN_WARM = 16
SUBLANES = 8
LANES = 128
VMEM_LIMIT = 56 * 1024 * 1024
NEG = -1e30
LOG2E = 1.0 / math.log(2.0)


def _odd_pitch(rows):
    tiles = -(-rows // SUBLANES)
    return (tiles + (tiles % 2 == 0)) * SUBLANES


RG_SEG = TM // SUBLANES
RG_HPITCH = _odd_pitch(RG_SEG)
RG_XPITCH = _odd_pitch(RG_SEG + 2 * SUBLANES)


def _params(n_axes=1):
    return pltpu.CompilerParams(dimension_semantics=("arbitrary",) * n_axes, vmem_limit_bytes=VMEM_LIMIT)


def _dot(a, b):
    return jnp.dot(a, b, preferred_element_type=F32)


def _dot_nt(a, b):
    return lax.dot_general(a, b, (((1,), (1,)), ((), ())), preferred_element_type=F32)


def _dot_tn(a, b):
    return lax.dot_general(a, b, (((0,), (0,)), ((), ())), preferred_element_type=F32)


def _rms(x, g):
    return (x * lax.rsqrt(jnp.mean(x * x, axis=-1, keepdims=True) + NORM_EPS)) * g


def _rms_mod(x, g, shift, scale):
    return _rms(x, g) * (1.0 + scale) + shift


def _sigmoid(x):
    return 1.0 / (1.0 + jnp.exp(-x))


def _silu(x):
    xh = 0.5 * x
    return xh + xh * jnp.tanh(xh)


def _softplus(x):
    return jnp.maximum(x, 0.0) + jnp.log1p(jnp.exp(-jnp.abs(x)))


def _gelu_tanh(x):
    return 0.5 * x * (1.0 + jnp.tanh(math.sqrt(2.0 / math.pi) * (x + 0.044715 * (x * x * x))))


def _row(ref, r):
    return ref[r:r + 1, :]


def _tile_spec(width, shift=0):
    return pl.BlockSpec((TM, width), lambda i: (jnp.maximum(i - shift, 0), 0))


def _full_spec(shape):
    nd = len(shape)
    return pl.BlockSpec(shape, lambda i: (0,) * nd, pipeline_mode=pl.Buffered(1))


def _pick_spec(shape, lead):
    tail = shape[len(lead):]
    return pl.BlockSpec((None,) * len(lead) + tuple(tail), lambda i: tuple(lead) + (0,) * len(tail),
                        pipeline_mode=pl.Buffered(1))


def _mod_spec(mods, layer, n_lat, shift=0):
    _, _, n_mod, d = mods.shape
    return pl.BlockSpec((None, None, n_mod, d),
                        lambda i: (layer, jnp.minimum(jnp.maximum(i - shift, 0) // n_lat, 1), 0, 0))


def _ada_kernel(cc_ref, w_ref, b_ref, o_ref):
    act = _silu(cc_ref[...]).astype(BF16)
    o_ref[...] = _dot(act, w_ref[...].astype(BF16)) + b_ref[...]


def _ada_vectors(c, c_ctx, ada_w, ada_b):
    depth, d, nd = ada_w.shape
    cc = jnp.zeros((SUBLANES, d), F32).at[0].set(c[0]).at[1].set(c_ctx)
    out = pl.pallas_call(
        _ada_kernel,
        grid=(depth, nd // d),
        in_specs=[pl.BlockSpec((SUBLANES, d), lambda l, j: (0, 0)),
                  pl.BlockSpec((None, d, d), lambda l, j: (l, 0, j)),
                  pl.BlockSpec((None, 1, d), lambda l, j: (l, 0, j))],
        out_specs=pl.BlockSpec((None, SUBLANES, d), lambda l, j: (l, 0, j)),
        out_shape=jax.ShapeDtypeStruct((depth, SUBLANES, nd), F32),
        compiler_params=_params(2),
        name="ada_vectors",
    )(cc, ada_w, ada_b.reshape(depth, 1, nd))
    return out.reshape(depth, SUBLANES, N_MOD, d)


def _swiglu_step(x, mod_ref, r0, g_ref, w1_ref, w2_ref):
    h = _rms_mod(x, g_ref[...], _row(mod_ref, r0), _row(mod_ref, r0 + 1)).astype(BF16)
    gu = _dot(h, w1_ref[...])
    f = w2_ref.shape[0]
    a = (_silu(gu[:, :f]) * gu[:, f:]).astype(BF16)
    return x + (0.5 * _row(mod_ref, r0 + 2)) * _dot(a, w2_ref[...])


def _stage_weights(w1c_ref, w2c_ref, w1_ref, w2_ref):
    i = pl.program_id(0)

    @pl.when(i < FFN_WARM)
    def _():
        for c_ref, s_ref in ((w1c_ref, w1_ref), (w2c_ref, w2_ref)):
            r = c_ref.shape[0]
            s_ref[pl.ds(pl.multiple_of(i * r, r), r), :] = c_ref[...].astype(BF16)


def _ffn_weight_specs(w1, w2, layer, half):
    specs, scratch = [], []
    for w in (w1, w2):
        rows, cols = w.shape[-2:]
        assert rows % (FFN_WARM * 2 * SUBLANES) == 0
        specs.append(pl.BlockSpec((None, None, rows // FFN_WARM, cols),
                                  lambda i: (layer, half, jnp.minimum(i, FFN_WARM - 1), 0)))
        scratch.append(pltpu.VMEM((rows, cols), BF16))
    return specs, scratch


def _pre_kernel(*refs, proj, n_par, n_lat, split_in):
    refs = list(refs)
    w1_ref, w2_ref = refs[-2:]
    refs = refs[:-2]
    x_refs, refs = (refs[:2], refs[2:]) if split_in else (refs[:1], refs[1:])
    mod_ref, g1_ref, w1c_ref, w2c_ref, gm_ref = refs[:5]
    par, x1_ref, outs = refs[5:5 + n_par], refs[5 + n_par], refs[6 + n_par:]
    _stage_weights(w1c_ref, w2c_ref, w1_ref, w2_ref)

    @pl.when(pl.program_id(0) >= FFN_WARM)
    def _():
        if split_in:
            x = jnp.where(pl.program_id(0) - FFN_WARM < n_lat, x_refs[0][...], x_refs[1][...])
        else:
            x = x_refs[0][...]
        x1 = _swiglu_step(x, mod_ref, 0, g1_ref, w1_ref, w2_ref)
        x1_ref[...] = x1
        h = _rms_mod(x1, gm_ref[...], _row(mod_ref, 3), _row(mod_ref, 4)).astype(BF16)
        proj(h, *par, *outs)


def _pre_call(name, proj, xs, mods, norm_g, w1, w2, layer, n_lat, n_tiles, par, par_specs, out_widths, out_dtypes):
    split_in = isinstance(xs, tuple)
    d = (xs[0] if split_in else xs).shape[1]
    tile = lambda i: jnp.maximum(i - FFN_WARM, 0)
    if split_in:
        x_specs = [pl.BlockSpec((TM, d), lambda i: (jnp.minimum(tile(i), n_lat - 1), 0)),
                   pl.BlockSpec((TM, d), lambda i: (jnp.maximum(tile(i) - n_lat, 0), 0))]
        x_args = list(xs)
    else:
        x_specs, x_args = [_tile_spec(d, FFN_WARM)], [xs]
    t = n_tiles * TM
    w_specs, w_scratch = _ffn_weight_specs(w1, w2, layer, 0)
    return pl.pallas_call(
        functools.partial(_pre_kernel, proj=proj, n_par=len(par), n_lat=n_lat, split_in=split_in),
        grid=(FFN_WARM + n_tiles,),
        in_specs=x_specs + [_mod_spec(mods, layer, n_lat, FFN_WARM), _pick_spec(norm_g.shape, (3 * layer,))] + w_specs
        + [_pick_spec(norm_g.shape, (3 * layer + 1,))] + par_specs,
        out_specs=[_tile_spec(d, FFN_WARM)] + [_tile_spec(w, FFN_WARM) for w in out_widths],
        out_shape=[jax.ShapeDtypeStruct((t, d), F32)]
        + [jax.ShapeDtypeStruct((t, w), dt) for w, dt in zip(out_widths, out_dtypes)],
        scratch_shapes=w_scratch,
        compiler_params=_params(),
        name=name,
    )(*x_args, mods, norm_g, w1, w2, norm_g, *par)


def _post_kernel(*refs, head, n_head, final):
    x_ref, mod_ref = refs[:2]
    head_refs = refs[2:2 + n_head]
    g2_ref, w1c_ref, w2c_ref = refs[2 + n_head:5 + n_head]
    o_ref, w1_ref, w2_ref = refs[-3:]
    _stage_weights(w1c_ref, w2c_ref, w1_ref, w2_ref)

    @pl.when(pl.program_id(0) >= FFN_WARM)
    def _():
        y = head(*head_refs[:-1])
        x2 = x_ref[...] + _row(mod_ref, 5) * _dot(y, head_refs[-1][...])
        x3 = _swiglu_step(x2, mod_ref, 6, g2_ref, w1_ref, w2_ref)
        o_ref[...] = _rms(x3, refs[5 + n_head][...]) if final else x3


def _post_call(name, head, xs, mods, norm_g, w1, w2, layer, n_lat, n_tiles, head_args, head_specs, final_g):
    d = xs.shape[1]
    fin_specs, fin_args = ([_full_spec((1, d))], [final_g.reshape(1, d)]) if final_g is not None else ([], [])
    w_specs, w_scratch = _ffn_weight_specs(w1, w2, layer, 1)
    return pl.pallas_call(
        functools.partial(_post_kernel, head=head, n_head=len(head_args), final=final_g is not None),
        grid=(FFN_WARM + n_tiles,),
        in_specs=[_tile_spec(d, FFN_WARM), _mod_spec(mods, layer, n_lat, FFN_WARM)] + head_specs
        + [_pick_spec(norm_g.shape, (3 * layer + 2,))] + w_specs + fin_specs,
        out_specs=_tile_spec(d, FFN_WARM),
        out_shape=jax.ShapeDtypeStruct((n_tiles * TM, d), F32),
        scratch_shapes=w_scratch,
        compiler_params=_params(),
        name=name,
    )(xs, mods, *head_args, norm_g, w1, w2, *fin_args)


def _gla_proj(h, w_ref, w2_ref, ba_ref, q_ref, k_ref, v_ref, og_ref, gg_ref):
    y = _dot(h, w_ref[...])
    q_ref[...] = (y[:, :GLA_HK] * (GLA_DK ** -0.5)).astype(BF16)
    k_ref[...] = y[:, GLA_HK:2 * GLA_HK].astype(BF16)
    v_ref[...] = y[:, 2 * GLA_HK:2 * GLA_HK + GLA_HV].astype(BF16)
    og_ref[...] = y[:, 2 * GLA_HK + GLA_HV:2 * GLA_HK + 2 * GLA_HV].astype(BF16)
    lr = y[:, 2 * GLA_HK + 2 * GLA_HV:].astype(BF16)
    logit = _dot(lr, w2_ref[...]) + ba_ref[...]
    gg_ref[...] = (-_softplus(-logit)) * (LOG2E / GLA_LOGIT_NORM)


def _gla_tile(q_ref, k_ref, v_ref, g_ref, o_ref, s_ref, rev):
    c = GLA_CHUNK
    tm = q_ref.shape[0]
    nc = tm // c
    r_i = lax.broadcasted_iota(jnp.int32, (tm, tm), 0)
    c_i = lax.broadcasted_iota(jnp.int32, (tm, tm), 1)
    same = (r_i // c) == (c_i // c)
    tri = jnp.logical_and(same, (c_i >= r_i) if rev else (c_i <= r_i))
    tri_bf = tri.astype(BF16)
    g = g_ref[...]
    g_hi = g.astype(BF16)
    g_lo = (g - g_hi.astype(F32)).astype(BF16)
    b = _dot(tri_bf, g_hi) + _dot(tri_bf, g_lo)
    ends = [b[n * c:n * c + 1, :] if rev else b[(n + 1) * c - 1:(n + 1) * c, :] for n in range(nc)]
    btot = jnp.concatenate([jnp.broadcast_to(e, (c, e.shape[1])) for e in ends], axis=0)
    k = k_ref[...].astype(F32)
    qe = (q_ref[...].astype(F32) * jnp.exp2(b)).astype(BF16)
    ke = (k * jnp.exp2(-b)).astype(BF16)
    kd = (k * jnp.exp2(btot - b)).astype(BF16)
    decs = [jnp.exp2(e) for e in ends]
    own = (lax.broadcasted_iota(jnp.int32, (tm, nc * GLA_DK), 0) // c
           == lax.broadcasted_iota(jnp.int32, (tm, nc * GLA_DK), 1) // GLA_DK)
    for h in range(GLA_HEADS):
        ks = slice(h * GLA_DK, (h + 1) * GLA_DK)
        vs = slice(h * GLA_DV, (h + 1) * GLA_DV)
        vh = v_ref[:, vs]
        att = jnp.where(tri, _dot_nt(qe[:, ks], ke[:, ks]), 0.0).astype(BF16)
        o_in = _dot(att, vh)
        kd_bd = jnp.where(own, jnp.concatenate([kd[:, ks]] * nc, axis=1), jnp.zeros((), BF16))
        upd = _dot_tn(vh, kd_bd)
        st = s_ref[h]
        for n in (reversed(range(nc)) if rev else range(nc)):
            rows = slice(n * c, (n + 1) * c)
            o_ref[rows, vs] = (o_in[rows, :] + _dot_nt(qe[rows, ks], st.astype(BF16))).astype(o_ref.dtype)
            st = decs[n][:, ks] * st + upd[:, n * GLA_DK:(n + 1) * GLA_DK]
        s_ref[h] = st


def _gla_kernel(qf_ref, kf_ref, vf_ref, gf_ref, qb_ref, kb_ref, vb_ref, gb_ref, of_ref, ob_ref, sf_ref, sb_ref):
    @pl.when(pl.program_id(0) == 0)
    def _():
        sf_ref[...] = jnp.zeros_like(sf_ref)
        sb_ref[...] = jnp.zeros_like(sb_ref)

    _gla_tile(qf_ref, kf_ref, vf_ref, gf_ref, of_ref, sf_ref, False)
    _gla_tile(qb_ref, kb_ref, vb_ref, gb_ref, ob_ref, sb_ref, True)


def _gla_scan(q, k, v, gg, n_lat):
    t = q.shape[0]
    n_tot = t // TM
    fwd = lambda s: (s + n_lat) % n_tot
    bwd = lambda s: n_tot - 1 - s

    def specs(blk, col):
        return [pl.BlockSpec((TM, GLA_HK), lambda s: (blk(s), 0)), pl.BlockSpec((TM, GLA_HK), lambda s: (blk(s), 0)),
                pl.BlockSpec((TM, GLA_HV), lambda s: (blk(s), 0)), pl.BlockSpec((TM, GLA_HK), lambda s: (blk(s), col))]

    return pl.pallas_call(
        _gla_kernel,
        grid=(n_tot,),
        in_specs=specs(fwd, 0) + specs(bwd, 1),
        out_specs=[pl.BlockSpec((TM, GLA_HV), lambda s: (fwd(s), 0)), pl.BlockSpec((TM, GLA_HV), lambda s: (bwd(s), 0))],
        out_shape=[jax.ShapeDtypeStruct((t, GLA_HV), BF16)] * 2,
        scratch_shapes=[pltpu.VMEM((GLA_HEADS, GLA_DV, GLA_DK), F32)] * 2,
        compiler_params=_params(),
        name="gla_scan",
    )(q, k, v, gg, q, k, v, gg)


def _gla_out(of_ref, ob_ref, og_ref, ng_ref):
    o = of_ref[...].astype(F32) + ob_ref[...].astype(F32)
    ng = ng_ref[...]
    parts = [_rms(o[:, h * GLA_DV:(h + 1) * GLA_DV], ng) for h in range(GLA_HEADS)]
    return (jnp.concatenate(parts, axis=1) * _silu(og_ref[...].astype(F32))).astype(BF16)


def _gla_weights(w_in, w_a2, b_a, gla_norm_g, w_out):
    na, d, n_in = w_in.shape
    n_main = 2 * GLA_HK + 2 * GLA_HV
    w = jnp.pad(w_in.astype(BF16), ((0, 0), (0, 0), (0, n_main + GLA_LR_PAD - n_in)))
    w2 = jnp.zeros((na, GLA_LR_PAD, 2 * GLA_HK), F32)
    w2 = w2.at[:, :GLA_RANK, :GLA_HK].set(w_a2[:, 0]).at[:, GLA_RANK:2 * GLA_RANK, GLA_HK:].set(w_a2[:, 1])
    return (w, w2.astype(BF16), b_a.reshape(na, 1, 2 * GLA_HK), gla_norm_g.reshape(na, 1, GLA_DV), w_out.astype(BF16))


def _gla_layer(xs, mods, norm_g, w1, w2, layer, gw, j, n_lat, n_tot, n_out_tiles, final_g):
    w, wlr, ba, ng, w_out = gw
    x1, q, k, v, og, gg = _pre_call(
        "pre_gla", _gla_proj, xs, mods, norm_g, w1, w2, layer, n_lat, n_tot, [w, wlr, ba],
        [_pick_spec(w.shape, (j,)), _pick_spec(wlr.shape, (j,)), _pick_spec(ba.shape, (j,))],
        [GLA_HK, GLA_HK, GLA_HV, GLA_HV, 2 * GLA_HK], [BF16, BF16, BF16, BF16, F32])
    o_f, o_b = _gla_scan(q, k, v, gg, n_lat)
    return _post_call(
        "post_gla", _gla_out, x1, mods, norm_g, w1, w2, layer, n_lat, n_out_tiles, [o_f, o_b, og, ng, w_out],
        [_tile_spec(GLA_HV, FFN_WARM)] * 3 + [_pick_spec(ng.shape, (j,)), _pick_spec(w_out.shape, (j,))], final_g)


def _rg_proj(h, w_ref, gate_ref, xr_ref):
    y = _dot(h, w_ref[...])
    gate_ref[...] = _gelu_tanh(y[:, :D_RNN]).astype(BF16)
    xr_ref[...] = y[:, D_RNN:]


def _rg_scan_kernel(xm_ref, xp_ref, xn_ref, cw_ref, cb_ref, w_ref, ba_ref, bx_ref, lam_ref, h_ref,
                    xs_ref, a_ref, u_ref, xp_pad_ref, hp_pad_ref, carry_ref, *, rev, n_lat, n_tot):
    s = pl.program_id(0)
    blk = (n_tot - 1 - s) if rev else (s + n_lat) % n_tot

    @pl.when(s == 0)
    def _():
        carry_ref[...] = jnp.zeros_like(carry_ref)

    tm = xm_ref.shape[0]
    hal = SUBLANES
    seg = tm // SUBLANES
    nt = a_ref.shape[0]
    assert seg == RG_SEG and RG_BS == LANES
    has_prev = jnp.logical_and(blk != 0, blk != n_lat)
    has_next = jnp.logical_and(blk != n_lat - 1, blk != n_tot - 1)
    xs_ref[0:hal, :] = jnp.where(has_prev, xp_ref[...], 0.0)
    xs_ref[hal:hal + tm, :] = xm_ref[...]
    xs_ref[hal + tm:, :] = jnp.where(has_next, xn_ref[...], 0.0)
    czh = (-0.5 * RG_C) * _softplus(-lam_ref[...])
    ext = seg + 2 * hal
    for n in range(nt):
        ns = slice(n * LANES, (n + 1) * LANES)
        for sg in range(SUBLANES):
            xp_pad_ref[n, sg * RG_XPITCH:sg * RG_XPITCH + ext, :] = xs_ref[sg * seg:sg * seg + ext, ns]
        xe = [xp_pad_ref[n, pl.ds(k, SUBLANES, stride=RG_XPITCH), :] for k in range(hal - 2, hal + seg + 1)]
        cw = [jnp.broadcast_to(cw_ref[j:j + 1, ns], (SUBLANES, LANES)) for j in range(4)]
        cb = jnp.broadcast_to(cb_ref[:, ns], (SUBLANES, LANES))
        xc = jnp.concatenate(
            [cb + cw[0] * xe[i] + cw[1] * xe[i + 1] + cw[2] * xe[i + 2] + cw[3] * xe[i + 3] for i in range(seg)],
            axis=0)
        z = _dot(xc.astype(BF16), w_ref[n])
        t_r = jnp.tanh(z[:, :RG_BS] + ba_ref[:, ns])
        t_i = jnp.tanh(z[:, RG_BS:] + bx_ref[:, ns])
        a = jnp.exp(czh[:, ns] + czh[:, ns] * t_r)
        a_ref[n] = a
        y = 1.0 - a * a
        xch = 0.5 * xc
        u_ref[n] = jnp.where(y > 0.0, y * lax.rsqrt(y), 0.0) * (xch + xch * t_i)

    def vreg(i):
        return pl.ds(pl.multiple_of(i * SUBLANES, SUBLANES), SUBLANES)

    def local(it, carry):
        i = (seg - 1 - it) if rev else it
        out = []
        for n in range(nt):
            a = a_ref[n, vreg(i), :]
            out += [a * carry[2 * n] + u_ref[n, vreg(i), :], a * carry[2 * n + 1]]
        return tuple(out)

    init = (jnp.zeros((SUBLANES, LANES), F32), jnp.ones((SUBLANES, LANES), F32)) * nt
    ends = lax.fori_loop(0, seg, local, init, unroll=4)
    cmats = []
    for n in range(nt):
        ns = slice(n * LANES, (n + 1) * LANES)
        hl_end, p_end = ends[2 * n], ends[2 * n + 1]
        c = carry_ref[:, ns]
        cin = [None] * SUBLANES
        for sg in (reversed(range(SUBLANES)) if rev else range(SUBLANES)):
            cin[sg] = c
            c = hl_end[sg:sg + 1, :] + p_end[sg:sg + 1, :] * c
        carry_ref[:, ns] = c
        cmats.append(jnp.concatenate(cin, axis=0))

    def final(it, carry):
        i = (seg - 1 - it) if rev else it
        out = []
        for n in range(nt):
            h = a_ref[n, vreg(i), :] * carry[n] + u_ref[n, vreg(i), :]
            hp_pad_ref[n, pl.ds(i, SUBLANES, stride=RG_HPITCH), :] = h
            out.append(h)
        return tuple(out)

    lax.fori_loop(0, seg, final, tuple(cmats), unroll=4)
    for n in range(nt):
        for sg in range(SUBLANES):
            h_ref[sg * seg:(sg + 1) * seg, n * LANES:(n + 1) * LANES] = (
                hp_pad_ref[n, sg * RG_HPITCH:sg * RG_HPITCH + seg, :].astype(h_ref.dtype))


def _rg_scan(xr, cw, cb, w, ba, bx, lam, z, n_lat):
    t, dr = xr.shape
    n_tot = t // TM
    per = TM // SUBLANES
    n8 = t // SUBLANES
    rev = z == 1
    if rev:
        blk = lambda s: n_tot - 1 - s
    else:
        blk = lambda s: (s + n_lat) % n_tot
    return pl.pallas_call(
        functools.partial(_rg_scan_kernel, rev=rev, n_lat=n_lat, n_tot=n_tot),
        grid=(n_tot,),
        in_specs=[pl.BlockSpec((TM, dr), lambda s: (blk(s), 0)),
                  pl.BlockSpec((SUBLANES, dr), lambda s: (jnp.maximum(blk(s) * per - 1, 0), 0)),
                  pl.BlockSpec((SUBLANES, dr), lambda s: (jnp.minimum((blk(s) + 1) * per, n8 - 1), 0)),
                  _full_spec(cw.shape), _full_spec(cb.shape), _pick_spec(w.shape, (z,)),
                  _pick_spec(ba.shape, (z,)), _pick_spec(bx.shape, (z,)), _pick_spec(lam.shape, (z,))],
        out_specs=pl.BlockSpec((TM, dr), lambda s: (blk(s), 0)),
        out_shape=jax.ShapeDtypeStruct((t, dr), BF16),
        scratch_shapes=[pltpu.VMEM((TM + 2 * SUBLANES, dr), F32),
                        pltpu.VMEM((dr // LANES, TM, LANES), F32), pltpu.VMEM((dr // LANES, TM, LANES), F32),
                        pltpu.VMEM((dr // LANES, SUBLANES * RG_XPITCH, LANES), F32),
                        pltpu.VMEM((dr // LANES, SUBLANES * RG_HPITCH, LANES), F32),
                        pltpu.VMEM((1, dr), F32)],
        compiler_params=_params(),
        name="rg_scan_bwd" if rev else "rg_scan_fwd",
    )(xr, xr, xr, cw, cb, w, ba, bx, lam)


def _rg_out(hf_ref, hb_ref, gate_ref):
    return ((hf_ref[...].astype(F32) + hb_ref[...].astype(F32)) * gate_ref[...].astype(F32)).astype(BF16)


def _rg_layer(xs, mods, norm_g, w1, w2, layer, w_in, conv_w, conv_b, w_a, b_a, w_x, b_x, lam, w_out, n_lat, n_tot,
              n_out_tiles, final_g):
    w_in = w_in.astype(BF16)
    x1, gate, xr = _pre_call("pre_rg", _rg_proj, xs, mods, norm_g, w1, w2, layer, n_lat, n_tot, [w_in],
                             [_full_spec(w_in.shape)], [D_RNN, D_RNN], [BF16, F32])
    w = (0.5 * jnp.concatenate([w_a, w_x], axis=-1)).astype(BF16)
    hs = [_rg_scan(xr, conv_w, conv_b.reshape(1, D_RNN), w, (0.5 * b_a).reshape(2, 1, D_RNN),
                   (0.5 * b_x).reshape(2, 1, D_RNN), lam.reshape(2, 1, D_RNN), z, n_lat) for z in range(2)]
    w_out = w_out.astype(BF16)
    return _post_call("post_rg", _rg_out, x1, mods, norm_g, w1, w2, layer, n_lat, n_out_tiles,
                      [hs[0], hs[1], gate, w_out], [_tile_spec(D_RNN, FFN_WARM)] * 3 + [_full_spec(w_out.shape)],
                      final_g)


def _swa_proj(h, w_ref, cos_ref, sin_ref, q_ref, kz_ref, vz_ref):
    y = _dot(h, w_ref[...])
    nq = SWA_HEADS * HEAD_DIM
    nk = SWA_KV * HEAD_DIM
    cos = cos_ref[...]
    sin = sin_ref[...]
    rot = ROT_AXIS // 2
    lane = lax.broadcasted_iota(jnp.int32, (1, LANES), 1)
    first = (lane % ROT_AXIS) < rot

    def rope(xg):
        partner = jnp.where(first, pltpu.roll(xg, LANES - rot, axis=1), pltpu.roll(xg, rot, axis=1))
        return xg * cos + partner * sin

    q = y[:, :nq] * (HEAD_DIM ** -0.5 * LOG2E)
    for gi in range(nq // LANES):
        q_ref[:, gi * LANES:(gi + 1) * LANES] = rope(q[:, gi * LANES:(gi + 1) * LANES]).astype(BF16)
    low = lane < HEAD_DIM
    for gi in range(nk // LANES):
        kg = rope(y[:, nq + gi * LANES:nq + (gi + 1) * LANES])
        vg = y[:, nq + nk + gi * LANES:nq + nk + (gi + 1) * LANES]
        for par_j in range(2):
            j = 2 * gi + par_j
            for par_q in range(2):
                src_k = kg if par_q == par_j else pltpu.roll(kg, HEAD_DIM, axis=1)
                src_v = vg if par_q == par_j else pltpu.roll(vg, HEAD_DIM, axis=1)
                keep = low if par_q == 0 else jnp.logical_not(low)
                cs = slice((2 * j + par_q) * LANES, (2 * j + par_q + 1) * LANES)
                kz_ref[:, cs] = jnp.where(keep, src_k, 0.0).astype(BF16)
                vz_ref[:, cs] = jnp.where(keep, src_v, 1.0).astype(BF16)


def _swa_kernel(sink_ref, q_ref, kp_ref, k0_ref, kn_ref, vp_ref, v0_ref, vn_ref, kc_ref, vc_ref, o_ref,
                kz_ref, vz_ref, *, nb):
    qb = q_ref.shape[0]
    c_len = kc_ref.shape[0]
    i = pl.program_id(0)

    @pl.when(i == 0)
    def _():
        kz_ref[0:c_len, :] = kc_ref[...]
        vz_ref[0:c_len, :] = vc_ref[...]

    row = lax.broadcasted_iota(jnp.int32, (2 * qb, 1), 0)
    low = lax.broadcasted_iota(jnp.int32, (1, LANES), 1) < HEAD_DIM

    def attend(n_keys, mask):
        for j in range(SWA_KV):
            qs = jnp.concatenate([q_ref[:, (2 * j) * LANES:(2 * j + 1) * LANES],
                                  q_ref[:, (2 * j + 1) * LANES:(2 * j + 2) * LANES]], axis=0)
            lo = slice((2 * j) * LANES, (2 * j + 1) * LANES)
            hi = slice((2 * j + 1) * LANES, (2 * j + 2) * LANES)
            s_all = _dot_nt(qs, jnp.concatenate([kz_ref[0:n_keys, lo], kz_ref[0:n_keys, hi]], axis=0))
            ps, sinks, ms = [], [], []
            for par in range(2):
                s = mask(s_all[:, par * n_keys:(par + 1) * n_keys])
                sink = LOG2E * jnp.where(row < qb, sink_ref[j * SWA_G + par], sink_ref[j * SWA_G + 2 + par])
                m = jnp.maximum(jnp.max(s, axis=-1, keepdims=True), sink)
                ps.append(jnp.exp2(s - m).astype(BF16))
                sinks.append(sink)
                ms.append(m)
            r_all = _dot(jnp.concatenate(ps, axis=0), vz_ref[0:n_keys, (2 * j) * LANES:(2 * j + 2) * LANES])
            halves = []
            for par in range(2):
                r = r_all[par * 2 * qb:(par + 1) * 2 * qb, par * LANES:(par + 1) * LANES]
                halves.append(r / (pltpu.roll(r, HEAD_DIM, axis=1) + jnp.exp2(sinks[par] - ms[par])))
            o = jnp.where(low, halves[0], halves[1]).astype(BF16)
            o_ref[:, (2 * j) * LANES:(2 * j + 1) * LANES] = o[0:qb]
            o_ref[:, (2 * j + 1) * LANES:(2 * j + 2) * LANES] = o[qb:]

    @pl.when(i < nb)
    def _():
        for b, (k_ref, v_ref) in enumerate(((kp_ref, vp_ref), (k0_ref, v0_ref), (kn_ref, vn_ref))):
            kz_ref[c_len + b * qb:c_len + (b + 1) * qb, :] = k_ref[...]
            vz_ref[c_len + b * qb:c_len + (b + 1) * qb, :] = v_ref[...]
        r_i = lax.broadcasted_iota(jnp.int32, (2 * qb, qb), 0) % qb
        c_i = lax.broadcasted_iota(jnp.int32, (2 * qb, qb), 1)
        b_prev = jnp.where(jnp.logical_and(c_i >= r_i, i > 0), 0.0, NEG)
        b_next = jnp.where(jnp.logical_and(c_i <= r_i, i < nb - 1), 0.0, NEG)

        def mask(s):
            return jnp.concatenate([s[:, :c_len], s[:, c_len:c_len + qb] + b_prev,
                                    s[:, c_len + qb:c_len + 2 * qb], s[:, c_len + 2 * qb:] + b_next], axis=1)

        attend(c_len + 3 * qb, mask)

    @pl.when(i >= nb)
    def _():
        attend(c_len, lambda s: s)


def _swa_attn(q, kz, vz, sink, s_len):
    t, nq = q.shape
    nz = kz.shape[1]
    c_len = t - s_len
    nb = s_len // Q_BLOCK
    n_blk = t // Q_BLOCK
    smem = pl.BlockSpec(memory_space=pltpu.SMEM)
    ctx_spec = pl.BlockSpec((c_len, nz), lambda i: (s_len // c_len, 0))
    band = [pl.BlockSpec((Q_BLOCK, nz), lambda i: (jnp.maximum(i - 1, 0), 0)),
            pl.BlockSpec((Q_BLOCK, nz), lambda i: (i, 0)),
            pl.BlockSpec((Q_BLOCK, nz), lambda i: (jnp.minimum(i + 1, n_blk - 1), 0))]
    n_keys = c_len + 3 * Q_BLOCK
    return pl.pallas_call(
        functools.partial(_swa_kernel, nb=nb),
        grid=(n_blk,),
        in_specs=[smem, pl.BlockSpec((Q_BLOCK, nq), lambda i: (i, 0))] + band + band + [ctx_spec, ctx_spec],
        out_specs=pl.BlockSpec((Q_BLOCK, nq), lambda i: (i, 0)),
        out_shape=jax.ShapeDtypeStruct((t, nq), BF16),
        scratch_shapes=[pltpu.VMEM((n_keys, nz), BF16), pltpu.VMEM((n_keys, nz), BF16)],
        compiler_params=_params(),
        name="swa_attn",
    )(sink, q, kz, kz, kz, vz, vz, vz, kz, vz)


def _rope_tables(s_len, c_len):
    rows = s_len // GRID_W
    freqs = ROPE_BASE ** (-jnp.arange(0, ROT_AXIS, 2, dtype=F32) / ROT_AXIS)
    dim = jnp.arange(LANES) % HEAD_DIM
    by_row = (dim // ROT_AXIS) == 0
    lane_freq = freqs[dim % (ROT_AXIS // 2)]
    sign = jnp.where((dim % ROT_AXIS) < ROT_AXIS // 2, -1.0, 1.0)
    ang_r = jnp.arange(rows, dtype=F32)[:, None] * lane_freq
    ang_c = jnp.arange(GRID_W, dtype=F32)[:, None] * lane_freq

    def table(fn, scale, ctx_fill):
        lat = jnp.where(by_row, (fn(ang_r) * scale)[:, None, :], (fn(ang_c) * scale)[None, :, :])
        return jnp.concatenate([lat.reshape(s_len, LANES), jnp.full((c_len, LANES), ctx_fill, F32)], axis=0)

    return table(jnp.cos, 1.0, 1.0), table(jnp.sin, sign, 0.0)


def _swa_out(o_ref):
    return o_ref[...]


def _swa_layer(xs, mods, norm_g, w1, w2, layer, w_qkv, sink, w_out, rope, n_lat, n_tot, n_out_tiles, s_len, final_g):
    w_qkv = w_qkv.astype(BF16)
    nq = SWA_HEADS * HEAD_DIM
    nz = 2 * SWA_KV * LANES
    x1, q, kz, vz = _pre_call("pre_swa", _swa_proj, xs, mods, norm_g, w1, w2, layer, n_lat, n_tot,
                              [w_qkv, rope[0], rope[1]],
                              [_full_spec(w_qkv.shape), _tile_spec(LANES, FFN_WARM), _tile_spec(LANES, FFN_WARM)],
                              [nq, nz, nz], [BF16, BF16, BF16])
    o = _swa_attn(q, kz, vz, sink, s_len)
    w_out = w_out.astype(BF16)
    return _post_call("post_swa", _swa_out, x1, mods, norm_g, w1, w2, layer, n_lat, n_out_tiles, [o, w_out],
                      [_tile_spec(nq, FFN_WARM), _full_spec(w_out.shape)], final_g)


def kernel(x, c, ctx, c_ctx, ada_w, ada_b, norm_g, final_g, ffn_w_in, ffn_w_out, gla_w_in, gla_w_a2, gla_b_a, gla_norm_g, gla_w_out, rg_w_in, rg_conv_w, rg_conv_b, rg_w_a, rg_b_a, rg_w_x, rg_b_x, rg_lam, rg_w_out, swa_w_qkv, swa_sink, swa_w_out):
    b, s_len, d = x.shape
    c_len = ctx.shape[1]
    assert b == 1 and s_len % TM == 0 and c_len % TM == 0 and s_len % c_len == 0
    depth = ada_w.shape[0]
    n_lat = s_len // TM
    n_tot = n_lat + c_len // TM
    mods = _ada_vectors(c, c_ctx, ada_w, ada_b)
    ng = norm_g.reshape(depth * 3, 1, d)
    w1, w2 = ffn_w_in, ffn_w_out
    gw = _gla_weights(gla_w_in, gla_w_a2, gla_b_a, gla_norm_g, gla_w_out)
    rope = _rope_tables(s_len, c_len)
    xs = (x[0], ctx[0])
    for i in range(depth):
        m, j = i % N_MIXERS, i // N_MIXERS
        last = i == depth - 1
        n_after = n_lat if last else n_tot
        fin = final_g if last else None
        if m == 0:
            xs = _gla_layer(xs, mods, ng, w1, w2, i, gw, j, n_lat, n_tot, n_after, fin)
        elif m == 1:
            xs = _rg_layer(xs, mods, ng, w1, w2, i, rg_w_in[j], rg_conv_w[j], rg_conv_b[j], rg_w_a[j], rg_b_a[j],
                           rg_w_x[j], rg_b_x[j], rg_lam[j], rg_w_out[j], n_lat, n_tot, n_after, fin)
        else:
            xs = _swa_layer(xs, mods, ng, w1, w2, i, swa_w_qkv[j], swa_sink[j], swa_w_out[j], rope, n_lat, n_tot,
                            n_after, s_len, fin)
    return xs[None]
```

```python
import functools
import math

import jax
import jax.numpy as jnp
from jax import lax
from jax.experimental import pallas as pl
from jax.experimental.pallas import tpu as pltpu

F32 = jnp.float32
BF16 = jnp.bfloat16

N_MOD = 9
NORM_EPS = 1e-6
GLA_HEADS = 4
GLA_DK = 128
GLA_DV = 256
GLA_RANK = 16
GLA_LOGIT_NORM = 16.0
GLA_CHUNK = 64
GLA_HK = GLA_HEADS * GLA_DK
GLA_HV = GLA_HEADS * GLA_DV
GLA_LR_PAD = 128
D_RNN = 1536
RG_BLOCKS = 12
RG_BS = D_RNN // RG_BLOCKS
RG_C = 8.0
SWA_HEADS = 16
SWA_KV = 4
SWA_G = SWA_HEADS // SWA_KV
HEAD_DIM = 64
WINDOW = 128
Q_BLOCK = 128
ROT_AXIS = HEAD_DIM // 2
ROPE_BASE = 10000.0
GRID_W = 64
N_MIXERS = 3

TM = 512
TB = 256
FFN_WARM = 16
SUBLANES = 8
LANES = 128
VMEM_LIMIT = 60 * 1024 * 1024
NEG = -1e30
LOG2E = 1.0 / math.log(2.0)


def _odd_pitch(rows):
    tiles = -(-rows // SUBLANES)
    return (tiles + (tiles % 2 == 0)) * SUBLANES


RG_SEG = TB // SUBLANES
RG_HPITCH = _odd_pitch(RG_SEG)
RG_XPITCH = _odd_pitch(RG_SEG + 2 * SUBLANES)


def _params(n_axes=1):
    return pltpu.CompilerParams(dimension_semantics=("arbitrary",) * n_axes, vmem_limit_bytes=VMEM_LIMIT)


def _dot(a, b):
    return jnp.dot(a, b, preferred_element_type=F32)


def _dot_nt(a, b):
    return lax.dot_general(a, b, (((1,), (1,)), ((), ())), preferred_element_type=F32)


def _dot_tn(a, b):
    return lax.dot_general(a, b, (((0,), (0,)), ((), ())), preferred_element_type=F32)


def _rms(x, g):
    return (x * lax.rsqrt(jnp.mean(x * x, axis=-1, keepdims=True) + NORM_EPS)) * g


def _rms_mod(x, g, shift, scale):
    return _rms(x, g) * (1.0 + scale) + shift


def _sigmoid(x):
    return 1.0 / (1.0 + jnp.exp(-x))


def _silu(x):
    xh = 0.5 * x
    return xh + xh * jnp.tanh(xh)


def _softplus(x):
    return jnp.maximum(x, 0.0) + jnp.log1p(jnp.exp(-jnp.abs(x)))


def _gelu_tanh(x):
    return 0.5 * x * (1.0 + jnp.tanh(math.sqrt(2.0 / math.pi) * (x + 0.044715 * (x * x * x))))


def _row(ref, r):
    return ref[r:r + 1, :]


def _tile_spec(width, shift=0):
    return pl.BlockSpec((TM, width), lambda i: (jnp.maximum(i - shift, 0), 0))


def _full_spec(shape):
    nd = len(shape)
    return pl.BlockSpec(shape, lambda i: (0,) * nd, pipeline_mode=pl.Buffered(1))


def _pick_spec(shape, lead):
    tail = shape[len(lead):]
    return pl.BlockSpec((None,) * len(lead) + tuple(tail), lambda i: tuple(lead) + (0,) * len(tail),
                        pipeline_mode=pl.Buffered(1))


def _mod_spec(mods, layer, n_lat, shift=0):
    _, _, n_mod, d = mods.shape
    return pl.BlockSpec((None, None, n_mod, d),
                        lambda i: (layer, jnp.minimum(jnp.maximum(i - shift, 0) // n_lat, 1), 0, 0))


def _ada_kernel(cc_ref, w_ref, b_ref, o_ref):
    act = _silu(cc_ref[...]).astype(BF16)
    o_ref[...] = _dot(act, w_ref[...].astype(BF16)) + b_ref[...]


def _ada_vectors(c, c_ctx, ada_w, ada_b):
    depth, d, nd = ada_w.shape
    cc = jnp.zeros((SUBLANES, d), F32).at[0].set(c[0]).at[1].set(c_ctx)
    out = pl.pallas_call(
        _ada_kernel,
        grid=(depth, nd // d),
        in_specs=[pl.BlockSpec((SUBLANES, d), lambda l, j: (0, 0)),
                  pl.BlockSpec((None, d, d), lambda l, j: (l, 0, j)),
                  pl.BlockSpec((None, 1, d), lambda l, j: (l, 0, j))],
        out_specs=pl.BlockSpec((None, SUBLANES, d), lambda l, j: (l, 0, j)),
        out_shape=jax.ShapeDtypeStruct((depth, SUBLANES, nd), F32),
        compiler_params=_params(2),
        name="ada_vectors",
    )(cc, ada_w, ada_b.reshape(depth, 1, nd))
    return out.reshape(depth, SUBLANES, N_MOD, d)


def _swiglu_step(x, mod_ref, r0, g_ref, w1_ref, w2_ref):
    h = _rms_mod(x, g_ref[...], _row(mod_ref, r0), _row(mod_ref, r0 + 1)).astype(BF16)
    gu = _dot(h, w1_ref[...])
    f = w2_ref.shape[0]
    a = (_silu(gu[:, :f]) * gu[:, f:]).astype(BF16)
    return x + (0.5 * _row(mod_ref, r0 + 2)) * _dot(a, w2_ref[...])


def _stage_weights(w1c_ref, w2c_ref, w1_ref, w2_ref):
    i = pl.program_id(0)

    @pl.when(i < FFN_WARM)
    def _():
        for c_ref, s_ref in ((w1c_ref, w1_ref), (w2c_ref, w2_ref)):
            r = c_ref.shape[0]
            s_ref[pl.ds(pl.multiple_of(i * r, r), r), :] = c_ref[...].astype(BF16)


def _ffn_weight_specs(w1, w2, layer, half):
    specs, scratch = [], []
    for w in (w1, w2):
        rows, cols = w.shape[-2:]
        assert rows % (FFN_WARM * 2 * SUBLANES) == 0
        specs.append(pl.BlockSpec((None, None, rows // FFN_WARM, cols),
                                  lambda i: (layer, half, jnp.minimum(i, FFN_WARM - 1), 0)))
        scratch.append(pltpu.VMEM((rows, cols), BF16))
    return specs, scratch


def _pre_kernel(*refs, proj, n_par, n_lat, split_in):
    refs = list(refs)
    w1_ref, w2_ref = refs[-2:]
    refs = refs[:-2]
    x_refs, refs = (refs[:2], refs[2:]) if split_in else (refs[:1], refs[1:])
    mod_ref, g1_ref, w1c_ref, w2c_ref, gm_ref = refs[:5]
    par, x1_ref, outs = refs[5:5 + n_par], refs[5 + n_par], refs[6 + n_par:]
    _stage_weights(w1c_ref, w2c_ref, w1_ref, w2_ref)

    @pl.when(pl.program_id(0) >= FFN_WARM)
    def _():
        if split_in:
            x = jnp.where(pl.program_id(0) - FFN_WARM < n_lat, x_refs[0][...], x_refs[1][...])
        else:
            x = x_refs[0][...]
        x1 = _swiglu_step(x, mod_ref, 0, g1_ref, w1_ref, w2_ref)
        x1_ref[...] = x1
        h = _rms_mod(x1, gm_ref[...], _row(mod_ref, 3), _row(mod_ref, 4)).astype(BF16)
        proj(h, *par, *outs)


def _pre_call(name, proj, xs, mods, norm_g, w1, w2, layer, n_lat, n_tiles, par, par_specs, out_widths, out_dtypes):
    split_in = isinstance(xs, tuple)
    d = (xs[0] if split_in else xs).shape[1]
    tile = lambda i: jnp.maximum(i - FFN_WARM, 0)
    if split_in:
        x_specs = [pl.BlockSpec((TM, d), lambda i: (jnp.minimum(tile(i), n_lat - 1), 0)),
                   pl.BlockSpec((TM, d), lambda i: (jnp.maximum(tile(i) - n_lat, 0), 0))]
        x_args = list(xs)
    else:
        x_specs, x_args = [_tile_spec(d, FFN_WARM)], [xs]
    t = n_tiles * TM
    w_specs, w_scratch = _ffn_weight_specs(w1, w2, layer, 0)
    return pl.pallas_call(
        functools.partial(_pre_kernel, proj=proj, n_par=len(par), n_lat=n_lat, split_in=split_in),
        grid=(FFN_WARM + n_tiles,),
        in_specs=x_specs + [_mod_spec(mods, layer, n_lat, FFN_WARM), _pick_spec(norm_g.shape, (3 * layer,))] + w_specs
        + [_pick_spec(norm_g.shape, (3 * layer + 1,))] + par_specs,
        out_specs=[_tile_spec(d, FFN_WARM)] + [_tile_spec(w, FFN_WARM) for w in out_widths],
        out_shape=[jax.ShapeDtypeStruct((t, d), F32)]
        + [jax.ShapeDtypeStruct((t, w), dt) for w, dt in zip(out_widths, out_dtypes)],
        scratch_shapes=w_scratch,
        compiler_params=_params(),
        name=name,
    )(*x_args, mods, norm_g, w1, w2, norm_g, *par)


def _post_kernel(*refs, head, n_head, final):
    x_ref, mod_ref = refs[:2]
    head_refs = refs[2:2 + n_head]
    g2_ref, w1c_ref, w2c_ref = refs[2 + n_head:5 + n_head]
    o_ref, w1_ref, w2_ref = refs[-3:]
    _stage_weights(w1c_ref, w2c_ref, w1_ref, w2_ref)

    @pl.when(pl.program_id(0) >= FFN_WARM)
    def _():
        y = head(*head_refs[:-1])
        x2 = x_ref[...] + _row(mod_ref, 5) * _dot(y, head_refs[-1][...])
        x3 = _swiglu_step(x2, mod_ref, 6, g2_ref, w1_ref, w2_ref)
        o_ref[...] = _rms(x3, refs[5 + n_head][...]) if final else x3


def _post_call(name, head, xs, mods, norm_g, w1, w2, layer, n_lat, n_tiles, head_args, head_specs, final_g):
    d = xs.shape[1]
    fin_specs, fin_args = ([_full_spec((1, d))], [final_g.reshape(1, d)]) if final_g is not None else ([], [])
    w_specs, w_scratch = _ffn_weight_specs(w1, w2, layer, 1)
    return pl.pallas_call(
        functools.partial(_post_kernel, head=head, n_head=len(head_args), final=final_g is not None),
        grid=(FFN_WARM + n_tiles,),
        in_specs=[_tile_spec(d, FFN_WARM), _mod_spec(mods, layer, n_lat, FFN_WARM)] + head_specs
        + [_pick_spec(norm_g.shape, (3 * layer + 2,))] + w_specs + fin_specs,
        out_specs=_tile_spec(d, FFN_WARM),
        out_shape=jax.ShapeDtypeStruct((n_tiles * TM, d), F32),
        scratch_shapes=w_scratch,
        compiler_params=_params(),
        name=name,
    )(xs, mods, *head_args, norm_g, w1, w2, *fin_args)


def _gla_proj(h, w_ref, w2_ref, ba_ref, q_ref, k_ref, v_ref, og_ref, gg_ref):
    y = _dot(h, w_ref[...])
    q_ref[...] = (y[:, :GLA_HK] * (GLA_DK ** -0.5)).astype(BF16)
    k_ref[...] = y[:, GLA_HK:2 * GLA_HK].astype(BF16)
    v_ref[...] = y[:, 2 * GLA_HK:2 * GLA_HK + GLA_HV].astype(BF16)
    og_ref[...] = y[:, 2 * GLA_HK + GLA_HV:2 * GLA_HK + 2 * GLA_HV].astype(BF16)
    lr = y[:, 2 * GLA_HK + 2 * GLA_HV:].astype(BF16)
    logit = _dot(lr, w2_ref[...]) + ba_ref[...]
    gg_ref[...] = (-_softplus(-logit)) * (LOG2E / GLA_LOGIT_NORM)


def _gla_tile(q_ref, k_ref, v_ref, g_ref, o_ref, s_ref, rev):
    c = GLA_CHUNK
    tm = q_ref.shape[0]
    nc = tm // c
    r_i = lax.broadcasted_iota(jnp.int32, (tm, tm), 0)
    c_i = lax.broadcasted_iota(jnp.int32, (tm, tm), 1)
    same = (r_i // c) == (c_i // c)
    tri = jnp.logical_and(same, (c_i >= r_i) if rev else (c_i <= r_i))
    tri_bf = tri.astype(BF16)
    g = g_ref[...]
    g_hi = g.astype(BF16)
    g_lo = (g - g_hi.astype(F32)).astype(BF16)
    b = _dot(tri_bf, g_hi) + _dot(tri_bf, g_lo)
    ends = [b[n * c:n * c + 1, :] if rev else b[(n + 1) * c - 1:(n + 1) * c, :] for n in range(nc)]
    btot = jnp.concatenate([jnp.broadcast_to(e, (c, e.shape[1])) for e in ends], axis=0)
    k = k_ref[...].astype(F32)
    qe = (q_ref[...].astype(F32) * jnp.exp2(b)).astype(BF16)
    ke = (k * jnp.exp2(-b)).astype(BF16)
    kd = (k * jnp.exp2(btot - b)).astype(BF16)
    decs = [jnp.exp2(e) for e in ends]
    own = (lax.broadcasted_iota(jnp.int32, (tm, nc * GLA_DK), 0) // c
           == lax.broadcasted_iota(jnp.int32, (tm, nc * GLA_DK), 1) // GLA_DK)
    for h in range(GLA_HEADS):
        ks = slice(h * GLA_DK, (h + 1) * GLA_DK)
        vs = slice(h * GLA_DV, (h + 1) * GLA_DV)
        vh = v_ref[:, vs]
        att = jnp.where(tri, _dot_nt(qe[:, ks], ke[:, ks]), 0.0).astype(BF16)
        o_in = _dot(att, vh)
        kd_bd = jnp.where(own, jnp.concatenate([kd[:, ks]] * nc, axis=1), jnp.zeros((), BF16))
        upd = _dot_tn(vh, kd_bd)
        st = s_ref[h]
        for n in (reversed(range(nc)) if rev else range(nc)):
            rows = slice(n * c, (n + 1) * c)
            o_ref[rows, vs] = (o_in[rows, :] + _dot_nt(qe[rows, ks], st.astype(BF16))).astype(o_ref.dtype)
            st = decs[n][:, ks] * st + upd[:, n * GLA_DK:(n + 1) * GLA_DK]
        s_ref[h] = st


def _gla_kernel(qf_ref, kf_ref, vf_ref, gf_ref, qb_ref, kb_ref, vb_ref, gb_ref, of_ref, ob_ref, sf_ref, sb_ref):
    @pl.when(pl.program_id(0) == 0)
    def _():
        sf_ref[...] = jnp.zeros_like(sf_ref)
        sb_ref[...] = jnp.zeros_like(sb_ref)

    _gla_tile(qf_ref, kf_ref, vf_ref, gf_ref, of_ref, sf_ref, False)
    _gla_tile(qb_ref, kb_ref, vb_ref, gb_ref, ob_ref, sb_ref, True)


def _block_order(s, nb_lat, nb_real, rev):
    real = (nb_real - 1 - s) if rev else (s + nb_lat) % nb_real
    return jnp.where(s < nb_real, real, s)


def _gla_scan(q, k, v, gg, nb_lat, nb_real):
    t = q.shape[0]
    fwd = lambda s: _block_order(s, nb_lat, nb_real, False)
    bwd = lambda s: _block_order(s, nb_lat, nb_real, True)

    def specs(blk, col):
        return [pl.BlockSpec((TB, GLA_HK), lambda s: (blk(s), 0)), pl.BlockSpec((TB, GLA_HK), lambda s: (blk(s), 0)),
                pl.BlockSpec((TB, GLA_HV), lambda s: (blk(s), 0)), pl.BlockSpec((TB, GLA_HK), lambda s: (blk(s), col))]

    return pl.pallas_call(
        _gla_kernel,
        grid=(t // TB,),
        in_specs=specs(fwd, 0) + specs(bwd, 1),
        out_specs=[pl.BlockSpec((TB, GLA_HV), lambda s: (fwd(s), 0)), pl.BlockSpec((TB, GLA_HV), lambda s: (bwd(s), 0))],
        out_shape=[jax.ShapeDtypeStruct((t, GLA_HV), BF16)] * 2,
        scratch_shapes=[pltpu.VMEM((GLA_HEADS, GLA_DV, GLA_DK), F32)] * 2,
        compiler_params=_params(),
        name="gla_scan",
    )(q, k, v, gg, q, k, v, gg)


def _gla_out(of_ref, ob_ref, og_ref, ng_ref):
    o = of_ref[...].astype(F32) + ob_ref[...].astype(F32)
    ng = ng_ref[...]
    parts = [_rms(o[:, h * GLA_DV:(h + 1) * GLA_DV], ng) for h in range(GLA_HEADS)]
    return (jnp.concatenate(parts, axis=1) * _silu(og_ref[...].astype(F32))).astype(BF16)


def _gla_weights(w_in, w_a2, b_a, gla_norm_g, w_out):
    na, d, n_in = w_in.shape
    n_main = 2 * GLA_HK + 2 * GLA_HV
    w = jnp.pad(w_in.astype(BF16), ((0, 0), (0, 0), (0, n_main + GLA_LR_PAD - n_in)))
    w2 = jnp.zeros((na, GLA_LR_PAD, 2 * GLA_HK), F32)
    w2 = w2.at[:, :GLA_RANK, :GLA_HK].set(w_a2[:, 0]).at[:, GLA_RANK:2 * GLA_RANK, GLA_HK:].set(w_a2[:, 1])
    return (w, w2.astype(BF16), b_a.reshape(na, 1, 2 * GLA_HK), gla_norm_g.reshape(na, 1, GLA_DV), w_out.astype(BF16))


def _gla_layer(xs, mods, norm_g, w1, w2, layer, gw, j, n_lat, n_tot, n_out_tiles, nb_lat, nb_real, final_g):
    w, wlr, ba, ng, w_out = gw
    x1, q, k, v, og, gg = _pre_call(
        "pre_gla", _gla_proj, xs, mods, norm_g, w1, w2, layer, n_lat, n_tot, [w, wlr, ba],
        [_pick_spec(w.shape, (j,)), _pick_spec(wlr.shape, (j,)), _pick_spec(ba.shape, (j,))],
        [GLA_HK, GLA_HK, GLA_HV, GLA_HV, 2 * GLA_HK], [BF16, BF16, BF16, BF16, F32])
    o_f, o_b = _gla_scan(q, k, v, gg, nb_lat, nb_real)
    return _post_call(
        "post_gla", _gla_out, x1, mods, norm_g, w1, w2, layer, n_lat, n_out_tiles, [o_f, o_b, og, ng, w_out],
        [_tile_spec(GLA_HV, FFN_WARM)] * 3 + [_pick_spec(ng.shape, (j,)), _pick_spec(w_out.shape, (j,))], final_g)


def _rg_proj(h, w_ref, gate_ref, xr_ref):
    y = _dot(h, w_ref[...])
    gate_ref[...] = _gelu_tanh(y[:, :D_RNN]).astype(BF16)
    xr_ref[...] = y[:, D_RNN:]


def _rg_scan_kernel(xm_ref, xp_ref, xn_ref, cw_ref, cb_ref, w_ref, ba_ref, bx_ref, lam_ref, h_ref,
                    xs_ref, a_ref, u_ref, xp_pad_ref, hp_pad_ref, carry_ref, *, rev, nb_lat, nb_real):
    s = pl.program_id(0)
    blk = _block_order(s, nb_lat, nb_real, rev)

    @pl.when(s == 0)
    def _():
        carry_ref[...] = jnp.zeros_like(carry_ref)

    tm = xm_ref.shape[0]
    hal = SUBLANES
    seg = tm // SUBLANES
    nt = a_ref.shape[0]
    assert seg == RG_SEG and RG_BS == LANES
    has_prev = jnp.logical_and(blk != 0, jnp.logical_and(blk != nb_lat, blk != nb_real))
    has_next = jnp.logical_and(blk != nb_lat - 1, jnp.logical_and(blk != nb_real - 1, blk != pl.num_programs(0) - 1))
    xs_ref[0:hal, :] = jnp.where(has_prev, xp_ref[...], 0.0)
    xs_ref[hal:hal + tm, :] = xm_ref[...]
    xs_ref[hal + tm:, :] = jnp.where(has_next, xn_ref[...], 0.0)
    czh = (-0.5 * RG_C) * _softplus(-lam_ref[...])
    ext = seg + 2 * hal
    for n in range(nt):
        ns = slice(n * LANES, (n + 1) * LANES)
        for sg in range(SUBLANES):
            xp_pad_ref[n, sg * RG_XPITCH:sg * RG_XPITCH + ext, :] = xs_ref[sg * seg:sg * seg + ext, ns]
        xe = [xp_pad_ref[n, pl.ds(k, SUBLANES, stride=RG_XPITCH), :] for k in range(hal - 2, hal + seg + 1)]
        cw = [jnp.broadcast_to(cw_ref[j:j + 1, ns], (SUBLANES, LANES)) for j in range(4)]
        cb = jnp.broadcast_to(cb_ref[:, ns], (SUBLANES, LANES))
        xc = jnp.concatenate(
            [cb + cw[0] * xe[i] + cw[1] * xe[i + 1] + cw[2] * xe[i + 2] + cw[3] * xe[i + 3] for i in range(seg)],
            axis=0)
        z = _dot(xc.astype(BF16), w_ref[n])
        t_r = jnp.tanh(z[:, :RG_BS] + ba_ref[:, ns])
        t_i = jnp.tanh(z[:, RG_BS:] + bx_ref[:, ns])
        a = jnp.exp(czh[:, ns] + czh[:, ns] * t_r)
        a_ref[n] = a
        y = 1.0 - a * a
        xch = 0.5 * xc
        u_ref[n] = jnp.where(y > 0.0, y * lax.rsqrt(y), 0.0) * (xch + xch * t_i)

    def vreg(i):
        return pl.ds(pl.multiple_of(i * SUBLANES, SUBLANES), SUBLANES)

    def local(it, carry):
        i = (seg - 1 - it) if rev else it
        out = []
        for n in range(nt):
            a = a_ref[n, vreg(i), :]
            out += [a * carry[2 * n] + u_ref[n, vreg(i), :], a * carry[2 * n + 1]]
        return tuple(out)

    init = (jnp.zeros((SUBLANES, LANES), F32), jnp.ones((SUBLANES, LANES), F32)) * nt
    ends = lax.fori_loop(0, seg, local, init, unroll=4)
    cmats = []
    for n in range(nt):
        ns = slice(n * LANES, (n + 1) * LANES)
        hl_end, p_end = ends[2 * n], ends[2 * n + 1]
        c = carry_ref[:, ns]
        cin = [None] * SUBLANES
        for sg in (reversed(range(SUBLANES)) if rev else range(SUBLANES)):
            cin[sg] = c
            c = hl_end[sg:sg + 1, :] + p_end[sg:sg + 1, :] * c
        carry_ref[:, ns] = c
        cmats.append(jnp.concatenate(cin, axis=0))

    def final(it, carry):
        i = (seg - 1 - it) if rev else it
        out = []
        for n in range(nt):
            h = a_ref[n, vreg(i), :] * carry[n] + u_ref[n, vreg(i), :]
            hp_pad_ref[n, pl.ds(i, SUBLANES, stride=RG_HPITCH), :] = h
            out.append(h)
        return tuple(out)

    lax.fori_loop(0, seg, final, tuple(cmats), unroll=4)
    for n in range(nt):
        for sg in range(SUBLANES):
            h_ref[sg * seg:(sg + 1) * seg, n * LANES:(n + 1) * LANES] = (
                hp_pad_ref[n, sg * RG_HPITCH:sg * RG_HPITCH + seg, :].astype(h_ref.dtype))


def _rg_scan(xr, cw, cb, w, ba, bx, lam, z, nb_lat, nb_real):
    t, dr = xr.shape
    per = TB // SUBLANES
    n8 = t // SUBLANES
    rev = z == 1
    blk = lambda s: _block_order(s, nb_lat, nb_real, rev)
    return pl.pallas_call(
        functools.partial(_rg_scan_kernel, rev=rev, nb_lat=nb_lat, nb_real=nb_real),
        grid=(t // TB,),
        in_specs=[pl.BlockSpec((TB, dr), lambda s: (blk(s), 0)),
                  pl.BlockSpec((SUBLANES, dr), lambda s: (jnp.maximum(blk(s) * per - 1, 0), 0)),
                  pl.BlockSpec((SUBLANES, dr), lambda s: (jnp.minimum((blk(s) + 1) * per, n8 - 1), 0)),
                  _full_spec(cw.shape), _full_spec(cb.shape), _pick_spec(w.shape, (z,)),
                  _pick_spec(ba.shape, (z,)), _pick_spec(bx.shape, (z,)), _pick_spec(lam.shape, (z,))],
        out_specs=pl.BlockSpec((TB, dr), lambda s: (blk(s), 0)),
        out_shape=jax.ShapeDtypeStruct((t, dr), BF16),
        scratch_shapes=[pltpu.VMEM((TB + 2 * SUBLANES, dr), F32),
                        pltpu.VMEM((dr // LANES, TB, LANES), F32), pltpu.VMEM((dr // LANES, TB, LANES), F32),
                        pltpu.VMEM((dr // LANES, SUBLANES * RG_XPITCH, LANES), F32),
                        pltpu.VMEM((dr // LANES, SUBLANES * RG_HPITCH, LANES), F32),
                        pltpu.VMEM((1, dr), F32)],
        compiler_params=_params(),
        name="rg_scan_bwd" if rev else "rg_scan_fwd",
    )(xr, xr, xr, cw, cb, w, ba, bx, lam)


def _rg_out(hf_ref, hb_ref, gate_ref):
    return ((hf_ref[...].astype(F32) + hb_ref[...].astype(F32)) * gate_ref[...].astype(F32)).astype(BF16)


def _rg_layer(xs, mods, norm_g, w1, w2, layer, w_in, conv_w, conv_b, w_a, b_a, w_x, b_x, lam, w_out, n_lat, n_tot,
              n_out_tiles, nb_lat, nb_real, final_g):
    w_in = w_in.astype(BF16)
    x1, gate, xr = _pre_call("pre_rg", _rg_proj, xs, mods, norm_g, w1, w2, layer, n_lat, n_tot, [w_in],
                             [_full_spec(w_in.shape)], [D_RNN, D_RNN], [BF16, F32])
    w = (0.5 * jnp.concatenate([w_a, w_x], axis=-1)).astype(BF16)
    hs = [_rg_scan(xr, conv_w, conv_b.reshape(1, D_RNN), w, (0.5 * b_a).reshape(2, 1, D_RNN),
                   (0.5 * b_x).reshape(2, 1, D_RNN), lam.reshape(2, 1, D_RNN), z, nb_lat, nb_real) for z in range(2)]
    w_out = w_out.astype(BF16)
    return _post_call("post_rg", _rg_out, x1, mods, norm_g, w1, w2, layer, n_lat, n_out_tiles,
                      [hs[0], hs[1], gate, w_out], [_tile_spec(D_RNN, FFN_WARM)] * 3 + [_full_spec(w_out.shape)],
                      final_g)


def _swa_proj(h, w_ref, cos_ref, sin_ref, q_ref, kz_ref, vz_ref):
    y = _dot(h, w_ref[...])
    nq = SWA_HEADS * HEAD_DIM
    nk = SWA_KV * HEAD_DIM
    cos = cos_ref[...]
    sin = sin_ref[...]
    rot = ROT_AXIS // 2
    lane = lax.broadcasted_iota(jnp.int32, (1, LANES), 1)
    first = (lane % ROT_AXIS) < rot

    def rope(xg):
        partner = jnp.where(first, pltpu.roll(xg, LANES - rot, axis=1), pltpu.roll(xg, rot, axis=1))
        return xg * cos + partner * sin

    q = y[:, :nq] * (HEAD_DIM ** -0.5 * LOG2E)
    for gi in range(nq // LANES):
        q_ref[:, gi * LANES:(gi + 1) * LANES] = rope(q[:, gi * LANES:(gi + 1) * LANES]).astype(BF16)
    low = lane < HEAD_DIM
    for gi in range(nk // LANES):
        kg = rope(y[:, nq + gi * LANES:nq + (gi + 1) * LANES])
        vg = y[:, nq + nk + gi * LANES:nq + nk + (gi + 1) * LANES]
        for par_j in range(2):
            j = 2 * gi + par_j
            for par_q in range(2):
                src_k = kg if par_q == par_j else pltpu.roll(kg, HEAD_DIM, axis=1)
                src_v = vg if par_q == par_j else pltpu.roll(vg, HEAD_DIM, axis=1)
                keep = low if par_q == 0 else jnp.logical_not(low)
                cs = slice((2 * j + par_q) * LANES, (2 * j + par_q + 1) * LANES)
                kz_ref[:, cs] = jnp.where(keep, src_k, 0.0).astype(BF16)
                vz_ref[:, cs] = jnp.where(keep, src_v, 1.0).astype(BF16)


def _swa_kernel(sink_ref, q_ref, kp_ref, k0_ref, kn_ref, vp_ref, v0_ref, vn_ref, kc_ref, vc_ref, o_ref,
                kz_ref, vz_ref, *, nb):
    qb = q_ref.shape[0]
    c_len = kc_ref.shape[0]
    i = pl.program_id(0)

    @pl.when(i == 0)
    def _():
        kz_ref[0:c_len, :] = kc_ref[...]
        vz_ref[0:c_len, :] = vc_ref[...]

    row = lax.broadcasted_iota(jnp.int32, (2 * qb, 1), 0)
    low = lax.broadcasted_iota(jnp.int32, (1, LANES), 1) < HEAD_DIM

    def attend(n_keys, mask):
        for j in range(SWA_KV):
            qs = jnp.concatenate([q_ref[:, (2 * j) * LANES:(2 * j + 1) * LANES],
                                  q_ref[:, (2 * j + 1) * LANES:(2 * j + 2) * LANES]], axis=0)
            lo = slice((2 * j) * LANES, (2 * j + 1) * LANES)
            hi = slice((2 * j + 1) * LANES, (2 * j + 2) * LANES)
            s_all = _dot_nt(qs, jnp.concatenate([kz_ref[0:n_keys, lo], kz_ref[0:n_keys, hi]], axis=0))
            ps, sinks, ms = [], [], []
            for par in range(2):
                s = mask(s_all[:, par * n_keys:(par + 1) * n_keys])
                sink = LOG2E * jnp.where(row < qb, sink_ref[j * SWA_G + par], sink_ref[j * SWA_G + 2 + par])
                m = jnp.maximum(jnp.max(s, axis=-1, keepdims=True), sink)
                ps.append(jnp.exp2(s - m).astype(BF16))
                sinks.append(sink)
                ms.append(m)
            r_all = _dot(jnp.concatenate(ps, axis=0), vz_ref[0:n_keys, (2 * j) * LANES:(2 * j + 2) * LANES])
            halves = []
            for par in range(2):
                r = r_all[par * 2 * qb:(par + 1) * 2 * qb, par * LANES:(par + 1) * LANES]
                halves.append(r / (pltpu.roll(r, HEAD_DIM, axis=1) + jnp.exp2(sinks[par] - ms[par])))
            o = jnp.where(low, halves[0], halves[1]).astype(BF16)
            o_ref[:, (2 * j) * LANES:(2 * j + 1) * LANES] = o[0:qb]
            o_ref[:, (2 * j + 1) * LANES:(2 * j + 2) * LANES] = o[qb:]

    @pl.when(i < nb)
    def _():
        for b, (k_ref, v_ref) in enumerate(((kp_ref, vp_ref), (k0_ref, v0_ref), (kn_ref, vn_ref))):
            kz_ref[c_len + b * qb:c_len + (b + 1) * qb, :] = k_ref[...]
            vz_ref[c_len + b * qb:c_len + (b + 1) * qb, :] = v_ref[...]
        r_i = lax.broadcasted_iota(jnp.int32, (2 * qb, qb), 0) % qb
        c_i = lax.broadcasted_iota(jnp.int32, (2 * qb, qb), 1)
        b_prev = jnp.where(jnp.logical_and(c_i >= r_i, i > 0), 0.0, NEG)
        b_next = jnp.where(jnp.logical_and(c_i <= r_i, i < nb - 1), 0.0, NEG)

        def mask(s):
            return jnp.concatenate([s[:, :c_len], s[:, c_len:c_len + qb] + b_prev,
                                    s[:, c_len + qb:c_len + 2 * qb], s[:, c_len + 2 * qb:] + b_next], axis=1)

        attend(c_len + 3 * qb, mask)

    @pl.when(i >= nb)
    def _():
        attend(c_len, lambda s: s)


def _swa_attn(q, kz, vz, sink, s_len, c_len):
    t, nq = q.shape
    nz = kz.shape[1]
    nb = s_len // Q_BLOCK
    n_blk = t // Q_BLOCK
    smem = pl.BlockSpec(memory_space=pltpu.SMEM)
    ctx_spec = pl.BlockSpec((c_len, nz), lambda i: (s_len // c_len, 0))
    band = [pl.BlockSpec((Q_BLOCK, nz), lambda i: (jnp.maximum(i - 1, 0), 0)),
            pl.BlockSpec((Q_BLOCK, nz), lambda i: (i, 0)),
            pl.BlockSpec((Q_BLOCK, nz), lambda i: (jnp.minimum(i + 1, n_blk - 1), 0))]
    n_keys = c_len + 3 * Q_BLOCK
    return pl.pallas_call(
        functools.partial(_swa_kernel, nb=nb),
        grid=(n_blk,),
        in_specs=[smem, pl.BlockSpec((Q_BLOCK, nq), lambda i: (i, 0))] + band + band + [ctx_spec, ctx_spec],
        out_specs=pl.BlockSpec((Q_BLOCK, nq), lambda i: (i, 0)),
        out_shape=jax.ShapeDtypeStruct((t, nq), BF16),
        scratch_shapes=[pltpu.VMEM((n_keys, nz), BF16), pltpu.VMEM((n_keys, nz), BF16)],
        compiler_params=_params(),
        name="swa_attn",
    )(sink, q, kz, kz, kz, vz, vz, vz, kz, vz)


def _rope_tables(s_len, c_len):
    rows = s_len // GRID_W
    freqs = ROPE_BASE ** (-jnp.arange(0, ROT_AXIS, 2, dtype=F32) / ROT_AXIS)
    dim = jnp.arange(LANES) % HEAD_DIM
    by_row = (dim // ROT_AXIS) == 0
    lane_freq = freqs[dim % (ROT_AXIS // 2)]
    sign = jnp.where((dim % ROT_AXIS) < ROT_AXIS // 2, -1.0, 1.0)
    ang_r = jnp.arange(rows, dtype=F32)[:, None] * lane_freq
    ang_c = jnp.arange(GRID_W, dtype=F32)[:, None] * lane_freq

    def table(fn, scale, ctx_fill):
        lat = jnp.where(by_row, (fn(ang_r) * scale)[:, None, :], (fn(ang_c) * scale)[None, :, :])
        return jnp.concatenate([lat.reshape(s_len, LANES), jnp.full((c_len, LANES), ctx_fill, F32)], axis=0)

    return table(jnp.cos, 1.0, 1.0), table(jnp.sin, sign, 0.0)


def _swa_out(o_ref):
    return o_ref[...]


def _swa_layer(xs, mods, norm_g, w1, w2, layer, w_qkv, sink, w_out, rope, n_lat, n_tot, n_out_tiles, s_len, c_len,
               final_g):
    w_qkv = w_qkv.astype(BF16)
    nq = SWA_HEADS * HEAD_DIM
    nz = 2 * SWA_KV * LANES
    x1, q, kz, vz = _pre_call("pre_swa", _swa_proj, xs, mods, norm_g, w1, w2, layer, n_lat, n_tot,
                              [w_qkv, rope[0], rope[1]],
                              [_full_spec(w_qkv.shape), _tile_spec(LANES, FFN_WARM), _tile_spec(LANES, FFN_WARM)],
                              [nq, nz, nz], [BF16, BF16, BF16])
    o = _swa_attn(q, kz, vz, sink, s_len, c_len)
    w_out = w_out.astype(BF16)
    return _post_call("post_swa", _swa_out, x1, mods, norm_g, w1, w2, layer, n_lat, n_out_tiles, [o, w_out],
                      [_tile_spec(nq, FFN_WARM), _full_spec(w_out.shape)], final_g)


def kernel(x, c, ctx, c_ctx, ada_w, ada_b, norm_g, final_g, ffn_w_in, ffn_w_out, gla_w_in, gla_w_a2, gla_b_a, gla_norm_g, gla_w_out, rg_w_in, rg_conv_w, rg_conv_b, rg_w_a, rg_b_a, rg_w_x, rg_b_x, rg_lam, rg_w_out, swa_w_qkv, swa_sink, swa_w_out):
    b, s_len, d = x.shape
    c_len = ctx.shape[1]
    assert b == 1 and s_len % TM == 0 and c_len % TB == 0 and s_len % c_len == 0
    depth = ada_w.shape[0]
    n_lat = s_len // TM
    n_tot = n_lat + -(-c_len // TM)
    nb_lat = s_len // TB
    nb_real = nb_lat + c_len // TB
    c_rows = (n_tot - n_lat) * TM
    mods = _ada_vectors(c, c_ctx, ada_w, ada_b)
    ng = norm_g.reshape(depth * 3, 1, d)
    w1, w2 = ffn_w_in, ffn_w_out
    gw = _gla_weights(gla_w_in, gla_w_a2, gla_b_a, gla_norm_g, gla_w_out)
    rope = _rope_tables(s_len, c_rows)
    xs = (x[0], jnp.pad(ctx[0], ((0, c_rows - c_len), (0, 0))))
    for i in range(depth):
        m, j = i % N_MIXERS, i // N_MIXERS
        last = i == depth - 1
        n_after = n_lat if last else n_tot
        fin = final_g if last else None
        if m == 0:
            xs = _gla_layer(xs, mods, ng, w1, w2, i, gw, j, n_lat, n_tot, n_after, nb_lat, nb_real, fin)
        elif m == 1:
            xs = _rg_layer(xs, mods, ng, w1, w2, i, rg_w_in[j], rg_conv_w[j], rg_conv_b[j], rg_w_a[j], rg_b_a[j],
                           rg_w_x[j], rg_b_x[j], rg_lam[j], rg_w_out[j], n_lat, n_tot, n_after, nb_lat, nb_real, fin)
        else:
            xs = _swa_layer(xs, mods, ng, w1, w2, i, swa_w_qkv[j], swa_sink[j], swa_w_out[j], rope, n_lat, n_tot,
                            n_after, s_len, c_len, fin)
    return xs[None]
```

```python
import functools
import math

import jax
import jax.numpy as jnp
from jax import lax
from jax.experimental import pallas as pl
from jax.experimental.pallas import tpu as pltpu

F32 = jnp.float32
BF16 = jnp.bfloat16

N_MOD = 9
NORM_EPS = 1e-6
GLA_HEADS = 4
GLA_DK = 128
GLA_DV = 256
GLA_RANK = 16
GLA_LOGIT_NORM = 16.0
GLA_CHUNK = 64
GLA_HK = GLA_HEADS * GLA_DK
GLA_HV = GLA_HEADS * GLA_DV
GLA_LR_PAD = 128
D_RNN = 1536
RG_BLOCKS = 12
RG_BS = D_RNN // RG_BLOCKS
RG_C = 8.0
SWA_HEADS = 16
SWA_KV = 4
SWA_G = SWA_HEADS // SWA_KV
HEAD_DIM = 64
WINDOW = 128
Q_BLOCK = 128
ROT_AXIS = HEAD_DIM // 2
ROPE_BASE = 10000.0
GRID_W = 64
N_MIXERS = 3

TM = 512
TB = 256
FFN_WARM = 16
SUBLANES = 8
LANES = 128
VMEM_LIMIT = 60 * 1024 * 1024
NEG = -1e30
LOG2E = 1.0 / math.log(2.0)


def _odd_pitch(rows):
    tiles = -(-rows // SUBLANES)
    return (tiles + (tiles % 2 == 0)) * SUBLANES


RG_SEG = TB // SUBLANES
RG_HPITCH = _odd_pitch(RG_SEG)
RG_XPITCH = _odd_pitch(RG_SEG + 2 * SUBLANES)


def _params(n_axes=1):
    return pltpu.CompilerParams(dimension_semantics=("arbitrary",) * n_axes, vmem_limit_bytes=VMEM_LIMIT)


def _dot(a, b):
    return jnp.dot(a, b, preferred_element_type=F32)


def _dot_nt(a, b):
    return lax.dot_general(a, b, (((1,), (1,)), ((), ())), preferred_element_type=F32)


def _dot_tn(a, b):
    return lax.dot_general(a, b, (((0,), (0,)), ((), ())), preferred_element_type=F32)


def _rms(x, g):
    return (x * lax.rsqrt(jnp.mean(x * x, axis=-1, keepdims=True) + NORM_EPS)) * g


def _rms_mod(x, g, shift, scale):
    return _rms(x, g) * (1.0 + scale) + shift


def _sigmoid(x):
    return 1.0 / (1.0 + jnp.exp(-x))


def _silu(x):
    xh = 0.5 * x
    return xh + xh * jnp.tanh(xh)


def _softplus(x):
    return jnp.maximum(x, 0.0) + jnp.log1p(jnp.exp(-jnp.abs(x)))


def _gelu_tanh(x):
    return 0.5 * x * (1.0 + jnp.tanh(math.sqrt(2.0 / math.pi) * (x + 0.044715 * (x * x * x))))


def _row(ref, r):
    return ref[r:r + 1, :]


def _tile_spec(width, shift=0):
    return pl.BlockSpec((TM, width), lambda i: (jnp.maximum(i - shift, 0), 0))


def _full_spec(shape):
    nd = len(shape)
    return pl.BlockSpec(shape, lambda i: (0,) * nd, pipeline_mode=pl.Buffered(1))


def _pick_spec(shape, lead):
    tail = shape[len(lead):]
    return pl.BlockSpec((None,) * len(lead) + tuple(tail), lambda i: tuple(lead) + (0,) * len(tail),
                        pipeline_mode=pl.Buffered(1))


def _mod_spec(mods, layer, n_lat, shift=0):
    _, _, n_mod, d = mods.shape
    return pl.BlockSpec((None, None, n_mod, d),
                        lambda i: (layer, jnp.minimum(jnp.maximum(i - shift, 0) // n_lat, 1), 0, 0))


def _ada_kernel(cc_ref, w_ref, b_ref, o_ref):
    act = _silu(cc_ref[...]).astype(BF16)
    o_ref[...] = _dot(act, w_ref[...].astype(BF16)) + b_ref[...]


def _ada_vectors(c, c_ctx, ada_w, ada_b):
    depth, d, nd = ada_w.shape
    cc = jnp.zeros((SUBLANES, d), F32).at[0].set(c[0]).at[1].set(c_ctx)
    wb = 3 * d
    assert nd % wb == 0
    out = pl.pallas_call(
        _ada_kernel,
        grid=(depth, nd // wb),
        in_specs=[pl.BlockSpec((SUBLANES, d), lambda l, j: (0, 0)),
                  pl.BlockSpec((None, d, wb), lambda l, j: (l, 0, j)),
                  pl.BlockSpec((None, 1, wb), lambda l, j: (l, 0, j))],
        out_specs=pl.BlockSpec((None, SUBLANES, wb), lambda l, j: (l, 0, j)),
        out_shape=jax.ShapeDtypeStruct((depth, SUBLANES, nd), F32),
        compiler_params=_params(2),
        name="ada_vectors",
    )(cc, ada_w, ada_b.reshape(depth, 1, nd))
    return out.reshape(depth, SUBLANES, N_MOD, d)


def _swiglu_step(x, mod_ref, r0, g_ref, w1_ref, w2_ref):
    h = _rms_mod(x, g_ref[...], _row(mod_ref, r0), _row(mod_ref, r0 + 1)).astype(BF16)
    gu = _dot(h, w1_ref[...])
    f = w2_ref.shape[0]
    a = (_silu(gu[:, :f]) * gu[:, f:]).astype(BF16)
    return x + (0.5 * _row(mod_ref, r0 + 2)) * _dot(a, w2_ref[...])


def _stage_weights(w1c_ref, w2c_ref, w1_ref, w2_ref):
    i = pl.program_id(0)

    @pl.when(i < FFN_WARM)
    def _():
        for c_ref, s_ref in ((w1c_ref, w1_ref), (w2c_ref, w2_ref)):
            r = c_ref.shape[0]
            s_ref[pl.ds(pl.multiple_of(i * r, r), r), :] = c_ref[...].astype(BF16)


def _ffn_weight_specs(w1, w2, layer, half):
    specs, scratch = [], []
    for w in (w1, w2):
        rows, cols = w.shape[-2:]
        assert rows % (FFN_WARM * 2 * SUBLANES) == 0
        specs.append(pl.BlockSpec((None, None, rows // FFN_WARM, cols),
                                  lambda i: (layer, half, jnp.minimum(i, FFN_WARM - 1), 0)))
        scratch.append(pltpu.VMEM((rows, cols), BF16))
    return specs, scratch


def _pre_kernel(*refs, proj, n_par, n_lat, split_in):
    refs = list(refs)
    w1_ref, w2_ref = refs[-2:]
    refs = refs[:-2]
    x_refs, refs = (refs[:2], refs[2:]) if split_in else (refs[:1], refs[1:])
    mod_ref, g1_ref, w1c_ref, w2c_ref, gm_ref = refs[:5]
    par, x1_ref, outs = refs[5:5 + n_par], refs[5 + n_par], refs[6 + n_par:]
    _stage_weights(w1c_ref, w2c_ref, w1_ref, w2_ref)

    @pl.when(pl.program_id(0) >= FFN_WARM)
    def _():
        if split_in:
            x = jnp.where(pl.program_id(0) - FFN_WARM < n_lat, x_refs[0][...], x_refs[1][...])
        else:
            x = x_refs[0][...]
        x1 = _swiglu_step(x, mod_ref, 0, g1_ref, w1_ref, w2_ref)
        x1_ref[...] = x1
        h = _rms_mod(x1, gm_ref[...], _row(mod_ref, 3), _row(mod_ref, 4)).astype(BF16)
        proj(h, *par, *outs)


def _pre_call(name, proj, xs, mods, norm_g, w1, w2, layer, n_lat, n_tiles, par, par_specs, out_widths, out_dtypes):
    split_in = isinstance(xs, tuple)
    d = (xs[0] if split_in else xs).shape[1]
    tile = lambda i: jnp.maximum(i - FFN_WARM, 0)
    if split_in:
        x_specs = [pl.BlockSpec((TM, d), lambda i: (jnp.minimum(tile(i), n_lat - 1), 0)),
                   pl.BlockSpec((TM, d), lambda i: (jnp.maximum(tile(i) - n_lat, 0), 0))]
        x_args = list(xs)
    else:
        x_specs, x_args = [_tile_spec(d, FFN_WARM)], [xs]
    t = n_tiles * TM
    w_specs, w_scratch = _ffn_weight_specs(w1, w2, layer, 0)
    return pl.pallas_call(
        functools.partial(_pre_kernel, proj=proj, n_par=len(par), n_lat=n_lat, split_in=split_in),
        grid=(FFN_WARM + n_tiles,),
        in_specs=x_specs + [_mod_spec(mods, layer, n_lat, FFN_WARM), _pick_spec(norm_g.shape, (3 * layer,))] + w_specs
        + [_pick_spec(norm_g.shape, (3 * layer + 1,))] + par_specs,
        out_specs=[_tile_spec(d, FFN_WARM)] + [_tile_spec(w, FFN_WARM) for w in out_widths],
        out_shape=[jax.ShapeDtypeStruct((t, d), F32)]
        + [jax.ShapeDtypeStruct((t, w), dt) for w, dt in zip(out_widths, out_dtypes)],
        scratch_shapes=w_scratch,
        compiler_params=_params(),
        name=name,
    )(*x_args, mods, norm_g, w1, w2, norm_g, *par)


def _post_kernel(*refs, head, n_head, final):
    x_ref, mod_ref = refs[:2]
    head_refs = refs[2:2 + n_head]
    g2_ref, w1c_ref, w2c_ref = refs[2 + n_head:5 + n_head]
    o_ref, w1_ref, w2_ref = refs[-3:]
    _stage_weights(w1c_ref, w2c_ref, w1_ref, w2_ref)

    @pl.when(pl.program_id(0) >= FFN_WARM)
    def _():
        y = head(*head_refs[:-1])
        x2 = x_ref[...] + _row(mod_ref, 5) * _dot(y, head_refs[-1][...])
        x3 = _swiglu_step(x2, mod_ref, 6, g2_ref, w1_ref, w2_ref)
        o_ref[...] = _rms(x3, refs[5 + n_head][...]) if final else x3


def _post_call(name, head, xs, mods, norm_g, w1, w2, layer, n_lat, n_tiles, head_args, head_specs, final_g):
    d = xs.shape[1]
    fin_specs, fin_args = ([_full_spec((1, d))], [final_g.reshape(1, d)]) if final_g is not None else ([], [])
    w_specs, w_scratch = _ffn_weight_specs(w1, w2, layer, 1)
    return pl.pallas_call(
        functools.partial(_post_kernel, head=head, n_head=len(head_args), final=final_g is not None),
        grid=(FFN_WARM + n_tiles,),
        in_specs=[_tile_spec(d, FFN_WARM), _mod_spec(mods, layer, n_lat, FFN_WARM)] + head_specs
        + [_pick_spec(norm_g.shape, (3 * layer + 2,))] + w_specs + fin_specs,
        out_specs=_tile_spec(d, FFN_WARM),
        out_shape=jax.ShapeDtypeStruct((n_tiles * TM, d), F32),
        scratch_shapes=w_scratch,
        compiler_params=_params(),
        name=name,
    )(xs, mods, *head_args, norm_g, w1, w2, *fin_args)


def _gla_proj(h, w_ref, w2_ref, ba_ref, q_ref, k_ref, v_ref, og_ref, gg_ref):
    y = _dot(h, w_ref[...])
    q_ref[...] = (y[:, :GLA_HK] * (GLA_DK ** -0.5)).astype(BF16)
    k_ref[...] = y[:, GLA_HK:2 * GLA_HK].astype(BF16)
    v_ref[...] = y[:, 2 * GLA_HK:2 * GLA_HK + GLA_HV].astype(BF16)
    og_ref[...] = y[:, 2 * GLA_HK + GLA_HV:2 * GLA_HK + 2 * GLA_HV].astype(BF16)
    lr = y[:, 2 * GLA_HK + 2 * GLA_HV:].astype(BF16)
    logit = _dot(lr, w2_ref[...]) + ba_ref[...]
    gg_ref[...] = (-_softplus(-logit)) * (LOG2E / GLA_LOGIT_NORM)


def _gla_tile(q_ref, k_ref, v_ref, g_ref, o_ref, s_ref, rev):
    c = GLA_CHUNK
    tm = q_ref.shape[0]
    nc = tm // c
    r_i = lax.broadcasted_iota(jnp.int32, (tm, tm), 0)
    c_i = lax.broadcasted_iota(jnp.int32, (tm, tm), 1)
    same = (r_i // c) == (c_i // c)
    tri = jnp.logical_and(same, (c_i >= r_i) if rev else (c_i <= r_i))
    tri_bf = tri.astype(BF16)
    g = g_ref[...]
    g_hi = g.astype(BF16)
    g_lo = (g - g_hi.astype(F32)).astype(BF16)
    b = _dot(tri_bf, g_hi) + _dot(tri_bf, g_lo)
    ends = [b[n * c:n * c + 1, :] if rev else b[(n + 1) * c - 1:(n + 1) * c, :] for n in range(nc)]
    btot = jnp.concatenate([jnp.broadcast_to(e, (c, e.shape[1])) for e in ends], axis=0)
    k = k_ref[...].astype(F32)
    qe = (q_ref[...].astype(F32) * jnp.exp2(b)).astype(BF16)
    ke = (k * jnp.exp2(-b)).astype(BF16)
    kd = (k * jnp.exp2(btot - b)).astype(BF16)
    decs = [jnp.exp2(e) for e in ends]
    own = (lax.broadcasted_iota(jnp.int32, (tm, nc * GLA_DK), 0) // c
           == lax.broadcasted_iota(jnp.int32, (tm, nc * GLA_DK), 1) // GLA_DK)
    for h in range(GLA_HEADS):
        ks = slice(h * GLA_DK, (h + 1) * GLA_DK)
        vs = slice(h * GLA_DV, (h + 1) * GLA_DV)
        vh = v_ref[:, vs]
        att = jnp.where(tri, _dot_nt(qe[:, ks], ke[:, ks]), 0.0).astype(BF16)
        o_in = _dot(att, vh)
        kd_bd = jnp.where(own, jnp.concatenate([kd[:, ks]] * nc, axis=1), jnp.zeros((), BF16))
        upd = _dot_tn(vh, kd_bd)
        st = s_ref[h]
        for n in (reversed(range(nc)) if rev else range(nc)):
            rows = slice(n * c, (n + 1) * c)
            o_ref[rows, vs] = (o_in[rows, :] + _dot_nt(qe[rows, ks], st.astype(BF16))).astype(o_ref.dtype)
            st = decs[n][:, ks] * st + upd[:, n * GLA_DK:(n + 1) * GLA_DK]
        s_ref[h] = st


def _gla_kernel(qf_ref, kf_ref, vf_ref, gf_ref, qb_ref, kb_ref, vb_ref, gb_ref, of_ref, ob_ref, sf_ref, sb_ref):
    @pl.when(pl.program_id(0) == 0)
    def _():
        sf_ref[...] = jnp.zeros_like(sf_ref)
        sb_ref[...] = jnp.zeros_like(sb_ref)

    _gla_tile(qf_ref, kf_ref, vf_ref, gf_ref, of_ref, sf_ref, False)
    _gla_tile(qb_ref, kb_ref, vb_ref, gb_ref, ob_ref, sb_ref, True)


def _block_order(s, nb_lat, nb_real, rev):
    real = (nb_real - 1 - s) if rev else (s + nb_lat) % nb_real
    return jnp.where(s < nb_real, real, s)


def _gla_scan(q, k, v, gg, nb_lat, nb_real):
    t = q.shape[0]
    fwd = lambda s: _block_order(s, nb_lat, nb_real, False)
    bwd = lambda s: _block_order(s, nb_lat, nb_real, True)

    def specs(blk, col):
        return [pl.BlockSpec((TB, GLA_HK), lambda s: (blk(s), 0)), pl.BlockSpec((TB, GLA_HK), lambda s: (blk(s), 0)),
                pl.BlockSpec((TB, GLA_HV), lambda s: (blk(s), 0)), pl.BlockSpec((TB, GLA_HK), lambda s: (blk(s), col))]

    return pl.pallas_call(
        _gla_kernel,
        grid=(t // TB,),
        in_specs=specs(fwd, 0) + specs(bwd, 1),
        out_specs=[pl.BlockSpec((TB, GLA_HV), lambda s: (fwd(s), 0)), pl.BlockSpec((TB, GLA_HV), lambda s: (bwd(s), 0))],
        out_shape=[jax.ShapeDtypeStruct((t, GLA_HV), BF16)] * 2,
        scratch_shapes=[pltpu.VMEM((GLA_HEADS, GLA_DV, GLA_DK), F32)] * 2,
        compiler_params=_params(),
        name="gla_scan",
    )(q, k, v, gg, q, k, v, gg)


def _gla_out(of_ref, ob_ref, og_ref, ng_ref):
    o = of_ref[...].astype(F32) + ob_ref[...].astype(F32)
    ng = ng_ref[...]
    parts = [_rms(o[:, h * GLA_DV:(h + 1) * GLA_DV], ng) for h in range(GLA_HEADS)]
    return (jnp.concatenate(parts, axis=1) * _silu(og_ref[...].astype(F32))).astype(BF16)


def _gla_weights(w_in, w_a2, b_a, gla_norm_g, w_out):
    na, d, n_in = w_in.shape
    n_main = 2 * GLA_HK + 2 * GLA_HV
    w = jnp.pad(w_in.astype(BF16), ((0, 0), (0, 0), (0, n_main + GLA_LR_PAD - n_in)))
    w2 = jnp.zeros((na, GLA_LR_PAD, 2 * GLA_HK), F32)
    w2 = w2.at[:, :GLA_RANK, :GLA_HK].set(w_a2[:, 0]).at[:, GLA_RANK:2 * GLA_RANK, GLA_HK:].set(w_a2[:, 1])
    return (w, w2.astype(BF16), b_a.reshape(na, 1, 2 * GLA_HK), gla_norm_g.reshape(na, 1, GLA_DV), w_out.astype(BF16))


def _gla_layer(xs, mods, norm_g, w1, w2, layer, gw, j, n_lat, n_tot, n_out_tiles, nb_lat, nb_real, final_g):
    w, wlr, ba, ng, w_out = gw
    x1, q, k, v, og, gg = _pre_call(
        "pre_gla", _gla_proj, xs, mods, norm_g, w1, w2, layer, n_lat, n_tot, [w, wlr, ba],
        [_pick_spec(w.shape, (j,)), _pick_spec(wlr.shape, (j,)), _pick_spec(ba.shape, (j,))],
        [GLA_HK, GLA_HK, GLA_HV, GLA_HV, 2 * GLA_HK], [BF16, BF16, BF16, BF16, F32])
    o_f, o_b = _gla_scan(q, k, v, gg, nb_lat, nb_real)
    return _post_call(
        "post_gla", _gla_out, x1, mods, norm_g, w1, w2, layer, n_lat, n_out_tiles, [o_f, o_b, og, ng, w_out],
        [_tile_spec(GLA_HV, FFN_WARM)] * 3 + [_pick_spec(ng.shape, (j,)), _pick_spec(w_out.shape, (j,))], final_g)


def _rg_proj(h, w_ref, gate_ref, xr_ref):
    y = _dot(h, w_ref[...])
    gate_ref[...] = _gelu_tanh(y[:, :D_RNN]).astype(BF16)
    xr_ref[...] = y[:, D_RNN:]


def _rg_scan_kernel(xm_ref, xp_ref, xn_ref, cw_ref, cb_ref, w_ref, ba_ref, bx_ref, lam_ref, h_ref,
                    xs_ref, a_ref, u_ref, xp_pad_ref, hp_pad_ref, carry_ref, *, rev, nb_lat, nb_real):
    s = pl.program_id(0)
    blk = _block_order(s, nb_lat, nb_real, rev)

    @pl.when(s == 0)
    def _():
        carry_ref[...] = jnp.zeros_like(carry_ref)

    tm = xm_ref.shape[0]
    hal = SUBLANES
    seg = tm // SUBLANES
    nt = a_ref.shape[0]
    assert seg == RG_SEG and RG_BS == LANES
    has_prev = jnp.logical_and(blk != 0, jnp.logical_and(blk != nb_lat, blk != nb_real))
    has_next = jnp.logical_and(blk != nb_lat - 1, jnp.logical_and(blk != nb_real - 1, blk != pl.num_programs(0) - 1))
    xs_ref[0:hal, :] = jnp.where(has_prev, xp_ref[...], 0.0)
    xs_ref[hal:hal + tm, :] = xm_ref[...]
    xs_ref[hal + tm:, :] = jnp.where(has_next, xn_ref[...], 0.0)
    czh = (-0.5 * RG_C) * _softplus(-lam_ref[...])
    ext = seg + 2 * hal
    for n in range(nt):
        ns = slice(n * LANES, (n + 1) * LANES)
        for sg in range(SUBLANES):
            xp_pad_ref[n, sg * RG_XPITCH:sg * RG_XPITCH + ext, :] = xs_ref[sg * seg:sg * seg + ext, ns]
        xe = [xp_pad_ref[n, pl.ds(k, SUBLANES, stride=RG_XPITCH), :] for k in range(hal - 2, hal + seg + 1)]
        cw = [jnp.broadcast_to(cw_ref[j:j + 1, ns], (SUBLANES, LANES)) for j in range(4)]
        cb = jnp.broadcast_to(cb_ref[:, ns], (SUBLANES, LANES))
        xc = jnp.concatenate(
            [cb + cw[0] * xe[i] + cw[1] * xe[i + 1] + cw[2] * xe[i + 2] + cw[3] * xe[i + 3] for i in range(seg)],
            axis=0)
        z = _dot(xc.astype(BF16), w_ref[n])
        t_r = jnp.tanh(z[:, :RG_BS] + ba_ref[:, ns])
        t_i = jnp.tanh(z[:, RG_BS:] + bx_ref[:, ns])
        a = jnp.exp(czh[:, ns] + czh[:, ns] * t_r)
        a_ref[n] = a
        y = 1.0 - a * a
        xch = 0.5 * xc
        u_ref[n] = jnp.where(y > 0.0, y * lax.rsqrt(y), 0.0) * (xch + xch * t_i)

    def vreg(i):
        return pl.ds(i * SUBLANES, SUBLANES)

    def local(it, carry):
        i = (seg - 1 - it) if rev else it
        out = []
        for n in range(nt):
            a = a_ref[n, vreg(i), :]
            out += [a * carry[2 * n] + u_ref[n, vreg(i), :], a * carry[2 * n + 1]]
        return tuple(out)

    ends = (jnp.zeros((SUBLANES, LANES), F32), jnp.ones((SUBLANES, LANES), F32)) * nt
    for it in range(seg):
        ends = local(it, ends)
    cmats = []
    for n in range(nt):
        ns = slice(n * LANES, (n + 1) * LANES)
        hl_end, p_end = ends[2 * n], ends[2 * n + 1]
        c = carry_ref[:, ns]
        cin = [None] * SUBLANES
        for sg in (reversed(range(SUBLANES)) if rev else range(SUBLANES)):
            cin[sg] = c
            c = hl_end[sg:sg + 1, :] + p_end[sg:sg + 1, :] * c
        carry_ref[:, ns] = c
        cmats.append(jnp.concatenate(cin, axis=0))

    def final(it, carry):
        i = (seg - 1 - it) if rev else it
        out = []
        for n in range(nt):
            h = a_ref[n, vreg(i), :] * carry[n] + u_ref[n, vreg(i), :]
            hp_pad_ref[n, pl.ds(i, SUBLANES, stride=RG_HPITCH), :] = h
            out.append(h)
        return tuple(out)

    hs = tuple(cmats)
    for it in range(seg):
        hs = final(it, hs)
    for n in range(nt):
        for sg in range(SUBLANES):
            h_ref[sg * seg:(sg + 1) * seg, n * LANES:(n + 1) * LANES] = (
                hp_pad_ref[n, sg * RG_HPITCH:sg * RG_HPITCH + seg, :].astype(h_ref.dtype))


def _rg_scan(xr, cw, cb, w, ba, bx, lam, z, nb_lat, nb_real):
    t, dr = xr.shape
    per = TB // SUBLANES
    n8 = t // SUBLANES
    rev = z == 1
    blk = lambda s: _block_order(s, nb_lat, nb_real, rev)
    return pl.pallas_call(
        functools.partial(_rg_scan_kernel, rev=rev, nb_lat=nb_lat, nb_real=nb_real),
        grid=(t // TB,),
        in_specs=[pl.BlockSpec((TB, dr), lambda s: (blk(s), 0)),
                  pl.BlockSpec((SUBLANES, dr), lambda s: (jnp.maximum(blk(s) * per - 1, 0), 0)),
                  pl.BlockSpec((SUBLANES, dr), lambda s: (jnp.minimum((blk(s) + 1) * per, n8 - 1), 0)),
                  _full_spec(cw.shape), _full_spec(cb.shape), _pick_spec(w.shape, (z,)),
                  _pick_spec(ba.shape, (z,)), _pick_spec(bx.shape, (z,)), _pick_spec(lam.shape, (z,))],
        out_specs=pl.BlockSpec((TB, dr), lambda s: (blk(s), 0)),
        out_shape=jax.ShapeDtypeStruct((t, dr), BF16),
        scratch_shapes=[pltpu.VMEM((TB + 2 * SUBLANES, dr), F32),
                        pltpu.VMEM((dr // LANES, TB, LANES), F32), pltpu.VMEM((dr // LANES, TB, LANES), F32),
                        pltpu.VMEM((dr // LANES, SUBLANES * RG_XPITCH, LANES), F32),
                        pltpu.VMEM((dr // LANES, SUBLANES * RG_HPITCH, LANES), F32),
                        pltpu.VMEM((1, dr), F32)],
        compiler_params=_params(),
        name="rg_scan_bwd" if rev else "rg_scan_fwd",
    )(xr, xr, xr, cw, cb, w, ba, bx, lam)


def _rg_out(hf_ref, hb_ref, gate_ref):
    return ((hf_ref[...].astype(F32) + hb_ref[...].astype(F32)) * gate_ref[...].astype(F32)).astype(BF16)


def _rg_layer(xs, mods, norm_g, w1, w2, layer, w_in, conv_w, conv_b, w_a, b_a, w_x, b_x, lam, w_out, n_lat, n_tot,
              n_out_tiles, nb_lat, nb_real, final_g):
    w_in = w_in.astype(BF16)
    x1, gate, xr = _pre_call("pre_rg", _rg_proj, xs, mods, norm_g, w1, w2, layer, n_lat, n_tot, [w_in],
                             [_full_spec(w_in.shape)], [D_RNN, D_RNN], [BF16, F32])
    w = (0.5 * jnp.concatenate([w_a, w_x], axis=-1)).astype(BF16)
    hs = [_rg_scan(xr, conv_w, conv_b.reshape(1, D_RNN), w, (0.5 * b_a).reshape(2, 1, D_RNN),
                   (0.5 * b_x).reshape(2, 1, D_RNN), lam.reshape(2, 1, D_RNN), z, nb_lat, nb_real) for z in range(2)]
    w_out = w_out.astype(BF16)
    return _post_call("post_rg", _rg_out, x1, mods, norm_g, w1, w2, layer, n_lat, n_out_tiles,
                      [hs[0], hs[1], gate, w_out], [_tile_spec(D_RNN, FFN_WARM)] * 3 + [_full_spec(w_out.shape)],
                      final_g)


def _swa_proj(h, w_ref, cos_ref, sin_ref, q_ref, kz_ref, vz_ref):
    y = _dot(h, w_ref[...])
    nq = SWA_HEADS * HEAD_DIM
    nk = SWA_KV * HEAD_DIM
    cos = cos_ref[...]
    sin = sin_ref[...]
    rot = ROT_AXIS // 2
    lane = lax.broadcasted_iota(jnp.int32, (1, LANES), 1)
    first = (lane % ROT_AXIS) < rot

    def rope(xg):
        partner = jnp.where(first, pltpu.roll(xg, LANES - rot, axis=1), pltpu.roll(xg, rot, axis=1))
        return xg * cos + partner * sin

    q = y[:, :nq] * (HEAD_DIM ** -0.5 * LOG2E)
    for gi in range(nq // LANES):
        q_ref[:, gi * LANES:(gi + 1) * LANES] = rope(q[:, gi * LANES:(gi + 1) * LANES]).astype(BF16)
    low = lane < HEAD_DIM
    for gi in range(nk // LANES):
        kg = rope(y[:, nq + gi * LANES:nq + (gi + 1) * LANES])
        vg = y[:, nq + nk + gi * LANES:nq + nk + (gi + 1) * LANES]
        for par_j in range(2):
            j = 2 * gi + par_j
            for par_q in range(2):
                src_k = kg if par_q == par_j else pltpu.roll(kg, HEAD_DIM, axis=1)
                src_v = vg if par_q == par_j else pltpu.roll(vg, HEAD_DIM, axis=1)
                keep = low if par_q == 0 else jnp.logical_not(low)
                cs = slice((2 * j + par_q) * LANES, (2 * j + par_q + 1) * LANES)
                kz_ref[:, cs] = jnp.where(keep, src_k, 0.0).astype(BF16)
                vz_ref[:, cs] = jnp.where(keep, src_v, 1.0).astype(BF16)


def _swa_kernel(sink_ref, q_ref, kp_ref, k0_ref, kn_ref, vp_ref, v0_ref, vn_ref, kc_ref, vc_ref, o_ref,
                kz_ref, vz_ref, *, nb):
    qb = q_ref.shape[0]
    c_len = kc_ref.shape[0]
    i = pl.program_id(0)

    @pl.when(i == 0)
    def _():
        kz_ref[0:c_len, :] = kc_ref[...]
        vz_ref[0:c_len, :] = vc_ref[...]

    row = lax.broadcasted_iota(jnp.int32, (2 * qb, 1), 0)
    low = lax.broadcasted_iota(jnp.int32, (1, LANES), 1) < HEAD_DIM

    def attend(n_keys, mask):
        for j in range(SWA_KV):
            qs = jnp.concatenate([q_ref[:, (2 * j) * LANES:(2 * j + 1) * LANES],
                                  q_ref[:, (2 * j + 1) * LANES:(2 * j + 2) * LANES]], axis=0)
            lo = slice((2 * j) * LANES, (2 * j + 1) * LANES)
            hi = slice((2 * j + 1) * LANES, (2 * j + 2) * LANES)
            s_all = _dot_nt(qs, jnp.concatenate([kz_ref[0:n_keys, lo], kz_ref[0:n_keys, hi]], axis=0))
            ps, sinks, ms = [], [], []
            for par in range(2):
                s = mask(s_all[:, par * n_keys:(par + 1) * n_keys])
                sink = LOG2E * jnp.where(row < qb, sink_ref[j * SWA_G + par], sink_ref[j * SWA_G + 2 + par])
                m = jnp.maximum(jnp.max(s, axis=-1, keepdims=True), sink)
                ps.append(jnp.exp2(s - m).astype(BF16))
                sinks.append(sink)
                ms.append(m)
            r_all = _dot(jnp.concatenate(ps, axis=0), vz_ref[0:n_keys, (2 * j) * LANES:(2 * j + 2) * LANES])
            halves = []
            for par in range(2):
                r = r_all[par * 2 * qb:(par + 1) * 2 * qb, par * LANES:(par + 1) * LANES]
                halves.append(r / (pltpu.roll(r, HEAD_DIM, axis=1) + jnp.exp2(sinks[par] - ms[par])))
            o = jnp.where(low, halves[0], halves[1]).astype(BF16)
            o_ref[:, (2 * j) * LANES:(2 * j + 1) * LANES] = o[0:qb]
            o_ref[:, (2 * j + 1) * LANES:(2 * j + 2) * LANES] = o[qb:]

    @pl.when(i < nb)
    def _():
        for b, (k_ref, v_ref) in enumerate(((kp_ref, vp_ref), (k0_ref, v0_ref), (kn_ref, vn_ref))):
            kz_ref[c_len + b * qb:c_len + (b + 1) * qb, :] = k_ref[...]
            vz_ref[c_len + b * qb:c_len + (b + 1) * qb, :] = v_ref[...]
        r_i = lax.broadcasted_iota(jnp.int32, (2 * qb, qb), 0) % qb
        c_i = lax.broadcasted_iota(jnp.int32, (2 * qb, qb), 1)
        b_prev = jnp.where(jnp.logical_and(c_i >= r_i, i > 0), 0.0, NEG)
        b_next = jnp.where(jnp.logical_and(c_i <= r_i, i < nb - 1), 0.0, NEG)

        def mask(s):
            return jnp.concatenate([s[:, :c_len], s[:, c_len:c_len + qb] + b_prev,
                                    s[:, c_len + qb:c_len + 2 * qb], s[:, c_len + 2 * qb:] + b_next], axis=1)

        attend(c_len + 3 * qb, mask)

    @pl.when(i >= nb)
    def _():
        attend(c_len, lambda s: s)


def _swa_attn(q, kz, vz, sink, s_len, c_len):
    t, nq = q.shape
    nz = kz.shape[1]
    nb = s_len // Q_BLOCK
    n_blk = t // Q_BLOCK
    smem = pl.BlockSpec(memory_space=pltpu.SMEM)
    ctx_spec = pl.BlockSpec((c_len, nz), lambda i: (s_len // c_len, 0))
    band = [pl.BlockSpec((Q_BLOCK, nz), lambda i: (jnp.maximum(i - 1, 0), 0)),
            pl.BlockSpec((Q_BLOCK, nz), lambda i: (i, 0)),
            pl.BlockSpec((Q_BLOCK, nz), lambda i: (jnp.minimum(i + 1, n_blk - 1), 0))]
    n_keys = c_len + 3 * Q_BLOCK
    return pl.pallas_call(
        functools.partial(_swa_kernel, nb=nb),
        grid=(n_blk,),
        in_specs=[smem, pl.BlockSpec((Q_BLOCK, nq), lambda i: (i, 0))] + band + band + [ctx_spec, ctx_spec],
        out_specs=pl.BlockSpec((Q_BLOCK, nq), lambda i: (i, 0)),
        out_shape=jax.ShapeDtypeStruct((t, nq), BF16),
        scratch_shapes=[pltpu.VMEM((n_keys, nz), BF16), pltpu.VMEM((n_keys, nz), BF16)],
        compiler_params=_params(),
        name="swa_attn",
    )(sink, q, kz, kz, kz, vz, vz, vz, kz, vz)


def _rope_tables(s_len, c_len):
    rows = s_len // GRID_W
    freqs = ROPE_BASE ** (-jnp.arange(0, ROT_AXIS, 2, dtype=F32) / ROT_AXIS)
    dim = jnp.arange(LANES) % HEAD_DIM
    by_row = (dim // ROT_AXIS) == 0
    lane_freq = freqs[dim % (ROT_AXIS // 2)]
    sign = jnp.where((dim % ROT_AXIS) < ROT_AXIS // 2, -1.0, 1.0)
    ang_r = jnp.arange(rows, dtype=F32)[:, None] * lane_freq
    ang_c = jnp.arange(GRID_W, dtype=F32)[:, None] * lane_freq

    def table(fn, scale, ctx_fill):
        lat = jnp.where(by_row, (fn(ang_r) * scale)[:, None, :], (fn(ang_c) * scale)[None, :, :])
        return jnp.concatenate([lat.reshape(s_len, LANES), jnp.full((c_len, LANES), ctx_fill, F32)], axis=0)

    return table(jnp.cos, 1.0, 1.0), table(jnp.sin, sign, 0.0)


def _swa_out(o_ref):
    return o_ref[...]


def _swa_layer(xs, mods, norm_g, w1, w2, layer, w_qkv, sink, w_out, rope, n_lat, n_tot, n_out_tiles, s_len, c_len,
               final_g):
    w_qkv = w_qkv.astype(BF16)
    nq = SWA_HEADS * HEAD_DIM
    nz = 2 * SWA_KV * LANES
    x1, q, kz, vz = _pre_call("pre_swa", _swa_proj, xs, mods, norm_g, w1, w2, layer, n_lat, n_tot,
                              [w_qkv, rope[0], rope[1]],
                              [_full_spec(w_qkv.shape), _tile_spec(LANES, FFN_WARM), _tile_spec(LANES, FFN_WARM)],
                              [nq, nz, nz], [BF16, BF16, BF16])
    o = _swa_attn(q, kz, vz, sink, s_len, c_len)
    w_out = w_out.astype(BF16)
    return _post_call("post_swa", _swa_out, x1, mods, norm_g, w1, w2, layer, n_lat, n_out_tiles, [o, w_out],
                      [_tile_spec(nq, FFN_WARM), _full_spec(w_out.shape)], final_g)


def kernel(x, c, ctx, c_ctx, ada_w, ada_b, norm_g, final_g, ffn_w_in, ffn_w_out, gla_w_in, gla_w_a2, gla_b_a, gla_norm_g, gla_w_out, rg_w_in, rg_conv_w, rg_conv_b, rg_w_a, rg_b_a, rg_w_x, rg_b_x, rg_lam, rg_w_out, swa_w_qkv, swa_sink, swa_w_out):
    b, s_len, d = x.shape
    c_len = ctx.shape[1]
    assert b == 1 and s_len % TM == 0 and c_len % TB == 0 and s_len % c_len == 0
    depth = ada_w.shape[0]
    n_lat = s_len // TM
    n_tot = n_lat + -(-c_len // TM)
    nb_lat = s_len // TB
    nb_real = nb_lat + c_len // TB
    c_rows = (n_tot - n_lat) * TM
    mods = _ada_vectors(c, c_ctx, ada_w, ada_b)
    ng = norm_g.reshape(depth * 3, 1, d)
    w1, w2 = ffn_w_in, ffn_w_out
    gw = _gla_weights(gla_w_in, gla_w_a2, gla_b_a, gla_norm_g, gla_w_out)
    rope = _rope_tables(s_len, c_rows)
    xs = (x[0], jnp.pad(ctx[0], ((0, c_rows - c_len), (0, 0))))
    for i in range(depth):
        m, j = i % N_MIXERS, i // N_MIXERS
        last = i == depth - 1
        n_after = n_lat if last else n_tot
        fin = final_g if last else None
        if m == 0:
            xs = _gla_layer(xs, mods, ng, w1, w2, i, gw, j, n_lat, n_tot, n_after, nb_lat, nb_real, fin)
        elif m == 1:
            xs = _rg_layer(xs, mods, ng, w1, w2, i, rg_w_in[j], rg_conv_w[j], rg_conv_b[j], rg_w_a[j], rg_b_a[j],
                           rg_w_x[j], rg_b_x[j], rg_lam[j], rg_w_out[j], n_lat, n_tot, n_after, nb_lat, nb_real, fin)
        else:
            xs = _swa_layer(xs, mods, ng, w1, w2, i, swa_w_qkv[j], swa_sink[j], swa_w_out[j], rope, n_lat, n_tot,
                            n_after, s_len, c_len, fin)
    return xs[None]
```

```python
import functools
import math

import jax
import jax.numpy as jnp
from jax import lax
from jax.experimental import pallas as pl
from jax.experimental.pallas import tpu as pltpu

F32 = jnp.float32
BF16 = jnp.bfloat16

N_MOD = 9
NORM_EPS = 1e-6
GLA_HEADS = 4
GLA_DK = 128
GLA_DV = 256
GLA_RANK = 16
GLA_LOGIT_NORM = 16.0
GLA_CHUNK = 64
GLA_HK = GLA_HEADS * GLA_DK
GLA_HV = GLA_HEADS * GLA_DV
GLA_LR_PAD = 128
D_RNN = 1536
RG_BLOCKS = 12
RG_BS = D_RNN // RG_BLOCKS
RG_C = 8.0
SWA_HEADS = 16
SWA_KV = 4
SWA_G = SWA_HEADS // SWA_KV
HEAD_DIM = 64
WINDOW = 128
Q_BLOCK = 128
ROT_AXIS = HEAD_DIM // 2
ROPE_BASE = 10000.0
GRID_W = 64
N_MIXERS = 3

TM = 512
TB = 256
FFN_WARM = 16
SUBLANES = 8
LANES = 128
VMEM_LIMIT = 60 * 1024 * 1024
NEG = -1e30
LOG2E = 1.0 / math.log(2.0)


def _odd_pitch(rows):
    tiles = -(-rows // SUBLANES)
    return (tiles + (tiles % 2 == 0)) * SUBLANES


RG_SEG = TB // SUBLANES
RG_HPITCH = _odd_pitch(RG_SEG)
RG_XPITCH = _odd_pitch(RG_SEG + 2 * SUBLANES)


def _params(n_axes=1):
    return pltpu.CompilerParams(dimension_semantics=("arbitrary",) * n_axes, vmem_limit_bytes=VMEM_LIMIT)


def _dot(a, b):
    return jnp.dot(a, b, preferred_element_type=F32)


def _dot_nt(a, b):
    return lax.dot_general(a, b, (((1,), (1,)), ((), ())), preferred_element_type=F32)


def _dot_tn(a, b):
    return lax.dot_general(a, b, (((0,), (0,)), ((), ())), preferred_element_type=F32)


def _rms(x, g):
    return (x * lax.rsqrt(jnp.mean(x * x, axis=-1, keepdims=True) + NORM_EPS)) * g


def _rms_mod(x, g, shift, scale):
    return _rms(x, g) * (1.0 + scale) + shift


def _sigmoid(x):
    return 1.0 / (1.0 + jnp.exp(-x))


def _silu(x):
    xh = 0.5 * x
    return xh + xh * jnp.tanh(xh)


def _softplus(x):
    return jnp.maximum(x, 0.0) + jnp.log(1.0 + jnp.exp(-jnp.abs(x)))


def _gelu_tanh(x):
    return 0.5 * x * (1.0 + jnp.tanh(math.sqrt(2.0 / math.pi) * (x + 0.044715 * (x * x * x))))


def _row(ref, r):
    return ref[r:r + 1, :]


def _tile_spec(width, shift=0):
    return pl.BlockSpec((TM, width), lambda i: (jnp.maximum(i - shift, 0), 0))


def _full_spec(shape):
    nd = len(shape)
    return pl.BlockSpec(shape, lambda i: (0,) * nd, pipeline_mode=pl.Buffered(1))


def _pick_spec(shape, lead):
    tail = shape[len(lead):]
    return pl.BlockSpec((None,) * len(lead) + tuple(tail), lambda i: tuple(lead) + (0,) * len(tail),
                        pipeline_mode=pl.Buffered(1))


def _mod_spec(mods, layer, n_lat, shift=0):
    _, _, n_mod, d = mods.shape
    return pl.BlockSpec((None, None, n_mod, d),
                        lambda i: (layer, jnp.minimum(jnp.maximum(i - shift, 0) // n_lat, 1), 0, 0))


def _ada_kernel(cc_ref, w_ref, b_ref, o_ref):
    act = _silu(cc_ref[...]).astype(BF16)
    o_ref[...] = _dot(act, w_ref[...].astype(BF16)) + b_ref[...]


def _ada_vectors(c, c_ctx, ada_w, ada_b):
    depth, d, nd = ada_w.shape
    cc = jnp.zeros((SUBLANES, d), F32).at[0].set(c[0]).at[1].set(c_ctx)
    wb = 3 * d
    assert nd % wb == 0
    out = pl.pallas_call(
        _ada_kernel,
        grid=(depth, nd // wb),
        in_specs=[pl.BlockSpec((SUBLANES, d), lambda l, j: (0, 0)),
                  pl.BlockSpec((None, d, wb), lambda l, j: (l, 0, j)),
                  pl.BlockSpec((None, 1, wb), lambda l, j: (l, 0, j))],
        out_specs=pl.BlockSpec((None, SUBLANES, wb), lambda l, j: (l, 0, j)),
        out_shape=jax.ShapeDtypeStruct((depth, SUBLANES, nd), F32),
        compiler_params=_params(2),
        name="ada_vectors",
    )(cc, ada_w, ada_b.reshape(depth, 1, nd))
    return out.reshape(depth, SUBLANES, N_MOD, d)


def _swiglu_step(x, mod_ref, r0, g_ref, w1_ref, w2_ref):
    h = _rms_mod(x, g_ref[...], _row(mod_ref, r0), _row(mod_ref, r0 + 1)).astype(BF16)
    gu = _dot(h, w1_ref[...])
    f = w2_ref.shape[0]
    a = (_silu(gu[:, :f]) * gu[:, f:]).astype(BF16)
    return x + (0.5 * _row(mod_ref, r0 + 2)) * _dot(a, w2_ref[...])


def _stage_weights(w1c_ref, w2c_ref, w1_ref, w2_ref):
    i = pl.program_id(0)

    @pl.when(i < FFN_WARM)
    def _():
        for c_ref, s_ref in ((w1c_ref, w1_ref), (w2c_ref, w2_ref)):
            r = c_ref.shape[0]
            s_ref[pl.ds(pl.multiple_of(i * r, r), r), :] = c_ref[...].astype(BF16)


def _ffn_weight_specs(w1, w2, layer, half):
    specs, scratch = [], []
    for w in (w1, w2):
        rows, cols = w.shape[-2:]
        assert rows % (FFN_WARM * 2 * SUBLANES) == 0
        specs.append(pl.BlockSpec((None, None, rows // FFN_WARM, cols),
                                  lambda i: (layer, half, jnp.minimum(i, FFN_WARM - 1), 0)))
        scratch.append(pltpu.VMEM((rows, cols), BF16))
    return specs, scratch


def _pre_kernel(*refs, proj, n_par, n_lat, split_in):
    refs = list(refs)
    w1_ref, w2_ref = refs[-2:]
    refs = refs[:-2]
    x_refs, refs = (refs[:2], refs[2:]) if split_in else (refs[:1], refs[1:])
    mod_ref, g1_ref, w1c_ref, w2c_ref, gm_ref = refs[:5]
    par, x1_ref, outs = refs[5:5 + n_par], refs[5 + n_par], refs[6 + n_par:]
    _stage_weights(w1c_ref, w2c_ref, w1_ref, w2_ref)

    @pl.when(pl.program_id(0) >= FFN_WARM)
    def _():
        if split_in:
            x = jnp.where(pl.program_id(0) - FFN_WARM < n_lat, x_refs[0][...], x_refs[1][...])
        else:
            x = x_refs[0][...]
        x1 = _swiglu_step(x, mod_ref, 0, g1_ref, w1_ref, w2_ref)
        x1_ref[...] = x1
        h = _rms_mod(x1, gm_ref[...], _row(mod_ref, 3), _row(mod_ref, 4)).astype(BF16)
        proj(h, *par, *outs)


def _pre_call(name, proj, xs, mods, norm_g, w1, w2, layer, n_lat, n_tiles, par, par_specs, out_widths, out_dtypes):
    split_in = isinstance(xs, tuple)
    d = (xs[0] if split_in else xs).shape[1]
    tile = lambda i: jnp.maximum(i - FFN_WARM, 0)
    if split_in:
        x_specs = [pl.BlockSpec((TM, d), lambda i: (jnp.minimum(tile(i), n_lat - 1), 0)),
                   pl.BlockSpec((TM, d), lambda i: (jnp.maximum(tile(i) - n_lat, 0), 0))]
        x_args = list(xs)
    else:
        x_specs, x_args = [_tile_spec(d, FFN_WARM)], [xs]
    t = n_tiles * TM
    w_specs, w_scratch = _ffn_weight_specs(w1, w2, layer, 0)
    return pl.pallas_call(
        functools.partial(_pre_kernel, proj=proj, n_par=len(par), n_lat=n_lat, split_in=split_in),
        grid=(FFN_WARM + n_tiles,),
        in_specs=x_specs + [_mod_spec(mods, layer, n_lat, FFN_WARM), _pick_spec(norm_g.shape, (3 * layer,))] + w_specs
        + [_pick_spec(norm_g.shape, (3 * layer + 1,))] + par_specs,
        out_specs=[_tile_spec(d, FFN_WARM)] + [_tile_spec(w, FFN_WARM) for w in out_widths],
        out_shape=[jax.ShapeDtypeStruct((t, d), F32)]
        + [jax.ShapeDtypeStruct((t, w), dt) for w, dt in zip(out_widths, out_dtypes)],
        scratch_shapes=w_scratch,
        compiler_params=_params(),
        name=name,
    )(*x_args, mods, norm_g, w1, w2, norm_g, *par)


def _post_kernel(*refs, head, n_head, final):
    x_ref, mod_ref = refs[:2]
    head_refs = refs[2:2 + n_head]
    g2_ref, w1c_ref, w2c_ref = refs[2 + n_head:5 + n_head]
    o_ref, w1_ref, w2_ref = refs[-3:]
    _stage_weights(w1c_ref, w2c_ref, w1_ref, w2_ref)

    @pl.when(pl.program_id(0) >= FFN_WARM)
    def _():
        y = head(*head_refs[:-1])
        x2 = x_ref[...] + _row(mod_ref, 5) * _dot(y, head_refs[-1][...])
        x3 = _swiglu_step(x2, mod_ref, 6, g2_ref, w1_ref, w2_ref)
        o_ref[...] = _rms(x3, refs[5 + n_head][...]) if final else x3


def _post_call(name, head, xs, mods, norm_g, w1, w2, layer, n_lat, n_tiles, head_args, head_specs, final_g):
    d = xs.shape[1]
    fin_specs, fin_args = ([_full_spec((1, d))], [final_g.reshape(1, d)]) if final_g is not None else ([], [])
    w_specs, w_scratch = _ffn_weight_specs(w1, w2, layer, 1)
    return pl.pallas_call(
        functools.partial(_post_kernel, head=head, n_head=len(head_args), final=final_g is not None),
        grid=(FFN_WARM + n_tiles,),
        in_specs=[_tile_spec(d, FFN_WARM), _mod_spec(mods, layer, n_lat, FFN_WARM)] + head_specs
        + [_pick_spec(norm_g.shape, (3 * layer + 2,))] + w_specs + fin_specs,
        out_specs=_tile_spec(d, FFN_WARM),
        out_shape=jax.ShapeDtypeStruct((n_tiles * TM, d), F32),
        scratch_shapes=w_scratch,
        compiler_params=_params(),
        name=name,
    )(xs, mods, *head_args, norm_g, w1, w2, *fin_args)


def _gla_proj(h, w_ref, w2_ref, ba_ref, q_ref, k_ref, v_ref, og_ref, gg_ref):
    y = _dot(h, w_ref[...])
    q_ref[...] = (y[:, :GLA_HK] * (GLA_DK ** -0.5)).astype(BF16)
    k_ref[...] = y[:, GLA_HK:2 * GLA_HK].astype(BF16)
    v_ref[...] = y[:, 2 * GLA_HK:2 * GLA_HK + GLA_HV].astype(BF16)
    og_ref[...] = y[:, 2 * GLA_HK + GLA_HV:2 * GLA_HK + 2 * GLA_HV].astype(BF16)
    lr = y[:, 2 * GLA_HK + 2 * GLA_HV:].astype(BF16)
    logit = _dot(lr, w2_ref[...]) + ba_ref[...]
    gg_ref[...] = (-_softplus(-logit)) * (LOG2E / GLA_LOGIT_NORM)


def _gla_tile(q_ref, k_ref, v_ref, g_ref, o_ref, s_ref, rev):
    c = GLA_CHUNK
    tm = q_ref.shape[0]
    nc = tm // c
    r_i = lax.broadcasted_iota(jnp.int32, (tm, tm), 0)
    c_i = lax.broadcasted_iota(jnp.int32, (tm, tm), 1)
    same = (r_i // c) == (c_i // c)
    tri = jnp.logical_and(same, (c_i >= r_i) if rev else (c_i <= r_i))
    tri_bf = tri.astype(BF16)
    g = g_ref[...]
    g_hi = g.astype(BF16)
    g_lo = (g - g_hi.astype(F32)).astype(BF16)
    b = _dot(tri_bf, g_hi) + _dot(tri_bf, g_lo)
    ends = [b[n * c:n * c + 1, :] if rev else b[(n + 1) * c - 1:(n + 1) * c, :] for n in range(nc)]
    btot = jnp.concatenate([jnp.broadcast_to(e, (c, e.shape[1])) for e in ends], axis=0)
    k = k_ref[...].astype(F32)
    qe = (q_ref[...].astype(F32) * jnp.exp2(b)).astype(BF16)
    ke = (k * jnp.exp2(-b)).astype(BF16)
    kd = (k * jnp.exp2(btot - b)).astype(BF16)
    decs = [jnp.exp2(e) for e in ends]
    own = (lax.broadcasted_iota(jnp.int32, (tm, nc * GLA_DK), 0) // c
           == lax.broadcasted_iota(jnp.int32, (tm, nc * GLA_DK), 1) // GLA_DK)
    for h in range(GLA_HEADS):
        ks = slice(h * GLA_DK, (h + 1) * GLA_DK)
        vs = slice(h * GLA_DV, (h + 1) * GLA_DV)
        vh = v_ref[:, vs]
        att = jnp.where(tri, _dot_nt(qe[:, ks], ke[:, ks]), 0.0).astype(BF16)
        o_in = _dot(att, vh)
        kd_bd = jnp.where(own, jnp.concatenate([kd[:, ks]] * nc, axis=1), jnp.zeros((), BF16))
        upd = _dot_tn(vh, kd_bd)
        st = s_ref[h]
        for n in (reversed(range(nc)) if rev else range(nc)):
            rows = slice(n * c, (n + 1) * c)
            o_ref[rows, vs] = (o_in[rows, :] + _dot_nt(qe[rows, ks], st.astype(BF16))).astype(o_ref.dtype)
            st = decs[n][:, ks] * st + upd[:, n * GLA_DK:(n + 1) * GLA_DK]
        s_ref[h] = st


def _gla_kernel(qf_ref, kf_ref, vf_ref, gf_ref, qb_ref, kb_ref, vb_ref, gb_ref, of_ref, ob_ref, sf_ref, sb_ref):
    @pl.when(pl.program_id(0) == 0)
    def _():
        sf_ref[...] = jnp.zeros_like(sf_ref)
        sb_ref[...] = jnp.zeros_like(sb_ref)

    _gla_tile(qf_ref, kf_ref, vf_ref, gf_ref, of_ref, sf_ref, False)
    _gla_tile(qb_ref, kb_ref, vb_ref, gb_ref, ob_ref, sb_ref, True)


def _block_order(s, nb_lat, nb_real, rev):
    real = (nb_real - 1 - s) if rev else (s + nb_lat) % nb_real
    return jnp.where(s < nb_real, real, s)


def _gla_scan(q, k, v, gg, nb_lat, nb_real):
    t = q.shape[0]
    fwd = lambda s: _block_order(s, nb_lat, nb_real, False)
    bwd = lambda s: _block_order(s, nb_lat, nb_real, True)

    def specs(blk, col):
        return [pl.BlockSpec((TB, GLA_HK), lambda s: (blk(s), 0)), pl.BlockSpec((TB, GLA_HK), lambda s: (blk(s), 0)),
                pl.BlockSpec((TB, GLA_HV), lambda s: (blk(s), 0)), pl.BlockSpec((TB, GLA_HK), lambda s: (blk(s), col))]

    return pl.pallas_call(
        _gla_kernel,
        grid=(t // TB,),
        in_specs=specs(fwd, 0) + specs(bwd, 1),
        out_specs=[pl.BlockSpec((TB, GLA_HV), lambda s: (fwd(s), 0)), pl.BlockSpec((TB, GLA_HV), lambda s: (bwd(s), 0))],
        out_shape=[jax.ShapeDtypeStruct((t, GLA_HV), BF16)] * 2,
        scratch_shapes=[pltpu.VMEM((GLA_HEADS, GLA_DV, GLA_DK), F32)] * 2,
        compiler_params=_params(),
        name="gla_scan",
    )(q, k, v, gg, q, k, v, gg)


def _gla_out(of_ref, ob_ref, og_ref, ng_ref):
    o = of_ref[...].astype(F32) + ob_ref[...].astype(F32)
    ng = ng_ref[...]
    parts = [_rms(o[:, h * GLA_DV:(h + 1) * GLA_DV], ng) for h in range(GLA_HEADS)]
    return (jnp.concatenate(parts, axis=1) * _silu(og_ref[...].astype(F32))).astype(BF16)


def _gla_weights(w_in, w_a2, b_a, gla_norm_g, w_out):
    na, d, n_in = w_in.shape
    n_main = 2 * GLA_HK + 2 * GLA_HV
    w = jnp.pad(w_in.astype(BF16), ((0, 0), (0, 0), (0, n_main + GLA_LR_PAD - n_in)))
    w2 = jnp.zeros((na, GLA_LR_PAD, 2 * GLA_HK), F32)
    w2 = w2.at[:, :GLA_RANK, :GLA_HK].set(w_a2[:, 0]).at[:, GLA_RANK:2 * GLA_RANK, GLA_HK:].set(w_a2[:, 1])
    return (w, w2.astype(BF16), b_a.reshape(na, 1, 2 * GLA_HK), gla_norm_g.reshape(na, 1, GLA_DV), w_out.astype(BF16))


def _gla_layer(xs, mods, norm_g, w1, w2, layer, gw, j, n_lat, n_tot, n_out_tiles, nb_lat, nb_real, final_g):
    w, wlr, ba, ng, w_out = gw
    x1, q, k, v, og, gg = _pre_call(
        "pre_gla", _gla_proj, xs, mods, norm_g, w1, w2, layer, n_lat, n_tot, [w, wlr, ba],
        [_pick_spec(w.shape, (j,)), _pick_spec(wlr.shape, (j,)), _pick_spec(ba.shape, (j,))],
        [GLA_HK, GLA_HK, GLA_HV, GLA_HV, 2 * GLA_HK], [BF16, BF16, BF16, BF16, F32])
    o_f, o_b = _gla_scan(q, k, v, gg, nb_lat, nb_real)
    return _post_call(
        "post_gla", _gla_out, x1, mods, norm_g, w1, w2, layer, n_lat, n_out_tiles, [o_f, o_b, og, ng, w_out],
        [_tile_spec(GLA_HV, FFN_WARM)] * 3 + [_pick_spec(ng.shape, (j,)), _pick_spec(w_out.shape, (j,))], final_g)


def _rg_proj(h, w_ref, gate_ref, xr_ref):
    y = _dot(h, w_ref[...])
    gate_ref[...] = _gelu_tanh(y[:, :D_RNN]).astype(BF16)
    xr_ref[...] = y[:, D_RNN:]


def _rg_scan_kernel(xm_ref, xp_ref, xn_ref, cw_ref, cb_ref, w_ref, ba_ref, bx_ref, lam_ref, h_ref,
                    xs_ref, a_ref, u_ref, xp_pad_ref, hp_pad_ref, carry_ref, *, rev, nb_lat, nb_real):
    s = pl.program_id(0)
    blk = _block_order(s, nb_lat, nb_real, rev)

    @pl.when(s == 0)
    def _():
        carry_ref[...] = jnp.zeros_like(carry_ref)

    tm = xm_ref.shape[0]
    hal = SUBLANES
    seg = tm // SUBLANES
    nt = a_ref.shape[0]
    assert seg == RG_SEG and RG_BS == LANES
    has_prev = jnp.logical_and(blk != 0, jnp.logical_and(blk != nb_lat, blk != nb_real))
    has_next = jnp.logical_and(blk != nb_lat - 1, jnp.logical_and(blk != nb_real - 1, blk != pl.num_programs(0) - 1))
    xs_ref[0:hal, :] = jnp.where(has_prev, xp_ref[...], 0.0)
    xs_ref[hal:hal + tm, :] = xm_ref[...]
    xs_ref[hal + tm:, :] = jnp.where(has_next, xn_ref[...], 0.0)
    czh = (-0.5 * RG_C) * _softplus(-lam_ref[...])
    ext = seg + 2 * hal
    for n in range(nt):
        ns = slice(n * LANES, (n + 1) * LANES)
        for sg in range(SUBLANES):
            xp_pad_ref[n, sg * RG_XPITCH:sg * RG_XPITCH + ext, :] = xs_ref[sg * seg:sg * seg + ext, ns]
        xe = [xp_pad_ref[n, pl.ds(k, SUBLANES, stride=RG_XPITCH), :] for k in range(hal - 2, hal + seg + 1)]
        cw = [jnp.broadcast_to(cw_ref[j:j + 1, ns], (SUBLANES, LANES)) for j in range(4)]
        cb = jnp.broadcast_to(cb_ref[:, ns], (SUBLANES, LANES))
        xc = jnp.concatenate(
            [cb + cw[0] * xe[i] + cw[1] * xe[i + 1] + cw[2] * xe[i + 2] + cw[3] * xe[i + 3] for i in range(seg)],
            axis=0)
        z = _dot(xc.astype(BF16), w_ref[n])
        t_r = jnp.tanh(z[:, :RG_BS] + ba_ref[:, ns])
        t_i = jnp.tanh(z[:, RG_BS:] + bx_ref[:, ns])
        a = jnp.exp(czh[:, ns] + czh[:, ns] * t_r)
        a_ref[n] = a
        y = 1.0 - a * a
        xch = 0.5 * xc
        u_ref[n] = jnp.where(y > 0.0, y * lax.rsqrt(y), 0.0) * (xch + xch * t_i)

    def vreg(i):
        return pl.ds(i * SUBLANES, SUBLANES)

    def local(it, carry):
        i = (seg - 1 - it) if rev else it
        out = []
        for n in range(nt):
            a = a_ref[n, vreg(i), :]
            out += [a * carry[2 * n] + u_ref[n, vreg(i), :], a * carry[2 * n + 1]]
        return tuple(out)

    ends = (jnp.zeros((SUBLANES, LANES), F32), jnp.ones((SUBLANES, LANES), F32)) * nt
    for it in range(seg):
        ends = local(it, ends)
    cmats = []
    for n in range(nt):
        ns = slice(n * LANES, (n + 1) * LANES)
        hl_end, p_end = ends[2 * n], ends[2 * n + 1]
        c = carry_ref[:, ns]
        cin = [None] * SUBLANES
        for sg in (reversed(range(SUBLANES)) if rev else range(SUBLANES)):
            cin[sg] = c
            c = hl_end[sg:sg + 1, :] + p_end[sg:sg + 1, :] * c
        carry_ref[:, ns] = c
        cmats.append(jnp.concatenate(cin, axis=0))

    def final(it, carry):
        i = (seg - 1 - it) if rev else it
        out = []
        for n in range(nt):
            h = a_ref[n, vreg(i), :] * carry[n] + u_ref[n, vreg(i), :]
            hp_pad_ref[n, pl.ds(i, SUBLANES, stride=RG_HPITCH), :] = h
            out.append(h)
        return tuple(out)

    hs = tuple(cmats)
    for it in range(seg):
        hs = final(it, hs)
    for n in range(nt):
        for sg in range(SUBLANES):
            h_ref[sg * seg:(sg + 1) * seg, n * LANES:(n + 1) * LANES] = (
                hp_pad_ref[n, sg * RG_HPITCH:sg * RG_HPITCH + seg, :].astype(h_ref.dtype))


def _rg_scan(xr, cw, cb, w, ba, bx, lam, z, nb_lat, nb_real):
    t, dr = xr.shape
    per = TB // SUBLANES
    n8 = t // SUBLANES
    rev = z == 1
    blk = lambda s: _block_order(s, nb_lat, nb_real, rev)
    return pl.pallas_call(
        functools.partial(_rg_scan_kernel, rev=rev, nb_lat=nb_lat, nb_real=nb_real),
        grid=(t // TB,),
        in_specs=[pl.BlockSpec((TB, dr), lambda s: (blk(s), 0)),
                  pl.BlockSpec((SUBLANES, dr), lambda s: (jnp.maximum(blk(s) * per - 1, 0), 0)),
                  pl.BlockSpec((SUBLANES, dr), lambda s: (jnp.minimum((blk(s) + 1) * per, n8 - 1), 0)),
                  _full_spec(cw.shape), _full_spec(cb.shape), _pick_spec(w.shape, (z,)),
                  _pick_spec(ba.shape, (z,)), _pick_spec(bx.shape, (z,)), _pick_spec(lam.shape, (z,))],
        out_specs=pl.BlockSpec((TB, dr), lambda s: (blk(s), 0)),
        out_shape=jax.ShapeDtypeStruct((t, dr), BF16),
        scratch_shapes=[pltpu.VMEM((TB + 2 * SUBLANES, dr), F32),
                        pltpu.VMEM((dr // LANES, TB, LANES), F32), pltpu.VMEM((dr // LANES, TB, LANES), F32),
                        pltpu.VMEM((dr // LANES, SUBLANES * RG_XPITCH, LANES), F32),
                        pltpu.VMEM((dr // LANES, SUBLANES * RG_HPITCH, LANES), F32),
                        pltpu.VMEM((1, dr), F32)],
        compiler_params=_params(),
        name="rg_scan_bwd" if rev else "rg_scan_fwd",
    )(xr, xr, xr, cw, cb, w, ba, bx, lam)


def _rg_out(hf_ref, hb_ref, gate_ref):
    return ((hf_ref[...].astype(F32) + hb_ref[...].astype(F32)) * gate_ref[...].astype(F32)).astype(BF16)


def _rg_layer(xs, mods, norm_g, w1, w2, layer, w_in, conv_w, conv_b, w_a, b_a, w_x, b_x, lam, w_out, n_lat, n_tot,
              n_out_tiles, nb_lat, nb_real, final_g):
    w_in = w_in.astype(BF16)
    x1, gate, xr = _pre_call("pre_rg", _rg_proj, xs, mods, norm_g, w1, w2, layer, n_lat, n_tot, [w_in],
                             [_full_spec(w_in.shape)], [D_RNN, D_RNN], [BF16, F32])
    w = (0.5 * jnp.concatenate([w_a, w_x], axis=-1)).astype(BF16)
    hs = [_rg_scan(xr, conv_w, conv_b.reshape(1, D_RNN), w, (0.5 * b_a).reshape(2, 1, D_RNN),
                   (0.5 * b_x).reshape(2, 1, D_RNN), lam.reshape(2, 1, D_RNN), z, nb_lat, nb_real) for z in range(2)]
    w_out = w_out.astype(BF16)
    return _post_call("post_rg", _rg_out, x1, mods, norm_g, w1, w2, layer, n_lat, n_out_tiles,
                      [hs[0], hs[1], gate, w_out], [_tile_spec(D_RNN, FFN_WARM)] * 3 + [_full_spec(w_out.shape)],
                      final_g)


def _swa_proj(h, w_ref, cos_ref, sin_ref, q_ref, kz_ref, vz_ref):
    y = _dot(h, w_ref[...])
    nq = SWA_HEADS * HEAD_DIM
    nk = SWA_KV * HEAD_DIM
    cos = cos_ref[...]
    sin = sin_ref[...]
    rot = ROT_AXIS // 2
    lane = lax.broadcasted_iota(jnp.int32, (1, LANES), 1)
    first = (lane % ROT_AXIS) < rot

    def rope(xg):
        partner = jnp.where(first, pltpu.roll(xg, LANES - rot, axis=1), pltpu.roll(xg, rot, axis=1))
        return xg * cos + partner * sin

    q = y[:, :nq] * (HEAD_DIM ** -0.5 * LOG2E)
    for gi in range(nq // LANES):
        q_ref[:, gi * LANES:(gi + 1) * LANES] = rope(q[:, gi * LANES:(gi + 1) * LANES]).astype(BF16)
    low = lane < HEAD_DIM
    for gi in range(nk // LANES):
        kg = rope(y[:, nq + gi * LANES:nq + (gi + 1) * LANES])
        vg = y[:, nq + nk + gi * LANES:nq + nk + (gi + 1) * LANES]
        for par_j in range(2):
            j = 2 * gi + par_j
            for par_q in range(2):
                src_k = kg if par_q == par_j else pltpu.roll(kg, HEAD_DIM, axis=1)
                src_v = vg if par_q == par_j else pltpu.roll(vg, HEAD_DIM, axis=1)
                keep = low if par_q == 0 else jnp.logical_not(low)
                cs = slice((2 * j + par_q) * LANES, (2 * j + par_q + 1) * LANES)
                kz_ref[:, cs] = jnp.where(keep, src_k, 0.0).astype(BF16)
                vz_ref[:, cs] = jnp.where(keep, src_v, 1.0).astype(BF16)


def _swa_kernel(sink_ref, q_ref, kp_ref, k0_ref, kn_ref, vp_ref, v0_ref, vn_ref, kc_ref, vc_ref, o_ref,
                kz_ref, vz_ref, *, nb):
    qb = q_ref.shape[0]
    c_len = kc_ref.shape[0]
    i = pl.program_id(0)

    @pl.when(i == 0)
    def _():
        kz_ref[0:c_len, :] = kc_ref[...]
        vz_ref[0:c_len, :] = vc_ref[...]

    row = lax.broadcasted_iota(jnp.int32, (2 * qb, 1), 0)
    low = lax.broadcasted_iota(jnp.int32, (1, LANES), 1) < HEAD_DIM

    def attend(n_keys, mask):
        for j in range(SWA_KV):
            qs = jnp.concatenate([q_ref[:, (2 * j) * LANES:(2 * j + 1) * LANES],
                                  q_ref[:, (2 * j + 1) * LANES:(2 * j + 2) * LANES]], axis=0)
            lo = slice((2 * j) * LANES, (2 * j + 1) * LANES)
            hi = slice((2 * j + 1) * LANES, (2 * j + 2) * LANES)
            s_all = _dot_nt(qs, jnp.concatenate([kz_ref[0:n_keys, lo], kz_ref[0:n_keys, hi]], axis=0))
            ps, sinks, ms = [], [], []
            for par in range(2):
                s = mask(s_all[:, par * n_keys:(par + 1) * n_keys])
                sink = LOG2E * jnp.where(row < qb, sink_ref[j * SWA_G + par], sink_ref[j * SWA_G + 2 + par])
                m = jnp.maximum(jnp.max(s, axis=-1, keepdims=True), sink)
                ps.append(jnp.exp2(s - m).astype(BF16))
                sinks.append(sink)
                ms.append(m)
            r_all = _dot(jnp.concatenate(ps, axis=0), vz_ref[0:n_keys, (2 * j) * LANES:(2 * j + 2) * LANES])
            halves = []
            for par in range(2):
                r = r_all[par * 2 * qb:(par + 1) * 2 * qb, par * LANES:(par + 1) * LANES]
                halves.append(r / (pltpu.roll(r, HEAD_DIM, axis=1) + jnp.exp2(sinks[par] - ms[par])))
            o = jnp.where(low, halves[0], halves[1]).astype(BF16)
            o_ref[:, (2 * j) * LANES:(2 * j + 1) * LANES] = o[0:qb]
            o_ref[:, (2 * j + 1) * LANES:(2 * j + 2) * LANES] = o[qb:]

    @pl.when(i < nb)
    def _():
        for b, (k_ref, v_ref) in enumerate(((kp_ref, vp_ref), (k0_ref, v0_ref), (kn_ref, vn_ref))):
            kz_ref[c_len + b * qb:c_len + (b + 1) * qb, :] = k_ref[...]
            vz_ref[c_len + b * qb:c_len + (b + 1) * qb, :] = v_ref[...]
        r_i = lax.broadcasted_iota(jnp.int32, (2 * qb, qb), 0) % qb
        c_i = lax.broadcasted_iota(jnp.int32, (2 * qb, qb), 1)
        b_prev = jnp.where(jnp.logical_and(c_i >= r_i, i > 0), 0.0, NEG)
        b_next = jnp.where(jnp.logical_and(c_i <= r_i, i < nb - 1), 0.0, NEG)

        def mask(s):
            return jnp.concatenate([s[:, :c_len], s[:, c_len:c_len + qb] + b_prev,
                                    s[:, c_len + qb:c_len + 2 * qb], s[:, c_len + 2 * qb:] + b_next], axis=1)

        attend(c_len + 3 * qb, mask)

    @pl.when(i >= nb)
    def _():
        attend(c_len, lambda s: s)


def _swa_attn(q, kz, vz, sink, s_len, c_len):
    t, nq = q.shape
    nz = kz.shape[1]
    nb = s_len // Q_BLOCK
    n_blk = t // Q_BLOCK
    smem = pl.BlockSpec(memory_space=pltpu.SMEM)
    ctx_spec = pl.BlockSpec((c_len, nz), lambda i: (s_len // c_len, 0))
    band = [pl.BlockSpec((Q_BLOCK, nz), lambda i: (jnp.maximum(i - 1, 0), 0)),
            pl.BlockSpec((Q_BLOCK, nz), lambda i: (i, 0)),
            pl.BlockSpec((Q_BLOCK, nz), lambda i: (jnp.minimum(i + 1, n_blk - 1), 0))]
    n_keys = c_len + 3 * Q_BLOCK
    return pl.pallas_call(
        functools.partial(_swa_kernel, nb=nb),
        grid=(n_blk,),
        in_specs=[smem, pl.BlockSpec((Q_BLOCK, nq), lambda i: (i, 0))] + band + band + [ctx_spec, ctx_spec],
        out_specs=pl.BlockSpec((Q_BLOCK, nq), lambda i: (i, 0)),
        out_shape=jax.ShapeDtypeStruct((t, nq), BF16),
        scratch_shapes=[pltpu.VMEM((n_keys, nz), BF16), pltpu.VMEM((n_keys, nz), BF16)],
        compiler_params=_params(),
        name="swa_attn",
    )(sink, q, kz, kz, kz, vz, vz, vz, kz, vz)


def _rope_tables(s_len, c_len):
    rows = s_len // GRID_W
    freqs = ROPE_BASE ** (-jnp.arange(0, ROT_AXIS, 2, dtype=F32) / ROT_AXIS)
    dim = jnp.arange(LANES) % HEAD_DIM
    by_row = (dim // ROT_AXIS) == 0
    lane_freq = freqs[dim % (ROT_AXIS // 2)]
    sign = jnp.where((dim % ROT_AXIS) < ROT_AXIS // 2, -1.0, 1.0)
    ang_r = jnp.arange(rows, dtype=F32)[:, None] * lane_freq
    ang_c = jnp.arange(GRID_W, dtype=F32)[:, None] * lane_freq

    def table(fn, scale, ctx_fill):
        lat = jnp.where(by_row, (fn(ang_r) * scale)[:, None, :], (fn(ang_c) * scale)[None, :, :])
        return jnp.concatenate([lat.reshape(s_len, LANES), jnp.full((c_len, LANES), ctx_fill, F32)], axis=0)

    return table(jnp.cos, 1.0, 1.0), table(jnp.sin, sign, 0.0)


def _swa_out(o_ref):
    return o_ref[...]


def _swa_layer(xs, mods, norm_g, w1, w2, layer, w_qkv, sink, w_out, rope, n_lat, n_tot, n_out_tiles, s_len, c_len,
               final_g):
    w_qkv = w_qkv.astype(BF16)
    nq = SWA_HEADS * HEAD_DIM
    nz = 2 * SWA_KV * LANES
    x1, q, kz, vz = _pre_call("pre_swa", _swa_proj, xs, mods, norm_g, w1, w2, layer, n_lat, n_tot,
                              [w_qkv, rope[0], rope[1]],
                              [_full_spec(w_qkv.shape), _tile_spec(LANES, FFN_WARM), _tile_spec(LANES, FFN_WARM)],
                              [nq, nz, nz], [BF16, BF16, BF16])
    o = _swa_attn(q, kz, vz, sink, s_len, c_len)
    w_out = w_out.astype(BF16)
    return _post_call("post_swa", _swa_out, x1, mods, norm_g, w1, w2, layer, n_lat, n_out_tiles, [o, w_out],
                      [_tile_spec(nq, FFN_WARM), _full_spec(w_out.shape)], final_g)


def kernel(x, c, ctx, c_ctx, ada_w, ada_b, norm_g, final_g, ffn_w_in, ffn_w_out, gla_w_in, gla_w_a2, gla_b_a, gla_norm_g, gla_w_out, rg_w_in, rg_conv_w, rg_conv_b, rg_w_a, rg_b_a, rg_w_x, rg_b_x, rg_lam, rg_w_out, swa_w_qkv, swa_sink, swa_w_out):
    b, s_len, d = x.shape
    c_len = ctx.shape[1]
    assert b == 1 and s_len % TM == 0 and c_len % TB == 0 and s_len % c_len == 0
    depth = ada_w.shape[0]
    n_lat = s_len // TM
    n_tot = n_lat + -(-c_len // TM)
    nb_lat = s_len // TB
    nb_real = nb_lat + c_len // TB
    c_rows = (n_tot - n_lat) * TM
    mods = _ada_vectors(c, c_ctx, ada_w, ada_b)
    ng = norm_g.reshape(depth * 3, 1, d)
    w1, w2 = ffn_w_in, ffn_w_out
    gw = _gla_weights(gla_w_in, gla_w_a2, gla_b_a, gla_norm_g, gla_w_out)
    rope = _rope_tables(s_len, c_rows)
    xs = (x[0], jnp.pad(ctx[0], ((0, c_rows - c_len), (0, 0))))
    for i in range(depth):
        m, j = i % N_MIXERS, i // N_MIXERS
        last = i == depth - 1
        n_after = n_lat if last else n_tot
        fin = final_g if last else None
        if m == 0:
            xs = _gla_layer(xs, mods, ng, w1, w2, i, gw, j, n_lat, n_tot, n_after, nb_lat, nb_real, fin)
        elif m == 1:
            xs = _rg_layer(xs, mods, ng, w1, w2, i, rg_w_in[j], rg_conv_w[j], rg_conv_b[j], rg_w_a[j], rg_b_a[j],
                           rg_w_x[j], rg_b_x[j], rg_lam[j], rg_w_out[j], n_lat, n_tot, n_after, nb_lat, nb_real, fin)
        else:
            xs = _swa_layer(xs, mods, ng, w1, w2, i, swa_w_qkv[j], swa_sink[j], swa_w_out[j], rope, n_lat, n_tot,
                            n_after, s_len, c_len, fin)
    return xs[None]
```

```python
import functools
import math

import jax
import jax.numpy as jnp
from jax import lax
from jax.experimental import pallas as pl
from jax.experimental.pallas import tpu as pltpu

F32 = jnp.float32
BF16 = jnp.bfloat16

N_MOD = 9
NORM_EPS = 1e-6
GLA_HEADS = 4
GLA_DK = 128
GLA_DV = 256
GLA_RANK = 16
GLA_LOGIT_NORM = 16.0
GLA_CHUNK = 64
GLA_HK = GLA_HEADS * GLA_DK
GLA_HV = GLA_HEADS * GLA_DV
GLA_LR_PAD = 128
D_RNN = 1536
RG_BLOCKS = 12
RG_BS = D_RNN // RG_BLOCKS
RG_C = 8.0
SWA_HEADS = 16
SWA_KV = 4
SWA_G = SWA_HEADS // SWA_KV
HEAD_DIM = 64
WINDOW = 128
Q_BLOCK = 128
ROT_AXIS = HEAD_DIM // 2
ROPE_BASE = 10000.0
GRID_W = 64
N_MIXERS = 3

TM = 512
TB = 256
FFN_WARM = 16
SUBLANES = 8
LANES = 128
VMEM_LIMIT = 60 * 1024 * 1024
NEG = -1e30
LOG2E = 1.0 / math.log(2.0)


def _odd_pitch(rows):
    tiles = -(-rows // SUBLANES)
    return (tiles + (tiles % 2 == 0)) * SUBLANES


RG_SEG = TB // SUBLANES
RG_HPITCH = _odd_pitch(RG_SEG)
RG_XPITCH = _odd_pitch(RG_SEG + 2 * SUBLANES)


def _params(n_axes=1):
    return pltpu.CompilerParams(dimension_semantics=("arbitrary",) * n_axes, vmem_limit_bytes=VMEM_LIMIT)


def _dot(a, b):
    return jnp.dot(a, b, preferred_element_type=F32)


def _dot_nt(a, b):
    return lax.dot_general(a, b, (((1,), (1,)), ((), ())), preferred_element_type=F32)


def _dot_tn(a, b):
    return lax.dot_general(a, b, (((0,), (0,)), ((), ())), preferred_element_type=F32)


def _rms(x, g):
    return (x * lax.rsqrt(jnp.mean(x * x, axis=-1, keepdims=True) + NORM_EPS)) * g


def _rms_mod(x, g, shift, scale):
    return _rms(x, g) * (1.0 + scale) + shift


def _silu(x):
    xh = 0.5 * x
    return xh + xh * jnp.tanh(xh)


def _softplus(x):
    return jnp.maximum(x, 0.0) + jnp.log(1.0 + jnp.exp(-jnp.abs(x)))


def _gelu_tanh(x):
    return 0.5 * x * (1.0 + jnp.tanh(math.sqrt(2.0 / math.pi) * (x + 0.044715 * (x * x * x))))


def _row(ref, r):
    return ref[r:r + 1, :]


def _tile_spec(width, shift=0):
    return pl.BlockSpec((TM, width), lambda i: (jnp.maximum(i - shift, 0), 0))


def _full_spec(shape):
    nd = len(shape)
    return pl.BlockSpec(shape, lambda i: (0,) * nd, pipeline_mode=pl.Buffered(1))


def _pick_spec(shape, lead):
    tail = shape[len(lead):]
    return pl.BlockSpec((None,) * len(lead) + tuple(tail), lambda i: tuple(lead) + (0,) * len(tail),
                        pipeline_mode=pl.Buffered(1))


def _mod_spec(mods, layer, n_lat, shift=0):
    _, _, n_mod, d = mods.shape
    return pl.BlockSpec((None, None, n_mod, d),
                        lambda i: (layer, jnp.minimum(jnp.maximum(i - shift, 0) // n_lat, 1), 0, 0))


def _ada_kernel(cc_ref, w_ref, b_ref, o_ref):
    act = _silu(cc_ref[...]).astype(BF16)
    o_ref[...] = _dot(act, w_ref[...].astype(BF16)) + b_ref[...]


def _ada_vectors(c, c_ctx, ada_w, ada_b):
    depth, d, nd = ada_w.shape
    cc = jnp.zeros((SUBLANES, d), F32).at[0].set(c[0]).at[1].set(c_ctx)
    wb = 3 * d
    assert nd % wb == 0
    out = pl.pallas_call(
        _ada_kernel,
        grid=(depth, nd // wb),
        in_specs=[pl.BlockSpec((SUBLANES, d), lambda l, j: (0, 0)),
                  pl.BlockSpec((None, d, wb), lambda l, j: (l, 0, j)),
                  pl.BlockSpec((None, 1, wb), lambda l, j: (l, 0, j))],
        out_specs=pl.BlockSpec((None, SUBLANES, wb), lambda l, j: (l, 0, j)),
        out_shape=jax.ShapeDtypeStruct((depth, SUBLANES, nd), F32),
        compiler_params=_params(2),
        name="ada_vectors",
    )(cc, ada_w, ada_b.reshape(depth, 1, nd))
    return out.reshape(depth, SUBLANES, N_MOD, d)


def _swiglu_step(x, mod_ref, r0, g_ref, w1_ref, w2_ref):
    h = _rms_mod(x, g_ref[...], _row(mod_ref, r0), _row(mod_ref, r0 + 1)).astype(BF16)
    gu = _dot(h, w1_ref[...])
    f = w2_ref.shape[0]
    a = (_silu(gu[:, :f]) * gu[:, f:]).astype(BF16)
    return x + (0.5 * _row(mod_ref, r0 + 2)) * _dot(a, w2_ref[...])


def _stage_weights(w1c_ref, w2c_ref, w1_ref, w2_ref):
    i = pl.program_id(0)

    @pl.when(i < FFN_WARM)
    def _():
        for c_ref, s_ref in ((w1c_ref, w1_ref), (w2c_ref, w2_ref)):
            r = c_ref.shape[0]
            s_ref[pl.ds(pl.multiple_of(i * r, r), r), :] = c_ref[...].astype(BF16)


def _ffn_weight_specs(w1, w2, layer, half):
    specs, scratch = [], []
    for w in (w1, w2):
        rows, cols = w.shape[-2:]
        assert rows % (FFN_WARM * 2 * SUBLANES) == 0
        specs.append(pl.BlockSpec((None, None, rows // FFN_WARM, cols),
                                  lambda i: (layer, half, jnp.minimum(i, FFN_WARM - 1), 0)))
        scratch.append(pltpu.VMEM((rows, cols), BF16))
    return specs, scratch


def _pre_kernel(*refs, proj, n_par, n_lat, split_in):
    refs = list(refs)
    w1_ref, w2_ref = refs[-2:]
    refs = refs[:-2]
    x_refs, refs = (refs[:2], refs[2:]) if split_in else (refs[:1], refs[1:])
    mod_ref, g1_ref, w1c_ref, w2c_ref, gm_ref = refs[:5]
    par, x1_ref, outs = refs[5:5 + n_par], refs[5 + n_par], refs[6 + n_par:]
    _stage_weights(w1c_ref, w2c_ref, w1_ref, w2_ref)

    @pl.when(pl.program_id(0) >= FFN_WARM)
    def _():
        if split_in:
            x = jnp.where(pl.program_id(0) - FFN_WARM < n_lat, x_refs[0][...], x_refs[1][...])
        else:
            x = x_refs[0][...]
        x1 = _swiglu_step(x, mod_ref, 0, g1_ref, w1_ref, w2_ref)
        x1_ref[...] = x1
        h = _rms_mod(x1, gm_ref[...], _row(mod_ref, 3), _row(mod_ref, 4)).astype(BF16)
        proj(h, *par, *outs)


def _pre_call(name, proj, xs, mods, norm_g, w1, w2, layer, n_lat, n_tiles, par, par_specs, out_widths, out_dtypes):
    split_in = isinstance(xs, tuple)
    d = (xs[0] if split_in else xs).shape[1]
    tile = lambda i: jnp.maximum(i - FFN_WARM, 0)
    if split_in:
        x_specs = [pl.BlockSpec((TM, d), lambda i: (jnp.minimum(tile(i), n_lat - 1), 0)),
                   pl.BlockSpec((TM, d), lambda i: (jnp.maximum(tile(i) - n_lat, 0), 0))]
        x_args = list(xs)
    else:
        x_specs, x_args = [_tile_spec(d, FFN_WARM)], [xs]
    t = n_tiles * TM
    w_specs, w_scratch = _ffn_weight_specs(w1, w2, layer, 0)
    return pl.pallas_call(
        functools.partial(_pre_kernel, proj=proj, n_par=len(par), n_lat=n_lat, split_in=split_in),
        grid=(FFN_WARM + n_tiles,),
        in_specs=x_specs + [_mod_spec(mods, layer, n_lat, FFN_WARM), _pick_spec(norm_g.shape, (3 * layer,))] + w_specs
        + [_pick_spec(norm_g.shape, (3 * layer + 1,))] + par_specs,
        out_specs=[_tile_spec(d, FFN_WARM)] + [_tile_spec(w, FFN_WARM) for w in out_widths],
        out_shape=[jax.ShapeDtypeStruct((t, d), F32)]
        + [jax.ShapeDtypeStruct((t, w), dt) for w, dt in zip(out_widths, out_dtypes)],
        scratch_shapes=w_scratch,
        compiler_params=_params(),
        name=name,
    )(*x_args, mods, norm_g, w1, w2, norm_g, *par)


def _post_kernel(*refs, head, n_head, final):
    x_ref, mod_ref = refs[:2]
    head_refs = refs[2:2 + n_head]
    g2_ref, w1c_ref, w2c_ref = refs[2 + n_head:5 + n_head]
    o_ref, w1_ref, w2_ref = refs[-3:]
    _stage_weights(w1c_ref, w2c_ref, w1_ref, w2_ref)

    @pl.when(pl.program_id(0) >= FFN_WARM)
    def _():
        y = head(*head_refs[:-1])
        x2 = x_ref[...] + _row(mod_ref, 5) * _dot(y, head_refs[-1][...])
        x3 = _swiglu_step(x2, mod_ref, 6, g2_ref, w1_ref, w2_ref)
        o_ref[...] = _rms(x3, refs[5 + n_head][...]) if final else x3


def _post_call(name, head, xs, mods, norm_g, w1, w2, layer, n_lat, n_tiles, head_args, head_specs, final_g):
    d = xs.shape[1]
    fin_specs, fin_args = ([_full_spec((1, d))], [final_g.reshape(1, d)]) if final_g is not None else ([], [])
    w_specs, w_scratch = _ffn_weight_specs(w1, w2, layer, 1)
    return pl.pallas_call(
        functools.partial(_post_kernel, head=head, n_head=len(head_args), final=final_g is not None),
        grid=(FFN_WARM + n_tiles,),
        in_specs=[_tile_spec(d, FFN_WARM), _mod_spec(mods, layer, n_lat, FFN_WARM)] + head_specs
        + [_pick_spec(norm_g.shape, (3 * layer + 2,))] + w_specs + fin_specs,
        out_specs=_tile_spec(d, FFN_WARM),
        out_shape=jax.ShapeDtypeStruct((n_tiles * TM, d), F32),
        scratch_shapes=w_scratch,
        compiler_params=_params(),
        name=name,
    )(xs, mods, *head_args, norm_g, w1, w2, *fin_args)


def _gla_proj(h, w_ref, w2_ref, ba_ref, q_ref, k_ref, v_ref, og_ref, gg_ref):
    y = _dot(h, w_ref[...])
    q_ref[...] = (y[:, :GLA_HK] * (GLA_DK ** -0.5)).astype(BF16)
    k_ref[...] = y[:, GLA_HK:2 * GLA_HK].astype(BF16)
    v_ref[...] = y[:, 2 * GLA_HK:2 * GLA_HK + GLA_HV].astype(BF16)
    og_ref[...] = y[:, 2 * GLA_HK + GLA_HV:2 * GLA_HK + 2 * GLA_HV].astype(BF16)
    lr = y[:, 2 * GLA_HK + 2 * GLA_HV:].astype(BF16)
    logit = _dot(lr, w2_ref[...]) + ba_ref[...]
    gg_ref[...] = (-_softplus(-logit)) * (LOG2E / GLA_LOGIT_NORM)


def _gla_tile(q_ref, k_ref, v_ref, g_ref, o_ref, s_ref, rev):
    c = GLA_CHUNK
    tm = q_ref.shape[0]
    nc = tm // c
    r_i = lax.broadcasted_iota(jnp.int32, (tm, tm), 0)
    c_i = lax.broadcasted_iota(jnp.int32, (tm, tm), 1)
    same = (r_i // c) == (c_i // c)
    tri = jnp.logical_and(same, (c_i >= r_i) if rev else (c_i <= r_i))
    tri_bf = tri.astype(BF16)
    g = g_ref[...]
    g_hi = g.astype(BF16)
    g_lo = (g - g_hi.astype(F32)).astype(BF16)
    b = _dot(tri_bf, g_hi) + _dot(tri_bf, g_lo)
    ends = [b[n * c:n * c + 1, :] if rev else b[(n + 1) * c - 1:(n + 1) * c, :] for n in range(nc)]
    btot = jnp.concatenate([jnp.broadcast_to(e, (c, e.shape[1])) for e in ends], axis=0)
    k = k_ref[...].astype(F32)
    qe = (q_ref[...].astype(F32) * jnp.exp2(b)).astype(BF16)
    ke = (k * jnp.exp2(-b)).astype(BF16)
    kd = (k * jnp.exp2(btot - b)).astype(BF16)
    decs = [jnp.exp2(e) for e in ends]
    own = (lax.broadcasted_iota(jnp.int32, (tm, nc * GLA_DK), 0) // c
           == lax.broadcasted_iota(jnp.int32, (tm, nc * GLA_DK), 1) // GLA_DK)
    for h in range(GLA_HEADS):
        ks = slice(h * GLA_DK, (h + 1) * GLA_DK)
        vs = slice(h * GLA_DV, (h + 1) * GLA_DV)
        vh = v_ref[:, vs]
        att = jnp.where(tri, _dot_nt(qe[:, ks], ke[:, ks]), 0.0).astype(BF16)
        o_in = _dot(att, vh)
        kd_bd = jnp.where(own, jnp.concatenate([kd[:, ks]] * nc, axis=1), jnp.zeros((), BF16))
        upd = _dot_tn(vh, kd_bd)
        st = s_ref[h]
        for n in (reversed(range(nc)) if rev else range(nc)):
            rows = slice(n * c, (n + 1) * c)
            o_ref[rows, vs] = (o_in[rows, :] + _dot_nt(qe[rows, ks], st.astype(BF16))).astype(o_ref.dtype)
            st = decs[n][:, ks] * st + upd[:, n * GLA_DK:(n + 1) * GLA_DK]
        s_ref[h] = st


def _gla_kernel(qf_ref, kf_ref, vf_ref, gf_ref, qb_ref, kb_ref, vb_ref, gb_ref, of_ref, ob_ref, sf_ref, sb_ref):
    @pl.when(pl.program_id(0) == 0)
    def _():
        sf_ref[...] = jnp.zeros_like(sf_ref)
        sb_ref[...] = jnp.zeros_like(sb_ref)

    _gla_tile(qf_ref, kf_ref, vf_ref, gf_ref, of_ref, sf_ref, False)
    _gla_tile(qb_ref, kb_ref, vb_ref, gb_ref, ob_ref, sb_ref, True)


def _block_order(s, nb_lat, nb_real, rev):
    real = (nb_real - 1 - s) if rev else (s + nb_lat) % nb_real
    return jnp.where(s < nb_real, real, s)


def _gla_scan(q, k, v, gg, nb_lat, nb_real):
    t = q.shape[0]
    fwd = lambda s: _block_order(s, nb_lat, nb_real, False)
    bwd = lambda s: _block_order(s, nb_lat, nb_real, True)

    def specs(blk, col):
        return [pl.BlockSpec((TB, GLA_HK), lambda s: (blk(s), 0)), pl.BlockSpec((TB, GLA_HK), lambda s: (blk(s), 0)),
                pl.BlockSpec((TB, GLA_HV), lambda s: (blk(s), 0)), pl.BlockSpec((TB, GLA_HK), lambda s: (blk(s), col))]

    return pl.pallas_call(
        _gla_kernel,
        grid=(t // TB,),
        in_specs=specs(fwd, 0) + specs(bwd, 1),
        out_specs=[pl.BlockSpec((TB, GLA_HV), lambda s: (fwd(s), 0)), pl.BlockSpec((TB, GLA_HV), lambda s: (bwd(s), 0))],
        out_shape=[jax.ShapeDtypeStruct((t, GLA_HV), BF16)] * 2,
        scratch_shapes=[pltpu.VMEM((GLA_HEADS, GLA_DV, GLA_DK), F32)] * 2,
        compiler_params=_params(),
        name="gla_scan",
    )(q, k, v, gg, q, k, v, gg)


def _gla_out(of_ref, ob_ref, og_ref, ng_ref):
    o = of_ref[...].astype(F32) + ob_ref[...].astype(F32)
    ng = ng_ref[...]
    parts = [_rms(o[:, h * GLA_DV:(h + 1) * GLA_DV], ng) for h in range(GLA_HEADS)]
    return (jnp.concatenate(parts, axis=1) * _silu(og_ref[...].astype(F32))).astype(BF16)


def _gla_weights(w_in, w_a2, b_a, gla_norm_g, w_out):
    na, d, n_in = w_in.shape
    n_main = 2 * GLA_HK + 2 * GLA_HV
    w = jnp.pad(w_in.astype(BF16), ((0, 0), (0, 0), (0, n_main + GLA_LR_PAD - n_in)))
    w2 = jnp.zeros((na, GLA_LR_PAD, 2 * GLA_HK), F32)
    w2 = w2.at[:, :GLA_RANK, :GLA_HK].set(w_a2[:, 0]).at[:, GLA_RANK:2 * GLA_RANK, GLA_HK:].set(w_a2[:, 1])
    return (w, w2.astype(BF16), b_a.reshape(na, 1, 2 * GLA_HK), gla_norm_g.reshape(na, 1, GLA_DV), w_out.astype(BF16))


def _gla_layer(xs, mods, norm_g, w1, w2, layer, gw, j, n_lat, n_tot, n_out_tiles, nb_lat, nb_real, final_g):
    w, wlr, ba, ng, w_out = gw
    x1, q, k, v, og, gg = _pre_call(
        "pre_gla", _gla_proj, xs, mods, norm_g, w1, w2, layer, n_lat, n_tot, [w, wlr, ba],
        [_pick_spec(w.shape, (j,)), _pick_spec(wlr.shape, (j,)), _pick_spec(ba.shape, (j,))],
        [GLA_HK, GLA_HK, GLA_HV, GLA_HV, 2 * GLA_HK], [BF16, BF16, BF16, BF16, F32])
    o_f, o_b = _gla_scan(q, k, v, gg, nb_lat, nb_real)
    return _post_call(
        "post_gla", _gla_out, x1, mods, norm_g, w1, w2, layer, n_lat, n_out_tiles, [o_f, o_b, og, ng, w_out],
        [_tile_spec(GLA_HV, FFN_WARM)] * 3 + [_pick_spec(ng.shape, (j,)), _pick_spec(w_out.shape, (j,))], final_g)


def _rg_proj(h, w_ref, gate_ref, xr_ref):
    y = _dot(h, w_ref[...])
    gate_ref[...] = _gelu_tanh(y[:, :D_RNN]).astype(BF16)
    xr_ref[...] = y[:, D_RNN:]


def _rg_scan_kernel(xm_ref, xp_ref, xn_ref, cw_ref, cb_ref, w_ref, ba_ref, bx_ref, lam_ref, h_ref,
                    xs_ref, a_ref, u_ref, xp_pad_ref, hp_pad_ref, carry_ref, *, rev, nb_lat, nb_real):
    s = pl.program_id(0)
    blk = _block_order(s, nb_lat, nb_real, rev)

    @pl.when(s == 0)
    def _():
        carry_ref[...] = jnp.zeros_like(carry_ref)

    tm = xm_ref.shape[0]
    hal = SUBLANES
    seg = tm // SUBLANES
    nt = a_ref.shape[0]
    assert seg == RG_SEG and RG_BS == LANES
    has_prev = jnp.logical_and(blk != 0, jnp.logical_and(blk != nb_lat, blk != nb_real))
    has_next = jnp.logical_and(blk != nb_lat - 1, jnp.logical_and(blk != nb_real - 1, blk != pl.num_programs(0) - 1))
    xs_ref[0:hal, :] = jnp.where(has_prev, xp_ref[...], 0.0)
    xs_ref[hal:hal + tm, :] = xm_ref[...]
    xs_ref[hal + tm:, :] = jnp.where(has_next, xn_ref[...], 0.0)
    czh = (-0.5 * RG_C) * _softplus(-lam_ref[...])
    ext = seg + 2 * hal
    for n in range(nt):
        ns = slice(n * LANES, (n + 1) * LANES)
        for sg in range(SUBLANES):
            xp_pad_ref[n, sg * RG_XPITCH:sg * RG_XPITCH + ext, :] = xs_ref[sg * seg:sg * seg + ext, ns]
        xe = [xp_pad_ref[n, pl.ds(k, SUBLANES, stride=RG_XPITCH), :] for k in range(hal - 2, hal + seg + 1)]
        cw = [jnp.broadcast_to(cw_ref[j:j + 1, ns], (SUBLANES, LANES)) for j in range(4)]
        cb = jnp.broadcast_to(cb_ref[:, ns], (SUBLANES, LANES))
        xc = jnp.concatenate(
            [cb + cw[0] * xe[i] + cw[1] * xe[i + 1] + cw[2] * xe[i + 2] + cw[3] * xe[i + 3] for i in range(seg)],
            axis=0)
        z = _dot(xc.astype(BF16), w_ref[n])
        t_r = jnp.tanh(z[:, :RG_BS] + ba_ref[:, ns])
        t_i = jnp.tanh(z[:, RG_BS:] + bx_ref[:, ns])
        a = jnp.exp(czh[:, ns] + czh[:, ns] * t_r)
        a_ref[n] = a
        y = 1.0 - a * a
        xch = 0.5 * xc
        u_ref[n] = jnp.where(y > 0.0, y * lax.rsqrt(y), 0.0) * (xch + xch * t_i)

    def vreg(i):
        return pl.ds(i * SUBLANES, SUBLANES)

    def local(it, carry):
        i = (seg - 1 - it) if rev else it
        out = []
        for n in range(nt):
            a = a_ref[n, vreg(i), :]
            out += [a * carry[2 * n] + u_ref[n, vreg(i), :], a * carry[2 * n + 1]]
        return tuple(out)

    ends = (jnp.zeros((SUBLANES, LANES), F32), jnp.ones((SUBLANES, LANES), F32)) * nt
    for it in range(seg):
        ends = local(it, ends)
    cmats = []
    for n in range(nt):
        ns = slice(n * LANES, (n + 1) * LANES)
        hl_end, p_end = ends[2 * n], ends[2 * n + 1]
        c = carry_ref[:, ns]
        cin = [None] * SUBLANES
        for sg in (reversed(range(SUBLANES)) if rev else range(SUBLANES)):
            cin[sg] = c
            c = hl_end[sg:sg + 1, :] + p_end[sg:sg + 1, :] * c
        carry_ref[:, ns] = c
        cmats.append(jnp.concatenate(cin, axis=0))

    def final(it, carry):
        i = (seg - 1 - it) if rev else it
        out = []
        for n in range(nt):
            h = a_ref[n, vreg(i), :] * carry[n] + u_ref[n, vreg(i), :]
            hp_pad_ref[n, pl.ds(i, SUBLANES, stride=RG_HPITCH), :] = h
            out.append(h)
        return tuple(out)

    hs = tuple(cmats)
    for it in range(seg):
        hs = final(it, hs)
    for n in range(nt):
        for sg in range(SUBLANES):
            h_ref[sg * seg:(sg + 1) * seg, n * LANES:(n + 1) * LANES] = (
                hp_pad_ref[n, sg * RG_HPITCH:sg * RG_HPITCH + seg, :].astype(h_ref.dtype))


def _rg_scan(xr, cw, cb, w, ba, bx, lam, z, nb_lat, nb_real):
    t, dr = xr.shape
    per = TB // SUBLANES
    n8 = t // SUBLANES
    rev = z == 1
    blk = lambda s: _block_order(s, nb_lat, nb_real, rev)
    return pl.pallas_call(
        functools.partial(_rg_scan_kernel, rev=rev, nb_lat=nb_lat, nb_real=nb_real),
        grid=(t // TB,),
        in_specs=[pl.BlockSpec((TB, dr), lambda s: (blk(s), 0)),
                  pl.BlockSpec((SUBLANES, dr), lambda s: (jnp.maximum(blk(s) * per - 1, 0), 0)),
                  pl.BlockSpec((SUBLANES, dr), lambda s: (jnp.minimum((blk(s) + 1) * per, n8 - 1), 0)),
                  _full_spec(cw.shape), _full_spec(cb.shape), _pick_spec(w.shape, (z,)),
                  _pick_spec(ba.shape, (z,)), _pick_spec(bx.shape, (z,)), _pick_spec(lam.shape, (z,))],
        out_specs=pl.BlockSpec((TB, dr), lambda s: (blk(s), 0)),
        out_shape=jax.ShapeDtypeStruct((t, dr), BF16),
        scratch_shapes=[pltpu.VMEM((TB + 2 * SUBLANES, dr), F32),
                        pltpu.VMEM((dr // LANES, TB, LANES), F32), pltpu.VMEM((dr // LANES, TB, LANES), F32),
                        pltpu.VMEM((dr // LANES, SUBLANES * RG_XPITCH, LANES), F32),
                        pltpu.VMEM((dr // LANES, SUBLANES * RG_HPITCH, LANES), F32),
                        pltpu.VMEM((1, dr), F32)],
        compiler_params=_params(),
        name="rg_scan_bwd" if rev else "rg_scan_fwd",
    )(xr, xr, xr, cw, cb, w, ba, bx, lam)


def _rg_out(hf_ref, hb_ref, gate_ref):
    return ((hf_ref[...].astype(F32) + hb_ref[...].astype(F32)) * gate_ref[...].astype(F32)).astype(BF16)


def _rg_layer(xs, mods, norm_g, w1, w2, layer, w_in, conv_w, conv_b, w_a, b_a, w_x, b_x, lam, w_out, n_lat, n_tot,
              n_out_tiles, nb_lat, nb_real, final_g):
    w_in = w_in.astype(BF16)
    x1, gate, xr = _pre_call("pre_rg", _rg_proj, xs, mods, norm_g, w1, w2, layer, n_lat, n_tot, [w_in],
                             [_full_spec(w_in.shape)], [D_RNN, D_RNN], [BF16, F32])
    w = (0.5 * jnp.concatenate([w_a, w_x], axis=-1)).astype(BF16)
    hs = [_rg_scan(xr, conv_w, conv_b.reshape(1, D_RNN), w, (0.5 * b_a).reshape(2, 1, D_RNN),
                   (0.5 * b_x).reshape(2, 1, D_RNN), lam.reshape(2, 1, D_RNN), z, nb_lat, nb_real) for z in range(2)]
    w_out = w_out.astype(BF16)
    return _post_call("post_rg", _rg_out, x1, mods, norm_g, w1, w2, layer, n_lat, n_out_tiles,
                      [hs[0], hs[1], gate, w_out], [_tile_spec(D_RNN, FFN_WARM)] * 3 + [_full_spec(w_out.shape)],
                      final_g)


def _swa_proj(h, w_ref, cos_ref, sin_ref, q_ref, kz_ref, vz_ref):
    y = _dot(h, w_ref[...])
    nq = SWA_HEADS * HEAD_DIM
    nk = SWA_KV * HEAD_DIM
    cos = cos_ref[...]
    sin = sin_ref[...]
    rot = ROT_AXIS // 2
    lane = lax.broadcasted_iota(jnp.int32, (1, LANES), 1)
    first = (lane % ROT_AXIS) < rot

    def rope(xg):
        partner = jnp.where(first, pltpu.roll(xg, LANES - rot, axis=1), pltpu.roll(xg, rot, axis=1))
        return xg * cos + partner * sin

    q = y[:, :nq] * (HEAD_DIM ** -0.5 * LOG2E)
    for gi in range(nq // LANES):
        q_ref[:, gi * LANES:(gi + 1) * LANES] = rope(q[:, gi * LANES:(gi + 1) * LANES]).astype(BF16)
    low = lane < HEAD_DIM
    for gi in range(nk // LANES):
        kg = rope(y[:, nq + gi * LANES:nq + (gi + 1) * LANES])
        vg = y[:, nq + nk + gi * LANES:nq + nk + (gi + 1) * LANES]
        for par_j in range(2):
            j = 2 * gi + par_j
            for par_q in range(2):
                src_k = kg if par_q == par_j else pltpu.roll(kg, HEAD_DIM, axis=1)
                src_v = vg if par_q == par_j else pltpu.roll(vg, HEAD_DIM, axis=1)
                keep = low if par_q == 0 else jnp.logical_not(low)
                cs = slice((2 * j + par_q) * LANES, (2 * j + par_q + 1) * LANES)
                kz_ref[:, cs] = jnp.where(keep, src_k, 0.0).astype(BF16)
                vz_ref[:, cs] = jnp.where(keep, src_v, 1.0).astype(BF16)


def _swa_kernel(sink_ref, q_ref, kp_ref, k0_ref, kn_ref, vp_ref, v0_ref, vn_ref, kc_ref, vc_ref, o_ref,
                kz_ref, vz_ref, *, nb):
    qb = q_ref.shape[0]
    c_len = kc_ref.shape[0]
    i = pl.program_id(0)

    @pl.when(i == 0)
    def _():
        kz_ref[0:c_len, :] = kc_ref[...]
        vz_ref[0:c_len, :] = vc_ref[...]

    row = lax.broadcasted_iota(jnp.int32, (2 * qb, 1), 0)
    low = lax.broadcasted_iota(jnp.int32, (1, LANES), 1) < HEAD_DIM

    def attend(n_keys, mask):
        for j in range(SWA_KV):
            qs = jnp.concatenate([q_ref[:, (2 * j) * LANES:(2 * j + 1) * LANES],
                                  q_ref[:, (2 * j + 1) * LANES:(2 * j + 2) * LANES]], axis=0)
            lo = slice((2 * j) * LANES, (2 * j + 1) * LANES)
            hi = slice((2 * j + 1) * LANES, (2 * j + 2) * LANES)
            s_all = _dot_nt(qs, jnp.concatenate([kz_ref[0:n_keys, lo], kz_ref[0:n_keys, hi]], axis=0))
            ps, sinks, ms = [], [], []
            for par in range(2):
                s = mask(s_all[:, par * n_keys:(par + 1) * n_keys])
                sink = LOG2E * jnp.where(row < qb, sink_ref[j * SWA_G + par], sink_ref[j * SWA_G + 2 + par])
                m = jnp.maximum(jnp.max(s, axis=-1, keepdims=True), sink)
                ps.append(jnp.exp2(s - m).astype(BF16))
                sinks.append(sink)
                ms.append(m)
            r_all = _dot(jnp.concatenate(ps, axis=0), vz_ref[0:n_keys, (2 * j) * LANES:(2 * j + 2) * LANES])
            halves = []
            for par in range(2):
                r = r_all[par * 2 * qb:(par + 1) * 2 * qb, par * LANES:(par + 1) * LANES]
                halves.append(r / (pltpu.roll(r, HEAD_DIM, axis=1) + jnp.exp2(sinks[par] - ms[par])))
            o = jnp.where(low, halves[0], halves[1]).astype(BF16)
            o_ref[:, (2 * j) * LANES:(2 * j + 1) * LANES] = o[0:qb]
            o_ref[:, (2 * j + 1) * LANES:(2 * j + 2) * LANES] = o[qb:]

    @pl.when(i < nb)
    def _():
        for b, (k_ref, v_ref) in enumerate(((kp_ref, vp_ref), (k0_ref, v0_ref), (kn_ref, vn_ref))):
            kz_ref[c_len + b * qb:c_len + (b + 1) * qb, :] = k_ref[...]
            vz_ref[c_len + b * qb:c_len + (b + 1) * qb, :] = v_ref[...]
        r_i = lax.broadcasted_iota(jnp.int32, (2 * qb, qb), 0) % qb
        c_i = lax.broadcasted_iota(jnp.int32, (2 * qb, qb), 1)
        b_prev = jnp.where(jnp.logical_and(c_i >= r_i, i > 0), 0.0, NEG)
        b_next = jnp.where(jnp.logical_and(c_i <= r_i, i < nb - 1), 0.0, NEG)

        def mask(s):
            return jnp.concatenate([s[:, :c_len], s[:, c_len:c_len + qb] + b_prev,
                                    s[:, c_len + qb:c_len + 2 * qb], s[:, c_len + 2 * qb:] + b_next], axis=1)

        attend(c_len + 3 * qb, mask)

    @pl.when(i >= nb)
    def _():
        attend(c_len, lambda s: s)


def _swa_attn(q, kz, vz, sink, s_len, c_len):
    t, nq = q.shape
    assert WINDOW == Q_BLOCK and s_len % Q_BLOCK == 0 and s_len % c_len == 0
    nz = kz.shape[1]
    nb = s_len // Q_BLOCK
    n_blk = t // Q_BLOCK
    smem = pl.BlockSpec(memory_space=pltpu.SMEM)
    ctx_spec = pl.BlockSpec((c_len, nz), lambda i: (s_len // c_len, 0))
    band = [pl.BlockSpec((Q_BLOCK, nz), lambda i: (jnp.maximum(i - 1, 0), 0)),
            pl.BlockSpec((Q_BLOCK, nz), lambda i: (i, 0)),
            pl.BlockSpec((Q_BLOCK, nz), lambda i: (jnp.minimum(i + 1, n_blk - 1), 0))]
    n_keys = c_len + 3 * Q_BLOCK
    return pl.pallas_call(
        functools.partial(_swa_kernel, nb=nb),
        grid=(n_blk,),
        in_specs=[smem, pl.BlockSpec((Q_BLOCK, nq), lambda i: (i, 0))] + band + band + [ctx_spec, ctx_spec],
        out_specs=pl.BlockSpec((Q_BLOCK, nq), lambda i: (i, 0)),
        out_shape=jax.ShapeDtypeStruct((t, nq), BF16),
        scratch_shapes=[pltpu.VMEM((n_keys, nz), BF16), pltpu.VMEM((n_keys, nz), BF16)],
        compiler_params=_params(),
        name="swa_attn",
    )(sink, q, kz, kz, kz, vz, vz, vz, kz, vz)


def _rope_tables(s_len, c_len):
    rows = s_len // GRID_W
    freqs = ROPE_BASE ** (-jnp.arange(0, ROT_AXIS, 2, dtype=F32) / ROT_AXIS)
    dim = jnp.arange(LANES) % HEAD_DIM
    by_row = (dim // ROT_AXIS) == 0
    lane_freq = freqs[dim % (ROT_AXIS // 2)]
    sign = jnp.where((dim % ROT_AXIS) < ROT_AXIS // 2, -1.0, 1.0)
    ang_r = jnp.arange(rows, dtype=F32)[:, None] * lane_freq
    ang_c = jnp.arange(GRID_W, dtype=F32)[:, None] * lane_freq

    def table(fn, scale, ctx_fill):
        lat = jnp.where(by_row, (fn(ang_r) * scale)[:, None, :], (fn(ang_c) * scale)[None, :, :])
        return jnp.concatenate([lat.reshape(s_len, LANES), jnp.full((c_len, LANES), ctx_fill, F32)], axis=0)

    return table(jnp.cos, 1.0, 1.0), table(jnp.sin, sign, 0.0)


def _swa_out(o_ref):
    return o_ref[...]


def _swa_layer(xs, mods, norm_g, w1, w2, layer, w_qkv, sink, w_out, rope, n_lat, n_tot, n_out_tiles, s_len, c_len,
               final_g):
    w_qkv = w_qkv.astype(BF16)
    nq = SWA_HEADS * HEAD_DIM
    nz = 2 * SWA_KV * LANES
    x1, q, kz, vz = _pre_call("pre_swa", _swa_proj, xs, mods, norm_g, w1, w2, layer, n_lat, n_tot,
                              [w_qkv, rope[0], rope[1]],
                              [_full_spec(w_qkv.shape), _tile_spec(LANES, FFN_WARM), _tile_spec(LANES, FFN_WARM)],
                              [nq, nz, nz], [BF16, BF16, BF16])
    o = _swa_attn(q, kz, vz, sink, s_len, c_len)
    w_out = w_out.astype(BF16)
    return _post_call("post_swa", _swa_out, x1, mods, norm_g, w1, w2, layer, n_lat, n_out_tiles, [o, w_out],
                      [_tile_spec(nq, FFN_WARM), _full_spec(w_out.shape)], final_g)


def kernel(x, c, ctx, c_ctx, ada_w, ada_b, norm_g, final_g, ffn_w_in, ffn_w_out, gla_w_in, gla_w_a2, gla_b_a, gla_norm_g, gla_w_out, rg_w_in, rg_conv_w, rg_conv_b, rg_w_a, rg_b_a, rg_w_x, rg_b_x, rg_lam, rg_w_out, swa_w_qkv, swa_sink, swa_w_out):
    b, s_len, d = x.shape
    c_len = ctx.shape[1]
    assert b == 1 and s_len % TM == 0 and c_len % TB == 0 and s_len % c_len == 0
    depth = ada_w.shape[0]
    n_lat = s_len // TM
    n_tot = n_lat + -(-c_len // TM)
    nb_lat = s_len // TB
    nb_real = nb_lat + c_len // TB
    c_rows = (n_tot - n_lat) * TM
    mods = _ada_vectors(c, c_ctx, ada_w, ada_b)
    ng = norm_g.reshape(depth * 3, 1, d)
    w1, w2 = ffn_w_in, ffn_w_out
    gw = _gla_weights(gla_w_in, gla_w_a2, gla_b_a, gla_norm_g, gla_w_out)
    rope = _rope_tables(s_len, c_rows)
    xs = (x[0], jnp.pad(ctx[0], ((0, c_rows - c_len), (0, 0))))
    for i in range(depth):
        m, j = i % N_MIXERS, i // N_MIXERS
        last = i == depth - 1
        n_after = n_lat if last else n_tot
        fin = final_g if last else None
        if m == 0:
            xs = _gla_layer(xs, mods, ng, w1, w2, i, gw, j, n_lat, n_tot, n_after, nb_lat, nb_real, fin)
        elif m == 1:
            xs = _rg_layer(xs, mods, ng, w1, w2, i, rg_w_in[j], rg_conv_w[j], rg_conv_b[j], rg_w_a[j], rg_b_a[j],
                           rg_w_x[j], rg_b_x[j], rg_lam[j], rg_w_out[j], n_lat, n_tot, n_after, nb_lat, nb_real, fin)
        else:
            xs = _swa_layer(xs, mods, ng, w1, w2, i, swa_w_qkv[j], swa_sink[j], swa_w_out[j], rope, n_lat, n_tot,
                            n_after, s_len, c_len, fin)
    return xs[None]
```

```python
import functools
import math

import jax
import jax.numpy as jnp
from jax import lax
from jax.experimental import pallas as pl
from jax.experimental.pallas import tpu as pltpu

F32 = jnp.float32
BF16 = jnp.bfloat16

N_MOD = 9
NORM_EPS = 1e-6
GLA_HEADS = 4
GLA_DK = 128
GLA_DV = 256
GLA_RANK = 16
GLA_LOGIT_NORM = 16.0
GLA_CHUNK = 64
GLA_HK = GLA_HEADS * GLA_DK
GLA_HV = GLA_HEADS * GLA_DV
GLA_LR_PAD = 128
D_RNN = 1536
RG_BLOCKS = 12
RG_BS = D_RNN // RG_BLOCKS
RG_C = 8.0
SWA_HEADS = 16
SWA_KV = 4
SWA_G = SWA_HEADS // SWA_KV
HEAD_DIM = 64
WINDOW = 128
Q_BLOCK = 128
ROT_AXIS = HEAD_DIM // 2
ROPE_BASE = 10000.0
GRID_W = 64
N_MIXERS = 3

TM = 512
TB = 256
FFN_WARM = 16
SUBLANES = 8
LANES = 128
VMEM_LIMIT = 60 * 1024 * 1024
NEG = -1e30
LOG2E = 1.0 / math.log(2.0)


def _odd_pitch(rows):
    tiles = -(-rows // SUBLANES)
    return (tiles + (tiles % 2 == 0)) * SUBLANES


RG_SEG = TB // SUBLANES
RG_HPITCH = _odd_pitch(RG_SEG)
RG_XPITCH = _odd_pitch(RG_SEG + 2 * SUBLANES)


def _params(n_axes=1):
    return pltpu.CompilerParams(dimension_semantics=("arbitrary",) * n_axes, vmem_limit_bytes=VMEM_LIMIT)


def _dot(a, b):
    return jnp.dot(a, b, preferred_element_type=F32)


def _dot_nt(a, b):
    return lax.dot_general(a, b, (((1,), (1,)), ((), ())), preferred_element_type=F32)


def _dot_tn(a, b):
    return lax.dot_general(a, b, (((0,), (0,)), ((), ())), preferred_element_type=F32)


def _rms(x, g):
    return (x * lax.rsqrt(jnp.mean(x * x, axis=-1, keepdims=True) + NORM_EPS)) * g


def _rms_mod(x, g, shift, scale):
    return _rms(x, g) * (1.0 + scale) + shift


def _silu(x):
    xh = 0.5 * x
    return xh + xh * jnp.tanh(xh)


def _softplus(x):
    return jnp.maximum(x, 0.0) + jnp.log(1.0 + jnp.exp(-jnp.abs(x)))


def _gelu_tanh(x):
    return 0.5 * x * (1.0 + jnp.tanh(math.sqrt(2.0 / math.pi) * (x + 0.044715 * (x * x * x))))


def _row(ref, r):
    return ref[r:r + 1, :]


def _tile_spec(width, shift=0):
    return pl.BlockSpec((TM, width), lambda i: (jnp.maximum(i - shift, 0), 0))


def _full_spec(shape):
    nd = len(shape)
    return pl.BlockSpec(shape, lambda i: (0,) * nd, pipeline_mode=pl.Buffered(1))


def _pick_spec(shape, lead):
    tail = shape[len(lead):]
    return pl.BlockSpec((None,) * len(lead) + tuple(tail), lambda i: tuple(lead) + (0,) * len(tail),
                        pipeline_mode=pl.Buffered(1))


def _mod_spec(mods, layer, n_lat, shift=0):
    _, _, n_mod, d = mods.shape
    return pl.BlockSpec((None, None, n_mod, d),
                        lambda i: (layer, jnp.minimum(jnp.maximum(i - shift, 0) // n_lat, 1), 0, 0))


def _ada_kernel(cc_ref, w_ref, b_ref, o_ref):
    act = _silu(cc_ref[...]).astype(BF16)
    o_ref[...] = _dot(act, w_ref[...].astype(BF16)) + b_ref[...]


def _ada_vectors(c, c_ctx, ada_w, ada_b):
    depth, d, nd = ada_w.shape
    cc = jnp.zeros((SUBLANES, d), F32).at[0].set(c[0]).at[1].set(c_ctx)
    wb = 3 * d
    assert nd % wb == 0
    out = pl.pallas_call(
        _ada_kernel,
        grid=(depth, nd // wb),
        in_specs=[pl.BlockSpec((SUBLANES, d), lambda l, j: (0, 0)),
                  pl.BlockSpec((None, d, wb), lambda l, j: (l, 0, j)),
                  pl.BlockSpec((None, 1, wb), lambda l, j: (l, 0, j))],
        out_specs=pl.BlockSpec((None, SUBLANES, wb), lambda l, j: (l, 0, j)),
        out_shape=jax.ShapeDtypeStruct((depth, SUBLANES, nd), F32),
        compiler_params=_params(2),
        name="ada_vectors",
    )(cc, ada_w, ada_b.reshape(depth, 1, nd))
    return out.reshape(depth, SUBLANES, N_MOD, d)


def _swiglu_step(x, mod_ref, r0, g_ref, w1_ref, w2_ref):
    h = _rms_mod(x, g_ref[...], _row(mod_ref, r0), _row(mod_ref, r0 + 1)).astype(BF16)
    gu = _dot(h, w1_ref[...])
    f = w2_ref.shape[0]
    a = (_silu(gu[:, :f]) * gu[:, f:]).astype(BF16)
    return x + (0.5 * _row(mod_ref, r0 + 2)) * _dot(a, w2_ref[...])


def _stage_weights(w1c_ref, w2c_ref, w1_ref, w2_ref):
    i = pl.program_id(0)

    @pl.when(i < FFN_WARM)
    def _():
        for c_ref, s_ref in ((w1c_ref, w1_ref), (w2c_ref, w2_ref)):
            r = c_ref.shape[0]
            s_ref[pl.ds(pl.multiple_of(i * r, r), r), :] = c_ref[...].astype(BF16)


def _ffn_weight_specs(w1, w2, layer, half):
    specs, scratch = [], []
    for w in (w1, w2):
        rows, cols = w.shape[-2:]
        assert rows % (FFN_WARM * 2 * SUBLANES) == 0
        specs.append(pl.BlockSpec((None, None, rows // FFN_WARM, cols),
                                  lambda i: (layer, half, jnp.minimum(i, FFN_WARM - 1), 0)))
        scratch.append(pltpu.VMEM((rows, cols), BF16))
    return specs, scratch


def _pre_kernel(*refs, proj, n_par, n_lat, split_in):
    refs = list(refs)
    w1_ref, w2_ref = refs[-2:]
    refs = refs[:-2]
    x_refs, refs = (refs[:2], refs[2:]) if split_in else (refs[:1], refs[1:])
    mod_ref, g1_ref, w1c_ref, w2c_ref, gm_ref = refs[:5]
    par, x1_ref, outs = refs[5:5 + n_par], refs[5 + n_par], refs[6 + n_par:]
    _stage_weights(w1c_ref, w2c_ref, w1_ref, w2_ref)

    @pl.when(pl.program_id(0) >= FFN_WARM)
    def _():
        if split_in:
            x = jnp.where(pl.program_id(0) - FFN_WARM < n_lat, x_refs[0][...], x_refs[1][...])
        else:
            x = x_refs[0][...]
        x1 = _swiglu_step(x, mod_ref, 0, g1_ref, w1_ref, w2_ref)
        x1_ref[...] = x1
        h = _rms_mod(x1, gm_ref[...], _row(mod_ref, 3), _row(mod_ref, 4)).astype(BF16)
        proj(h, *par, *outs)


def _pre_call(name, proj, xs, mods, norm_g, w1, w2, layer, n_lat, n_tiles, par, par_specs, out_widths, out_dtypes):
    split_in = isinstance(xs, tuple)
    d = (xs[0] if split_in else xs).shape[1]
    tile = lambda i: jnp.maximum(i - FFN_WARM, 0)
    if split_in:
        x_specs = [pl.BlockSpec((TM, d), lambda i: (jnp.minimum(tile(i), n_lat - 1), 0)),
                   pl.BlockSpec((TM, d), lambda i: (jnp.maximum(tile(i) - n_lat, 0), 0))]
        x_args = list(xs)
    else:
        x_specs, x_args = [_tile_spec(d, FFN_WARM)], [xs]
    t = n_tiles * TM
    w_specs, w_scratch = _ffn_weight_specs(w1, w2, layer, 0)
    return pl.pallas_call(
        functools.partial(_pre_kernel, proj=proj, n_par=len(par), n_lat=n_lat, split_in=split_in),
        grid=(FFN_WARM + n_tiles,),
        in_specs=x_specs + [_mod_spec(mods, layer, n_lat, FFN_WARM), _pick_spec(norm_g.shape, (3 * layer,))] + w_specs
        + [_pick_spec(norm_g.shape, (3 * layer + 1,))] + par_specs,
        out_specs=[_tile_spec(d, FFN_WARM)] + [_tile_spec(w, FFN_WARM) for w in out_widths],
        out_shape=[jax.ShapeDtypeStruct((t, d), F32)]
        + [jax.ShapeDtypeStruct((t, w), dt) for w, dt in zip(out_widths, out_dtypes)],
        scratch_shapes=w_scratch,
        compiler_params=_params(),
        name=name,
    )(*x_args, mods, norm_g, w1, w2, norm_g, *par)


def _post_kernel(*refs, head, n_head, final):
    x_ref, mod_ref = refs[:2]
    head_refs = refs[2:2 + n_head]
    g2_ref, w1c_ref, w2c_ref = refs[2 + n_head:5 + n_head]
    o_ref, w1_ref, w2_ref = refs[-3:]
    _stage_weights(w1c_ref, w2c_ref, w1_ref, w2_ref)

    @pl.when(pl.program_id(0) >= FFN_WARM)
    def _():
        y = head(*head_refs[:-1])
        x2 = x_ref[...] + _row(mod_ref, 5) * _dot(y, head_refs[-1][...])
        x3 = _swiglu_step(x2, mod_ref, 6, g2_ref, w1_ref, w2_ref)
        o_ref[...] = _rms(x3, refs[5 + n_head][...]) if final else x3


def _post_call(name, head, xs, mods, norm_g, w1, w2, layer, n_lat, n_tiles, head_args, head_specs, final_g):
    d = xs.shape[1]
    fin_specs, fin_args = ([_full_spec((1, d))], [final_g.reshape(1, d)]) if final_g is not None else ([], [])
    w_specs, w_scratch = _ffn_weight_specs(w1, w2, layer, 1)
    return pl.pallas_call(
        functools.partial(_post_kernel, head=head, n_head=len(head_args), final=final_g is not None),
        grid=(FFN_WARM + n_tiles,),
        in_specs=[_tile_spec(d, FFN_WARM), _mod_spec(mods, layer, n_lat, FFN_WARM)] + head_specs
        + [_pick_spec(norm_g.shape, (3 * layer + 2,))] + w_specs + fin_specs,
        out_specs=_tile_spec(d, FFN_WARM),
        out_shape=jax.ShapeDtypeStruct((n_tiles * TM, d), F32),
        scratch_shapes=w_scratch,
        compiler_params=_params(),
        name=name,
    )(xs, mods, *head_args, norm_g, w1, w2, *fin_args)


def _gla_proj(h, w_ref, w2_ref, ba_ref, q_ref, k_ref, v_ref, og_ref, gg_ref):
    y = _dot(h, w_ref[...])
    q_ref[...] = (y[:, :GLA_HK] * (GLA_DK ** -0.5)).astype(BF16)
    k_ref[...] = y[:, GLA_HK:2 * GLA_HK].astype(BF16)
    v_ref[...] = y[:, 2 * GLA_HK:2 * GLA_HK + GLA_HV].astype(BF16)
    og_ref[...] = y[:, 2 * GLA_HK + GLA_HV:2 * GLA_HK + 2 * GLA_HV].astype(BF16)
    lr = y[:, 2 * GLA_HK + 2 * GLA_HV:].astype(BF16)
    logit = _dot(lr, w2_ref[...]) + ba_ref[...]
    gg_ref[...] = (-_softplus(-logit)) * (LOG2E / GLA_LOGIT_NORM)


def _gla_tile(q_ref, k_ref, v_ref, g_ref, o_ref, s_ref, rev):
    c = GLA_CHUNK
    tm = q_ref.shape[0]
    nc = tm // c
    r_i = lax.broadcasted_iota(jnp.int32, (tm, tm), 0)
    c_i = lax.broadcasted_iota(jnp.int32, (tm, tm), 1)
    same = (r_i // c) == (c_i // c)
    tri = jnp.logical_and(same, (c_i >= r_i) if rev else (c_i <= r_i))
    tri_bf = tri.astype(BF16)
    g = g_ref[...]
    g_hi = g.astype(BF16)
    g_lo = (g - g_hi.astype(F32)).astype(BF16)
    b = _dot(tri_bf, g_hi) + _dot(tri_bf, g_lo)
    ends = [b[n * c:n * c + 1, :] if rev else b[(n + 1) * c - 1:(n + 1) * c, :] for n in range(nc)]
    btot = jnp.concatenate([jnp.broadcast_to(e, (c, e.shape[1])) for e in ends], axis=0)
    k = k_ref[...].astype(F32)
    qe = (q_ref[...].astype(F32) * jnp.exp2(b)).astype(BF16)
    ke = (k * jnp.exp2(-b)).astype(BF16)
    kd = (k * jnp.exp2(btot - b)).astype(BF16)
    decs = [jnp.exp2(e) for e in ends]
    own = (lax.broadcasted_iota(jnp.int32, (tm, nc * GLA_DK), 0) // c
           == lax.broadcasted_iota(jnp.int32, (tm, nc * GLA_DK), 1) // GLA_DK)
    for h in range(GLA_HEADS):
        ks = slice(h * GLA_DK, (h + 1) * GLA_DK)
        vs = slice(h * GLA_DV, (h + 1) * GLA_DV)
        vh = v_ref[:, vs]
        att = jnp.where(tri, _dot_nt(qe[:, ks], ke[:, ks]), 0.0).astype(BF16)
        o_in = _dot(att, vh)
        kd_bd = jnp.where(own, jnp.concatenate([kd[:, ks]] * nc, axis=1), jnp.zeros((), BF16))
        upd = _dot_tn(vh, kd_bd)
        st = s_ref[h]
        for n in (reversed(range(nc)) if rev else range(nc)):
            rows = slice(n * c, (n + 1) * c)
            o_ref[rows, vs] = (o_in[rows, :] + _dot_nt(qe[rows, ks], st.astype(BF16))).astype(o_ref.dtype)
            st = decs[n][:, ks] * st + upd[:, n * GLA_DK:(n + 1) * GLA_DK]
        s_ref[h] = st


def _gla_kernel(qf_ref, kf_ref, vf_ref, gf_ref, qb_ref, kb_ref, vb_ref, gb_ref, of_ref, ob_ref, sf_ref, sb_ref):
    @pl.when(pl.program_id(0) == 0)
    def _():
        sf_ref[...] = jnp.zeros_like(sf_ref)
        sb_ref[...] = jnp.zeros_like(sb_ref)

    _gla_tile(qf_ref, kf_ref, vf_ref, gf_ref, of_ref, sf_ref, False)
    _gla_tile(qb_ref, kb_ref, vb_ref, gb_ref, ob_ref, sb_ref, True)


def _block_order(s, nb_lat, nb_real, rev):
    real = (nb_real - 1 - s) if rev else (s + nb_lat) % nb_real
    return jnp.where(s < nb_real, real, s)


def _gla_scan(q, k, v, gg, nb_lat, nb_real):
    t = q.shape[0]
    fwd = lambda s: _block_order(s, nb_lat, nb_real, False)
    bwd = lambda s: _block_order(s, nb_lat, nb_real, True)

    def specs(blk, col):
        return [pl.BlockSpec((TB, GLA_HK), lambda s: (blk(s), 0)), pl.BlockSpec((TB, GLA_HK), lambda s: (blk(s), 0)),
                pl.BlockSpec((TB, GLA_HV), lambda s: (blk(s), 0)), pl.BlockSpec((TB, GLA_HK), lambda s: (blk(s), col))]

    return pl.pallas_call(
        _gla_kernel,
        grid=(t // TB,),
        in_specs=specs(fwd, 0) + specs(bwd, 1),
        out_specs=[pl.BlockSpec((TB, GLA_HV), lambda s: (fwd(s), 0)), pl.BlockSpec((TB, GLA_HV), lambda s: (bwd(s), 0))],
        out_shape=[jax.ShapeDtypeStruct((t, GLA_HV), BF16)] * 2,
        scratch_shapes=[pltpu.VMEM((GLA_HEADS, GLA_DV, GLA_DK), F32)] * 2,
        compiler_params=_params(),
        name="gla_scan",
    )(q, k, v, gg, q, k, v, gg)


def _gla_out(of_ref, ob_ref, og_ref, ng_ref):
    o = of_ref[...].astype(F32) + ob_ref[...].astype(F32)
    ng = ng_ref[...]
    parts = [_rms(o[:, h * GLA_DV:(h + 1) * GLA_DV], ng) for h in range(GLA_HEADS)]
    return (jnp.concatenate(parts, axis=1) * _silu(og_ref[...].astype(F32))).astype(BF16)


def _gla_weights(w_in, w_a2, b_a, gla_norm_g, w_out):
    na, d, n_in = w_in.shape
    n_main = 2 * GLA_HK + 2 * GLA_HV
    w = jnp.pad(w_in.astype(BF16), ((0, 0), (0, 0), (0, n_main + GLA_LR_PAD - n_in)))
    w2 = jnp.zeros((na, GLA_LR_PAD, 2 * GLA_HK), F32)
    w2 = w2.at[:, :GLA_RANK, :GLA_HK].set(w_a2[:, 0]).at[:, GLA_RANK:2 * GLA_RANK, GLA_HK:].set(w_a2[:, 1])
    return (w, w2.astype(BF16), b_a.reshape(na, 1, 2 * GLA_HK), gla_norm_g.reshape(na, 1, GLA_DV), w_out.astype(BF16))


def _gla_layer(xs, mods, norm_g, w1, w2, layer, gw, j, n_lat, n_tot, n_out_tiles, nb_lat, nb_real, final_g):
    w, wlr, ba, ng, w_out = gw
    x1, q, k, v, og, gg = _pre_call(
        "pre_gla", _gla_proj, xs, mods, norm_g, w1, w2, layer, n_lat, n_tot, [w, wlr, ba],
        [_pick_spec(w.shape, (j,)), _pick_spec(wlr.shape, (j,)), _pick_spec(ba.shape, (j,))],
        [GLA_HK, GLA_HK, GLA_HV, GLA_HV, 2 * GLA_HK], [BF16, BF16, BF16, BF16, F32])
    o_f, o_b = _gla_scan(q, k, v, gg, nb_lat, nb_real)
    return _post_call(
        "post_gla", _gla_out, x1, mods, norm_g, w1, w2, layer, n_lat, n_out_tiles, [o_f, o_b, og, ng, w_out],
        [_tile_spec(GLA_HV, FFN_WARM)] * 3 + [_pick_spec(ng.shape, (j,)), _pick_spec(w_out.shape, (j,))], final_g)


def _rg_proj(h, w_ref, gate_ref, xr_ref):
    y = _dot(h, w_ref[...])
    gate_ref[...] = _gelu_tanh(y[:, :D_RNN]).astype(BF16)
    xr_ref[...] = y[:, D_RNN:]


def _rg_scan_kernel(xm_ref, xp_ref, xn_ref, cw_ref, cb_ref, w_ref, ba_ref, bx_ref, lam_ref, h_ref,
                    xs_ref, a_ref, u_ref, xp_pad_ref, hp_pad_ref, carry_ref, *, rev, nb_lat, nb_real):
    s = pl.program_id(0)
    blk = _block_order(s, nb_lat, nb_real, rev)

    @pl.when(s == 0)
    def _():
        carry_ref[...] = jnp.zeros_like(carry_ref)

    tm = xm_ref.shape[0]
    hal = SUBLANES
    seg = tm // SUBLANES
    nt = a_ref.shape[0]
    assert seg == RG_SEG and RG_BS == LANES
    has_prev = jnp.logical_and(blk != 0, jnp.logical_and(blk != nb_lat, blk != nb_real))
    has_next = jnp.logical_and(blk != nb_lat - 1, jnp.logical_and(blk != nb_real - 1, blk != pl.num_programs(0) - 1))
    xs_ref[0:hal, :] = jnp.where(has_prev, xp_ref[...], 0.0)
    xs_ref[hal:hal + tm, :] = xm_ref[...]
    xs_ref[hal + tm:, :] = jnp.where(has_next, xn_ref[...], 0.0)
    lam = lam_ref[...]
    czh = (-0.5 * RG_C) * (jnp.maximum(-lam, 0.0) + jnp.log1p(jnp.exp(-jnp.abs(lam))))
    ext = seg + 2 * hal
    for n in range(nt):
        ns = slice(n * LANES, (n + 1) * LANES)
        for sg in range(SUBLANES):
            xp_pad_ref[n, sg * RG_XPITCH:sg * RG_XPITCH + ext, :] = xs_ref[sg * seg:sg * seg + ext, ns]
        xe = [xp_pad_ref[n, pl.ds(k, SUBLANES, stride=RG_XPITCH), :] for k in range(hal - 2, hal + seg + 1)]
        cw = [jnp.broadcast_to(cw_ref[j:j + 1, ns], (SUBLANES, LANES)) for j in range(4)]
        cb = jnp.broadcast_to(cb_ref[:, ns], (SUBLANES, LANES))
        xc = jnp.concatenate(
            [cb + cw[0] * xe[i] + cw[1] * xe[i + 1] + cw[2] * xe[i + 2] + cw[3] * xe[i + 3] for i in range(seg)],
            axis=0)
        z = _dot(xc.astype(BF16), w_ref[n])
        t_r = jnp.tanh(z[:, :RG_BS] + ba_ref[:, ns])
        t_i = jnp.tanh(z[:, RG_BS:] + bx_ref[:, ns])
        log_a = czh[:, ns] + czh[:, ns] * t_r
        a = jnp.exp(log_a)
        a_ref[n] = a
        y = 1.0 - a * a
        xch = 0.5 * xc
        u_ref[n] = jnp.where(y > 0.0, y * lax.rsqrt(y), 0.0) * (xch + xch * t_i)

    def vreg(i):
        return pl.ds(i * SUBLANES, SUBLANES)

    def local(it, carry):
        i = (seg - 1 - it) if rev else it
        out = []
        for n in range(nt):
            a = a_ref[n, vreg(i), :]
            out += [a * carry[2 * n] + u_ref[n, vreg(i), :], a * carry[2 * n + 1]]
        return tuple(out)

    ends = (jnp.zeros((SUBLANES, LANES), F32), jnp.ones((SUBLANES, LANES), F32)) * nt
    for it in range(seg):
        ends = local(it, ends)
    cmats = []
    for n in range(nt):
        ns = slice(n * LANES, (n + 1) * LANES)
        hl_end, p_end = ends[2 * n], ends[2 * n + 1]
        c = carry_ref[:, ns]
        cin = [None] * SUBLANES
        for sg in (reversed(range(SUBLANES)) if rev else range(SUBLANES)):
            cin[sg] = c
            c = hl_end[sg:sg + 1, :] + p_end[sg:sg + 1, :] * c
        carry_ref[:, ns] = c
        cmats.append(jnp.concatenate(cin, axis=0))

    def final(it, carry):
        i = (seg - 1 - it) if rev else it
        out = []
        for n in range(nt):
            h = a_ref[n, vreg(i), :] * carry[n] + u_ref[n, vreg(i), :]
            hp_pad_ref[n, pl.ds(i, SUBLANES, stride=RG_HPITCH), :] = h
            out.append(h)
        return tuple(out)

    hs = tuple(cmats)
    for it in range(seg):
        hs = final(it, hs)
    for n in range(nt):
        for sg in range(SUBLANES):
            h_ref[sg * seg:(sg + 1) * seg, n * LANES:(n + 1) * LANES] = (
                hp_pad_ref[n, sg * RG_HPITCH:sg * RG_HPITCH + seg, :].astype(h_ref.dtype))


def _rg_scan(xr, cw, cb, w, ba, bx, lam, z, nb_lat, nb_real):
    t, dr = xr.shape
    per = TB // SUBLANES
    n8 = t // SUBLANES
    rev = z == 1
    blk = lambda s: _block_order(s, nb_lat, nb_real, rev)
    return pl.pallas_call(
        functools.partial(_rg_scan_kernel, rev=rev, nb_lat=nb_lat, nb_real=nb_real),
        grid=(t // TB,),
        in_specs=[pl.BlockSpec((TB, dr), lambda s: (blk(s), 0)),
                  pl.BlockSpec((SUBLANES, dr), lambda s: (jnp.maximum(blk(s) * per - 1, 0), 0)),
                  pl.BlockSpec((SUBLANES, dr), lambda s: (jnp.minimum((blk(s) + 1) * per, n8 - 1), 0)),
                  _full_spec(cw.shape), _full_spec(cb.shape), _pick_spec(w.shape, (z,)),
                  _pick_spec(ba.shape, (z,)), _pick_spec(bx.shape, (z,)), _pick_spec(lam.shape, (z,))],
        out_specs=pl.BlockSpec((TB, dr), lambda s: (blk(s), 0)),
        out_shape=jax.ShapeDtypeStruct((t, dr), BF16),
        scratch_shapes=[pltpu.VMEM((TB + 2 * SUBLANES, dr), F32),
                        pltpu.VMEM((dr // LANES, TB, LANES), F32), pltpu.VMEM((dr // LANES, TB, LANES), F32),
                        pltpu.VMEM((dr // LANES, SUBLANES * RG_XPITCH, LANES), F32),
                        pltpu.VMEM((dr // LANES, SUBLANES * RG_HPITCH, LANES), F32),
                        pltpu.VMEM((1, dr), F32)],
        compiler_params=_params(),
        name="rg_scan_bwd" if rev else "rg_scan_fwd",
    )(xr, xr, xr, cw, cb, w, ba, bx, lam)


def _rg_out(hf_ref, hb_ref, gate_ref):
    return ((hf_ref[...].astype(F32) + hb_ref[...].astype(F32)) * gate_ref[...].astype(F32)).astype(BF16)


def _rg_layer(xs, mods, norm_g, w1, w2, layer, w_in, conv_w, conv_b, w_a, b_a, w_x, b_x, lam, w_out, n_lat, n_tot,
              n_out_tiles, nb_lat, nb_real, final_g):
    w_in = w_in.astype(BF16)
    x1, gate, xr = _pre_call("pre_rg", _rg_proj, xs, mods, norm_g, w1, w2, layer, n_lat, n_tot, [w_in],
                             [_full_spec(w_in.shape)], [D_RNN, D_RNN], [BF16, F32])
    w = (0.5 * jnp.concatenate([w_a, w_x], axis=-1)).astype(BF16)
    hs = [_rg_scan(xr, conv_w, conv_b.reshape(1, D_RNN), w, (0.5 * b_a).reshape(2, 1, D_RNN),
                   (0.5 * b_x).reshape(2, 1, D_RNN), lam.reshape(2, 1, D_RNN), z, nb_lat, nb_real) for z in range(2)]
    w_out = w_out.astype(BF16)
    return _post_call("post_rg", _rg_out, x1, mods, norm_g, w1, w2, layer, n_lat, n_out_tiles,
                      [hs[0], hs[1], gate, w_out], [_tile_spec(D_RNN, FFN_WARM)] * 3 + [_full_spec(w_out.shape)],
                      final_g)


def _swa_proj(h, w_ref, cos_ref, sin_ref, q_ref, kz_ref, vz_ref):
    y = _dot(h, w_ref[...])
    nq = SWA_HEADS * HEAD_DIM
    nk = SWA_KV * HEAD_DIM
    cos = cos_ref[...]
    sin = sin_ref[...]
    rot = ROT_AXIS // 2
    lane = lax.broadcasted_iota(jnp.int32, (1, LANES), 1)
    first = (lane % ROT_AXIS) < rot

    def rope(xg):
        partner = jnp.where(first, pltpu.roll(xg, LANES - rot, axis=1), pltpu.roll(xg, rot, axis=1))
        return xg * cos + partner * sin

    q = y[:, :nq] * (HEAD_DIM ** -0.5 * LOG2E)
    for gi in range(nq // LANES):
        q_ref[:, gi * LANES:(gi + 1) * LANES] = rope(q[:, gi * LANES:(gi + 1) * LANES]).astype(BF16)
    low = lane < HEAD_DIM
    for gi in range(nk // LANES):
        kg = rope(y[:, nq + gi * LANES:nq + (gi + 1) * LANES])
        vg = y[:, nq + nk + gi * LANES:nq + nk + (gi + 1) * LANES]
        for par_j in range(2):
            j = 2 * gi + par_j
            for par_q in range(2):
                src_k = kg if par_q == par_j else pltpu.roll(kg, HEAD_DIM, axis=1)
                src_v = vg if par_q == par_j else pltpu.roll(vg, HEAD_DIM, axis=1)
                keep = low if par_q == 0 else jnp.logical_not(low)
                cs = slice((2 * j + par_q) * LANES, (2 * j + par_q + 1) * LANES)
                kz_ref[:, cs] = jnp.where(keep, src_k, 0.0).astype(BF16)
                vz_ref[:, cs] = jnp.where(keep, src_v, 1.0).astype(BF16)


def _swa_kernel(sink_ref, q_ref, kp_ref, k0_ref, kn_ref, vp_ref, v0_ref, vn_ref, kc_ref, vc_ref, o_ref,
                kz_ref, vz_ref, *, nb):
    qb = q_ref.shape[0]
    c_len = kc_ref.shape[0]
    i = pl.program_id(0)

    @pl.when(i == 0)
    def _():
        kz_ref[0:c_len, :] = kc_ref[...]
        vz_ref[0:c_len, :] = vc_ref[...]

    row = lax.broadcasted_iota(jnp.int32, (2 * qb, 1), 0)
    low = lax.broadcasted_iota(jnp.int32, (1, LANES), 1) < HEAD_DIM

    def attend(n_keys, mask):
        for j in range(SWA_KV):
            qs = jnp.concatenate([q_ref[:, (2 * j) * LANES:(2 * j + 1) * LANES],
                                  q_ref[:, (2 * j + 1) * LANES:(2 * j + 2) * LANES]], axis=0)
            lo = slice((2 * j) * LANES, (2 * j + 1) * LANES)
            hi = slice((2 * j + 1) * LANES, (2 * j + 2) * LANES)
            s_all = _dot_nt(qs, jnp.concatenate([kz_ref[0:n_keys, lo], kz_ref[0:n_keys, hi]], axis=0))
            ps, sinks, ms = [], [], []
            for par in range(2):
                s = mask(s_all[:, par * n_keys:(par + 1) * n_keys])
                sink = LOG2E * jnp.where(row < qb, sink_ref[j * SWA_G + par], sink_ref[j * SWA_G + 2 + par])
                m = jnp.maximum(jnp.max(s, axis=-1, keepdims=True), sink)
                ps.append(jnp.exp2(s - m).astype(BF16))
                sinks.append(sink)
                ms.append(m)
            r_all = _dot(jnp.concatenate(ps, axis=0), vz_ref[0:n_keys, (2 * j) * LANES:(2 * j + 2) * LANES])
            halves = []
            for par in range(2):
                r = r_all[par * 2 * qb:(par + 1) * 2 * qb, par * LANES:(par + 1) * LANES]
                halves.append(r / (pltpu.roll(r, HEAD_DIM, axis=1) + jnp.exp2(sinks[par] - ms[par])))
            o = jnp.where(low, halves[0], halves[1]).astype(BF16)
            o_ref[:, (2 * j) * LANES:(2 * j + 1) * LANES] = o[0:qb]
            o_ref[:, (2 * j + 1) * LANES:(2 * j + 2) * LANES] = o[qb:]

    @pl.when(i < nb)
    def _():
        for b, (k_ref, v_ref) in enumerate(((kp_ref, vp_ref), (k0_ref, v0_ref), (kn_ref, vn_ref))):
            kz_ref[c_len + b * qb:c_len + (b + 1) * qb, :] = k_ref[...]
            vz_ref[c_len + b * qb:c_len + (b + 1) * qb, :] = v_ref[...]
        r_i = lax.broadcasted_iota(jnp.int32, (2 * qb, qb), 0) % qb
        c_i = lax.broadcasted_iota(jnp.int32, (2 * qb, qb), 1)
        b_prev = jnp.where(jnp.logical_and(c_i >= r_i, i > 0), 0.0, NEG)
        b_next = jnp.where(jnp.logical_and(c_i <= r_i, i < nb - 1), 0.0, NEG)

        def mask(s):
            return jnp.concatenate([s[:, :c_len], s[:, c_len:c_len + qb] + b_prev,
                                    s[:, c_len + qb:c_len + 2 * qb], s[:, c_len + 2 * qb:] + b_next], axis=1)

        attend(c_len + 3 * qb, mask)

    @pl.when(i >= nb)
    def _():
        attend(c_len, lambda s: s)


def _swa_attn(q, kz, vz, sink, s_len, c_len):
    t, nq = q.shape
    assert WINDOW == Q_BLOCK and s_len % Q_BLOCK == 0 and s_len % c_len == 0
    nz = kz.shape[1]
    nb = s_len // Q_BLOCK
    n_blk = t // Q_BLOCK
    smem = pl.BlockSpec(memory_space=pltpu.SMEM)
    ctx_spec = pl.BlockSpec((c_len, nz), lambda i: (s_len // c_len, 0))
    band = [pl.BlockSpec((Q_BLOCK, nz), lambda i: (jnp.maximum(i - 1, 0), 0)),
            pl.BlockSpec((Q_BLOCK, nz), lambda i: (i, 0)),
            pl.BlockSpec((Q_BLOCK, nz), lambda i: (jnp.minimum(i + 1, n_blk - 1), 0))]
    n_keys = c_len + 3 * Q_BLOCK
    return pl.pallas_call(
        functools.partial(_swa_kernel, nb=nb),
        grid=(n_blk,),
        in_specs=[smem, pl.BlockSpec((Q_BLOCK, nq), lambda i: (i, 0))] + band + band + [ctx_spec, ctx_spec],
        out_specs=pl.BlockSpec((Q_BLOCK, nq), lambda i: (i, 0)),
        out_shape=jax.ShapeDtypeStruct((t, nq), BF16),
        scratch_shapes=[pltpu.VMEM((n_keys, nz), BF16), pltpu.VMEM((n_keys, nz), BF16)],
        compiler_params=_params(),
        name="swa_attn",
    )(sink, q, kz, kz, kz, vz, vz, vz, kz, vz)


def _rope_tables(s_len, c_len):
    rows = s_len // GRID_W
    freqs = ROPE_BASE ** (-jnp.arange(0, ROT_AXIS, 2, dtype=F32) / ROT_AXIS)
    dim = jnp.arange(LANES) % HEAD_DIM
    by_row = (dim // ROT_AXIS) == 0
    lane_freq = freqs[dim % (ROT_AXIS // 2)]
    sign = jnp.where((dim % ROT_AXIS) < ROT_AXIS // 2, -1.0, 1.0)
    ang_r = jnp.arange(rows, dtype=F32)[:, None] * lane_freq
    ang_c = jnp.arange(GRID_W, dtype=F32)[:, None] * lane_freq

    def table(fn, scale, ctx_fill):
        lat = jnp.where(by_row, (fn(ang_r) * scale)[:, None, :], (fn(ang_c) * scale)[None, :, :])
        return jnp.concatenate([lat.reshape(s_len, LANES), jnp.full((c_len, LANES), ctx_fill, F32)], axis=0)

    return table(jnp.cos, 1.0, 1.0), table(jnp.sin, sign, 0.0)


def _swa_out(o_ref):
    return o_ref[...]


def _swa_layer(xs, mods, norm_g, w1, w2, layer, w_qkv, sink, w_out, rope, n_lat, n_tot, n_out_tiles, s_len, c_len,
               final_g):
    w_qkv = w_qkv.astype(BF16)
    nq = SWA_HEADS * HEAD_DIM
    nz = 2 * SWA_KV * LANES
    x1, q, kz, vz = _pre_call("pre_swa", _swa_proj, xs, mods, norm_g, w1, w2, layer, n_lat, n_tot,
                              [w_qkv, rope[0], rope[1]],
                              [_full_spec(w_qkv.shape), _tile_spec(LANES, FFN_WARM), _tile_spec(LANES, FFN_WARM)],
                              [nq, nz, nz], [BF16, BF16, BF16])
    o = _swa_attn(q, kz, vz, sink, s_len, c_len)
    w_out = w_out.astype(BF16)
    return _post_call("post_swa", _swa_out, x1, mods, norm_g, w1, w2, layer, n_lat, n_out_tiles, [o, w_out],
                      [_tile_spec(nq, FFN_WARM), _full_spec(w_out.shape)], final_g)


def kernel(x, c, ctx, c_ctx, ada_w, ada_b, norm_g, final_g, ffn_w_in, ffn_w_out, gla_w_in, gla_w_a2, gla_b_a, gla_norm_g, gla_w_out, rg_w_in, rg_conv_w, rg_conv_b, rg_w_a, rg_b_a, rg_w_x, rg_b_x, rg_lam, rg_w_out, swa_w_qkv, swa_sink, swa_w_out):
    b, s_len, d = x.shape
    c_len = ctx.shape[1]
    assert b == 1 and s_len % TM == 0 and c_len % TB == 0 and s_len % c_len == 0
    depth = ada_w.shape[0]
    n_lat = s_len // TM
    n_tot = n_lat + -(-c_len // TM)
    nb_lat = s_len // TB
    nb_real = nb_lat + c_len // TB
    c_rows = (n_tot - n_lat) * TM
    mods = _ada_vectors(c, c_ctx, ada_w, ada_b)
    ng = norm_g.reshape(depth * 3, 1, d)
    w1, w2 = ffn_w_in, ffn_w_out
    gw = _gla_weights(gla_w_in, gla_w_a2, gla_b_a, gla_norm_g, gla_w_out)
    rope = _rope_tables(s_len, c_rows)
    xs = (x[0], jnp.pad(ctx[0], ((0, c_rows - c_len), (0, 0))))
    for i in range(depth):
        m, j = i % N_MIXERS, i // N_MIXERS
        last = i == depth - 1
        n_after = n_lat if last else n_tot
        fin = final_g if last else None
        if m == 0:
            xs = _gla_layer(xs, mods, ng, w1, w2, i, gw, j, n_lat, n_tot, n_after, nb_lat, nb_real, fin)
        elif m == 1:
            xs = _rg_layer(xs, mods, ng, w1, w2, i, rg_w_in[j], rg_conv_w[j], rg_conv_b[j], rg_w_a[j], rg_b_a[j],
                           rg_w_x[j], rg_b_x[j], rg_lam[j], rg_w_out[j], n_lat, n_tot, n_after, nb_lat, nb_real, fin)
        else:
            xs = _swa_layer(xs, mods, ng, w1, w2, i, swa_w_qkv[j], swa_sink[j], swa_w_out[j], rope, n_lat, n_tot,
                            n_after, s_len, c_len, fin)
    return xs[None]
```
